```python
import math
import jax
import jax.numpy as jnp
from jax import lax
import numpy as np

D_MODEL = 1024
BATCH = 4
SEQ = 8192
DEPTH = 2

CTX_LEN = 256
GRID_W = 64
HEAD_DIM = 64
NA_HEADS = 4
NA_ROWS = 8
NA_COLS = 16
SWA_HEADS = 4
SWA_KV_HEADS = 2
SWA_WINDOW = 128
SWA_BLOCK = 128
GMLP_WIDTH = 256
GMLP_GROUPS = 4
GMLP_CHUNK = 128
S5_WIDTH = 256
S5_GROUP = 16
S5_GROUPS = S5_WIDTH // S5_GROUP
S5_STATE = 64
N_BRANCH = 4
BRANCH_WIDTH = 256
FFN_DIM = 2816
N_EXPERTS = 8
TOP_K = 2
EXPERT_DIM = 3584
N_DENSE = (DEPTH + 1) // 2
N_MOE = DEPTH // 2
ROPE_BASE = 10000.0
EPS = 1e-6
NEG_INF = -1e30
F32 = jnp.float32

COL_SIZES = (NA_HEADS * HEAD_DIM, NA_HEADS * HEAD_DIM, NA_HEADS * HEAD_DIM,
             SWA_HEADS * HEAD_DIM, SWA_KV_HEADS * HEAD_DIM, SWA_KV_HEADS * HEAD_DIM,
             GMLP_WIDTH, GMLP_WIDTH, S5_WIDTH, N_BRANCH * D_MODEL)
COL_SPLITS = tuple(int(v) for v in np.cumsum(COL_SIZES)[:-1])
IN_COLS = int(sum(COL_SIZES))

kernel_name = 'hybrid_gated_mixers_diffusion_block'


def rmsnorm(x, g):
    x32 = x.astype(F32)
    y = x32 * lax.rsqrt(jnp.mean(x32 * x32, axis=-1, keepdims=True) + EPS)
    return (y * g.astype(F32)).astype(x.dtype)


def layernorm(x, g, b):
    x32 = x.astype(F32)
    mu = jnp.mean(x32, axis=-1, keepdims=True)
    xc = x32 - mu
    y = xc * lax.rsqrt(jnp.mean(xc * xc, axis=-1, keepdims=True) + EPS)
    return (y * g.astype(F32) + b.astype(F32)).astype(x.dtype)


def heads(t, n):
    return t.reshape(t.shape[0], t.shape[1], n, HEAD_DIM)


def axial_rope(x, rows, cols):
    nq = HEAD_DIM // 4
    inv = ROPE_BASE ** (-jnp.arange(nq, dtype=F32) / nq)

    def rot(xp, pos):
        ang = pos.astype(F32)[:, None] * inv[None, :]
        cos = jnp.cos(ang)[None, :, None, :]
        sin = jnp.sin(ang)[None, :, None, :]
        x1, x2 = xp[..., :nq], xp[..., nq:]
        return jnp.concatenate([x1 * cos - x2 * sin, x1 * sin + x2 * cos], axis=-1)

    half = HEAD_DIM // 2
    return jnp.concatenate([rot(x[..., :half], rows), rot(x[..., half:], cols)], axis=-1).astype(x.dtype)


def ctx_attn(q, k, v):
    b, l, h, hd = q.shape
    s = jnp.einsum('blhd,bmhd->bhlm', q, k).astype(F32) * (HEAD_DIM ** -0.5)
    p = jax.nn.softmax(s, axis=-1).astype(v.dtype)
    return jnp.einsum('bhlm,bmhd->blhd', p, v).reshape(b, l, h * hd)


def neighbourhood_attn(q, k, v, kc, vc, rpb):
    b, s, h, hd = q.shape
    rows = s // GRID_W
    wr = min(NA_ROWS, rows)
    n_win = wr * NA_COLS
    scale = HEAD_DIM ** -0.5
    qg = q.reshape(b, rows, GRID_W, h, hd)
    kg = k.reshape(b, rows, GRID_W, h, hd)
    vg = v.reshape(b, rows, GRID_W, h, hd)
    col = jnp.arange(GRID_W)
    col_start = jnp.clip(col - NA_COLS // 2, 0, GRID_W - NA_COLS)
    col_idx = col_start[:, None] + jnp.arange(NA_COLS)[None, :]
    bias_c = rpb.astype(F32)[:, :, col_idx - col[:, None] + NA_COLS - 1]

    def row_block(r):
        rs = jnp.clip(r - wr // 2, 0, rows - wr)
        q_r = lax.dynamic_index_in_dim(qg, r, axis=1, keepdims=False)
        k_w = lax.dynamic_slice_in_dim(kg, rs, wr, axis=1)[:, :, col_idx]
        v_w = lax.dynamic_slice_in_dim(vg, rs, wr, axis=1)[:, :, col_idx]
        row_off = rs + jnp.arange(wr) - r + NA_ROWS - 1
        bias = jnp.transpose(bias_c[:, row_off], (0, 2, 1, 3))
        s_w = jnp.einsum('bchd,brckhd->bhcrk', q_r, k_w).astype(F32) * scale + bias[None]
        s_c = jnp.einsum('bchd,blhd->bhcl', q_r, kc).astype(F32) * scale
        p = jax.nn.softmax(jnp.concatenate([s_w.reshape(b, h, GRID_W, n_win), s_c], axis=-1), axis=-1)
        p_w = p[..., :n_win].reshape(b, h, GRID_W, wr, NA_COLS).astype(v.dtype)
        p_c = p[..., n_win:].astype(v.dtype)
        return jnp.einsum('bhcrk,brckhd->bchd', p_w, v_w) + jnp.einsum('bhcl,blhd->bchd', p_c, vc)

    out = lax.map(row_block, jnp.arange(rows))
    return jnp.transpose(out, (1, 0, 2, 3, 4)).reshape(b, s, h * hd)


def window_gqa(q, k, v, kc, vc, sink):
    b, s, hq, hd = q.shape
    hkv = k.shape[2]
    g = hq // hkv
    nb = s // SWA_BLOCK
    scale = HEAD_DIM ** -0.5
    qb = q.reshape(b, nb, SWA_BLOCK, hkv, g, hd)
    pad = ((0, 0), (SWA_BLOCK, SWA_BLOCK), (0, 0), (0, 0))
    kp = jnp.pad(k, pad).reshape(b, nb + 2, SWA_BLOCK, hkv, hd)
    vp = jnp.pad(v, pad).reshape(b, nb + 2, SWA_BLOCK, hkv, hd)
    k_band = jnp.concatenate([kp[:, :-2], kp[:, 1:-1], kp[:, 2:]], axis=2)
    v_band = jnp.concatenate([vp[:, :-2], vp[:, 1:-1], vp[:, 2:]], axis=2)
    n_band = 3 * SWA_BLOCK
    blk = jnp.arange(nb)[:, None, None]
    qpos = blk * SWA_BLOCK + jnp.arange(SWA_BLOCK)[None, :, None]
    kpos = (blk - 1) * SWA_BLOCK + jnp.arange(n_band)[None, None, :]
    mask = (jnp.abs(kpos - qpos) <= SWA_WINDOW) & (kpos >= 0) & (kpos < s)
    s_w = jnp.einsum('bnqhgd,bnkhd->bnhgqk', qb, k_band).astype(F32) * scale
    s_w = jnp.where(mask[None, :, None, None], s_w, NEG_INF)
    s_c = jnp.einsum('bnqhgd,blhd->bnhgql', qb, kc).astype(F32) * scale
    sink_col = jnp.broadcast_to(sink.astype(F32).reshape(hkv, g)[None, None, :, :, None, None],
                                (b, nb, hkv, g, SWA_BLOCK, 1))
    p = jax.nn.softmax(jnp.concatenate([s_w, s_c, sink_col], axis=-1), axis=-1)
    n_ctx = kc.shape[1]
    p_w = p[..., :n_band].astype(v.dtype)
    p_c = p[..., n_band:n_band + n_ctx].astype(v.dtype)
    o = jnp.einsum('bnhgqk,bnkhd->bnqhgd', p_w, v_band) + jnp.einsum('bnhgql,blhd->bnqhgd', p_c, vc)
    return o.reshape(b, s, hq * hd)


def ctx_gqa(qc, kc, vc, sink):
    b, l, hq, hd = qc.shape
    hkv = kc.shape[2]
    g = hq // hkv
    qg = qc.reshape(b, l, hkv, g, hd)
    s = jnp.einsum('blhgd,bmhd->bhglm', qg, kc).astype(F32) * (HEAD_DIM ** -0.5)
    sink_col = jnp.broadcast_to(sink.astype(F32).reshape(hkv, g)[None, :, :, None, None], (b, hkv, g, l, 1))
    p = jax.nn.softmax(jnp.concatenate([s, sink_col], axis=-1), axis=-1)[..., :l].astype(vc.dtype)
    return jnp.einsum('bhglm,bmhd->blhgd', p, vc).reshape(b, l, hq * hd)


def chunk_gmlp(u, v, ln_g, ln_b, ws, bs):
    b, n, _ = u.shape
    u = jax.nn.gelu(u)
    v = layernorm(jax.nn.gelu(v), ln_g, ln_b)
    vg = v.reshape(b, n // GMLP_CHUNK, GMLP_CHUNK, GMLP_GROUPS, GMLP_WIDTH // GMLP_GROUPS)
    sg = jnp.einsum('gij,bnjgc->bnigc', ws, vg) + bs.T[None, None, :, :, None]
    return u * sg.reshape(b, n, GMLP_WIDTH)


def s5_discretize(a_re, a_im, log_dt, b_re, b_im):
    lam_re = jnp.minimum(a_re.astype(F32), -1e-4)
    lam_im = a_im.astype(F32)
    dt = jnp.exp(log_dt.astype(F32))[:, None]
    mag = jnp.exp(lam_re * dt)
    ab_re = mag * jnp.cos(lam_im * dt)
    ab_im = mag * jnp.sin(lam_im * dt)
    den = lam_re * lam_re + lam_im * lam_im
    k_re = ((ab_re - 1.0) * lam_re + ab_im * lam_im) / den
    k_im = (ab_im * lam_re - (ab_re - 1.0) * lam_im) / den
    br, bi = b_re.astype(F32), b_im.astype(F32)
    bb_re = k_re[..., None] * br - k_im[..., None] * bi
    bb_im = k_re[..., None] * bi + k_im[..., None] * br
    return ab_re, ab_im, bb_re, bb_im


def s5_scan(u, ab_re, ab_im, bb_re, bb_im, h0_re, h0_im, reverse):
    bu_re = jnp.einsum('bngc,gpc->bngp', u, bb_re)
    bu_im = jnp.einsum('bngc,gpc->bngp', u, bb_im)
    if h0_re is not None:
        first = -1 if reverse else 0
        bu_re = bu_re.at[:, first].add(ab_re * h0_re - ab_im * h0_im)
        bu_im = bu_im.at[:, first].add(ab_re * h0_im + ab_im * h0_re)
    a_re = jnp.broadcast_to(ab_re, bu_re.shape)
    a_im = jnp.broadcast_to(ab_im, bu_im.shape)

    def combine(e1, e2):
        a1r, a1i, b1r, b1i = e1
        a2r, a2i, b2r, b2i = e2
        return (a2r * a1r - a2i * a1i, a2r * a1i + a2i * a1r,
                a2r * b1r - a2i * b1i + b2r, a2r * b1i + a2i * b1r + b2i)

    _, _, x_re, x_im = lax.associative_scan(combine, (a_re, a_im, bu_re, bu_im), reverse=reverse, axis=1)
    return x_re, x_im


def s5_readout(x_re, x_im, c_re, c_im):
    return (jnp.einsum('bngp,gcp->bngc', x_re, c_re.astype(F32))
            - jnp.einsum('bngp,gcp->bngc', x_im, c_im.astype(F32)))


def s5_glu(y, w, b):
    y = jax.nn.gelu(y)
    return y * jax.nn.sigmoid(y @ w + b)


def s5_mixer(ux, uc, a_re, a_im, log_dt, b_re, b_im, c_re, c_im, d, glu_w, glu_b, ctx_out):
    bsz, s, _ = ux.shape
    l = uc.shape[1]
    ug_x = ux.astype(F32).reshape(bsz, s, S5_GROUPS, S5_GROUP)
    ug_c = uc.astype(F32).reshape(bsz, l, S5_GROUPS, S5_GROUP)
    y_x = d.astype(F32) * ux.astype(F32)
    y_c = d.astype(F32) * uc.astype(F32)
    for direction, rev in enumerate((False, True)):
        ab_re, ab_im, bb_re, bb_im = s5_discretize(a_re[direction], a_im[direction], log_dt[direction],
                                                   b_re[direction], b_im[direction])
        xc_re, xc_im = s5_scan(ug_c, ab_re, ab_im, bb_re, bb_im, None, None, rev)
        end = 0 if rev else -1
        xx_re, xx_im = s5_scan(ug_x, ab_re, ab_im, bb_re, bb_im, xc_re[:, end], xc_im[:, end], rev)
        y_x = y_x + s5_readout(xx_re, xx_im, c_re[direction], c_im[direction]).reshape(bsz, s, S5_WIDTH)
        if ctx_out:
            y_c = y_c + s5_readout(xc_re, xc_im, c_re[direction], c_im[direction]).reshape(bsz, l, S5_WIDTH)
    out_x = s5_glu(y_x.astype(ux.dtype), glu_w, glu_b)
    out_c = s5_glu(y_c.astype(uc.dtype), glu_w, glu_b) if ctx_out else None
    return out_x, out_c


def merge_branches(outs, gates, w_branch, w_out):
    dm = w_out.shape[0]
    m = None
    for i, o in enumerate(outs):
        term = jax.nn.sigmoid(gates[..., i * dm:(i + 1) * dm]) * (o @ w_branch[i])
        m = term if m is None else m + term
    return m @ w_out


def token_mixers(zx, zc, rows_pos, cols_pos, rpb, sink, gln_g, gln_b, gws, gbs,
                 a_re, a_im, log_dt, b_re, b_im, c_re, c_im, d, glu_w, glu_b, w_branch, w_out, ctx_out):
    qa, ka, va, qd, kd, vd, gu, gv, su, gates = jnp.split(zx, COL_SPLITS, axis=-1)
    qa_c, ka_c, va_c, qd_c, kd_c, vd_c, gu_c, gv_c, su_c, gates_c = jnp.split(zc, COL_SPLITS, axis=-1)
    ka_ch, va_ch = heads(ka_c, NA_HEADS), heads(va_c, NA_HEADS)
    kd_ch, vd_ch = heads(kd_c, SWA_KV_HEADS), heads(vd_c, SWA_KV_HEADS)
    o_a = neighbourhood_attn(heads(qa, NA_HEADS), heads(ka, NA_HEADS), heads(va, NA_HEADS), ka_ch, va_ch, rpb)
    o_b = chunk_gmlp(gu, gv, gln_g, gln_b, gws, gbs)
    o_cx, o_cc = s5_mixer(su, su_c, a_re, a_im, log_dt, b_re, b_im, c_re, c_im, d, glu_w, glu_b, ctx_out)
    q_d = axial_rope(heads(qd, SWA_HEADS), rows_pos, cols_pos)
    k_d = axial_rope(heads(kd, SWA_KV_HEADS), rows_pos, cols_pos)
    o_d = window_gqa(q_d, k_d, heads(vd, SWA_KV_HEADS), kd_ch, vd_ch, sink)
    out_x = merge_branches((o_a, o_b, o_cx, o_d), gates, w_branch, w_out)
    out_c = None
    if ctx_out:
        o_a_c = ctx_attn(heads(qa_c, NA_HEADS), ka_ch, va_ch)
        o_b_c = chunk_gmlp(gu_c, gv_c, gln_g, gln_b, gws, gbs)
        o_d_c = ctx_gqa(heads(qd_c, SWA_HEADS), kd_ch, vd_ch, sink)
        out_c = merge_branches((o_a_c, o_b_c, o_cc, o_d_c), gates_c, w_branch, w_out)
    return out_x, out_c


def swiglu(h, w1, w3, w2):
    return (jax.nn.silu(h @ w1) * (h @ w3)) @ w2


def moe_ffn(h, router, w1, w3, w2):
    shp = h.shape
    hf = h.reshape(-1, shp[-1])
    logits = (hf @ router).astype(F32)
    top_v, top_i = lax.top_k(logits, TOP_K)
    wts = jax.nn.softmax(top_v, axis=-1)
    comb = jnp.sum(jax.nn.one_hot(top_i, N_EXPERTS, dtype=F32) * wts[..., None], axis=1).astype(h.dtype)
    y = None
    for e in range(N_EXPERTS):
        term = comb[:, e:e + 1] * swiglu(hf, w1[e], w3[e], w2[e])
        y = term if y is None else y + term
    return y.reshape(shp)


def setup_inputs(seed: int = 0) -> dict:
    key = jax.random.key(seed)
    ks = iter(jax.random.split(key, 48))
    dm = D_MODEL

    def nrm(shape, scale):
        return jax.random.normal(next(ks), shape, F32) * scale

    n_idx = jnp.arange(S5_STATE, dtype=F32)
    return {
        'x': nrm((BATCH, SEQ, dm), 1.0),
        'c': nrm((BATCH, dm), 1.0),
        'ctx': nrm((BATCH, CTX_LEN, dm), 1.0),
        'c_ctx': nrm((dm,), 1.0),
        'w_mod': nrm((DEPTH, dm, 6 * dm), 0.5 * dm ** -0.5),
        'b_mod': nrm((DEPTH, 6 * dm), 0.02),
        'g_pre_mix': 1.0 + nrm((DEPTH, dm), 0.02),
        'g_post_mix': 1.0 + nrm((DEPTH, dm), 0.02),
        'g_pre_ffn': 1.0 + nrm((DEPTH, dm), 0.02),
        'g_post_ffn': 1.0 + nrm((DEPTH, dm), 0.02),
        'w_in': nrm((DEPTH, dm, IN_COLS), dm ** -0.5),
        'na_rpb': nrm((DEPTH, NA_HEADS, 2 * NA_ROWS - 1, 2 * NA_COLS - 1), 0.1),
        'swa_sink': nrm((DEPTH, SWA_HEADS), 0.5),
        'gmlp_ln_g': 1.0 + nrm((DEPTH, GMLP_WIDTH), 0.02),
        'gmlp_ln_b': nrm((DEPTH, GMLP_WIDTH), 0.02),
        'gmlp_ws': nrm((DEPTH, GMLP_GROUPS, GMLP_CHUNK, GMLP_CHUNK), GMLP_CHUNK ** -0.5),
        'gmlp_bs': 1.0 + nrm((DEPTH, GMLP_GROUPS, GMLP_CHUNK), 0.02),
        's5_a_re': -0.5 + nrm((DEPTH, 2, S5_GROUPS, S5_STATE), 0.01),
        's5_a_im': math.pi * n_idx + nrm((DEPTH, 2, S5_GROUPS, S5_STATE), 0.01),
        's5_log_dt': jax.random.uniform(next(ks), (DEPTH, 2, S5_GROUPS), F32, math.log(1e-3), math.log(1e-1)),
        's5_b_re': nrm((DEPTH, 2, S5_GROUPS, S5_STATE, S5_GROUP), (2 * S5_GROUP) ** -0.5),
        's5_b_im': nrm((DEPTH, 2, S5_GROUPS, S5_STATE, S5_GROUP), (2 * S5_GROUP) ** -0.5),
        's5_c_re': nrm((DEPTH, 2, S5_GROUPS, S5_GROUP, S5_STATE), S5_STATE ** -0.5),
        's5_c_im': nrm((DEPTH, 2, S5_GROUPS, S5_GROUP, S5_STATE), S5_STATE ** -0.5),
        's5_d': nrm((DEPTH, S5_WIDTH), 0.5),
        's5_glu_w': nrm((DEPTH, S5_WIDTH, S5_WIDTH), S5_WIDTH ** -0.5),
        's5_glu_b': nrm((DEPTH, S5_WIDTH), 0.02),
        'w_branch': nrm((DEPTH, N_BRANCH, BRANCH_WIDTH, dm), BRANCH_WIDTH ** -0.5),
        'w_out': nrm((DEPTH, dm, dm), dm ** -0.5),
        'ffn_w1': nrm((N_DENSE, dm, FFN_DIM), dm ** -0.5),
        'ffn_w3': nrm((N_DENSE, dm, FFN_DIM), dm ** -0.5),
        'ffn_w2': nrm((N_DENSE, FFN_DIM, dm), FFN_DIM ** -0.5),
        'moe_router': nrm((N_MOE, dm, N_EXPERTS), dm ** -0.5),
        'moe_w1': nrm((N_MOE, N_EXPERTS, dm, EXPERT_DIM), dm ** -0.5),
        'moe_w3': nrm((N_MOE, N_EXPERTS, dm, EXPERT_DIM), dm ** -0.5),
        'moe_w2': nrm((N_MOE, N_EXPERTS, EXPERT_DIM, dm), EXPERT_DIM ** -0.5),
    }


def reference(x, c, ctx, c_ctx, w_mod, b_mod, g_pre_mix, g_post_mix, g_pre_ffn, g_post_ffn, w_in,
              na_rpb, swa_sink, gmlp_ln_g, gmlp_ln_b, gmlp_ws, gmlp_bs,
              s5_a_re, s5_a_im, s5_log_dt, s5_b_re, s5_b_im, s5_c_re, s5_c_im, s5_d, s5_glu_w, s5_glu_b,
              w_branch, w_out, ffn_w1, ffn_w3, ffn_w2, moe_router, moe_w1, moe_w3, moe_w2):
    bsz, s, dm = x.shape
    t = jnp.arange(s)
    rows_pos = t // GRID_W
    cols_pos = t % GRID_W
    h_ctx = ctx
    for i in range(DEPTH):
        ctx_out = i < DEPTH - 1
        mod_x = (jax.nn.silu(c) @ w_mod[i] + b_mod[i]).reshape(bsz, 6, 1, dm)
        mod_c = (jax.nn.silu(c_ctx) @ w_mod[i] + b_mod[i]).reshape(6, dm)
        hx = rmsnorm(x, g_pre_mix[i]) * (1.0 + mod_x[:, 1]) + mod_x[:, 0]
        hc = rmsnorm(h_ctx, g_pre_mix[i]) * (1.0 + mod_c[1]) + mod_c[0]
        mx, mc = token_mixers(hx @ w_in[i], hc @ w_in[i], rows_pos, cols_pos, na_rpb[i], swa_sink[i],
                              gmlp_ln_g[i], gmlp_ln_b[i], gmlp_ws[i], gmlp_bs[i],
                              s5_a_re[i], s5_a_im[i], s5_log_dt[i], s5_b_re[i], s5_b_im[i],
                              s5_c_re[i], s5_c_im[i], s5_d[i], s5_glu_w[i], s5_glu_b[i],
                              w_branch[i], w_out[i], ctx_out)
        x = x + mod_x[:, 2] * rmsnorm(mx, g_post_mix[i])
        if ctx_out:
            h_ctx = h_ctx + mod_c[2] * rmsnorm(mc, g_post_mix[i])
        j = i // 2
        hx = rmsnorm(x, g_pre_ffn[i]) * (1.0 + mod_x[:, 4]) + mod_x[:, 3]
        if i % 2 == 0:
            fx = swiglu(hx, ffn_w1[j], ffn_w3[j], ffn_w2[j])
        else:
            fx = moe_ffn(hx, moe_router[j], moe_w1[j], moe_w3[j], moe_w2[j])
        x = x + mod_x[:, 5] * rmsnorm(fx, g_post_ffn[i])
        if ctx_out:
            hc = rmsnorm(h_ctx, g_pre_ffn[i]) * (1.0 + mod_c[4]) + mod_c[3]
            if i % 2 == 0:
                fc = swiglu(hc, ffn_w1[j], ffn_w3[j], ffn_w2[j])
            else:
                fc = moe_ffn(hc, moe_router[j], moe_w1[j], moe_w3[j], moe_w2[j])
            h_ctx = h_ctx + mod_c[5] * rmsnorm(fc, g_post_ffn[i])
    return x
```

```python
import functools
import math

import numpy as np
import jax
import jax.numpy as jnp
from jax import lax
from jax.experimental import pallas as pl
from jax.experimental.pallas import tpu as pltpu

F32 = jnp.float32
BF16 = jnp.bfloat16
HIGHEST = lax.Precision.HIGHEST

GRID_W = 64
HEAD_DIM = 64
N_HEADS = 4
NA_ROWS = 8
NA_COLS = 16
SWA_KV_HEADS = 2
SWA_WINDOW = 128
GMLP_GROUPS = 4
GMLP_CHUNK = 128
S5_GROUP = 16
S5_GROUPS = 16
S5_STATE = 64
N_EXPERTS = 8
ROPE_BASE = 10000.0
EPS = 1e-6
NEG_INF = -1e30

BW = 256
S5_T = 16
S5_ROWS = 8
Z_GATES_W = 4096
(Z_QA, Z_KA, Z_VA, Z_QD, Z_KD, Z_VD, Z_GU, Z_GV, Z_SU) = range(Z_GATES_W // BW, Z_GATES_W // BW + 9)
ZW = Z_GATES_W + 9 * BW

V7X_VMEM_LIMIT = 56 * 1024 * 1024
ATT_TQ = 256


def _cparams(*sem):
    return pltpu.CompilerParams(dimension_semantics=sem, vmem_limit_bytes=V7X_VMEM_LIMIT)


def _const_spec(shape):
    nd = len(shape)
    return pl.BlockSpec(shape, lambda *_: (0,) * nd, pipeline_mode=pl.Buffered(1))


def _dot(a, b):
    return jnp.dot(a, b, preferred_element_type=F32)


def _dot_nt(a, b):
    return lax.dot_general(a, b, (((1,), (1,)), ((), ())), preferred_element_type=F32)


def _rms(x, g):
    return x * lax.rsqrt(jnp.mean(x * x, axis=-1, keepdims=True) + EPS) * g


def _mod_kernel(c_ref, w_ref, b_ref, o_ref):
    c = c_ref[...]
    a = c * jax.nn.sigmoid(c)
    o_ref[0] = jnp.dot(a, w_ref[0], preferred_element_type=F32, precision=HIGHEST) + b_ref[0]


def _modulation(c_all, w_mod, b_mod):
    depth, d, n = w_mod.shape
    tn = 1536
    return pl.pallas_call(
        _mod_kernel,
        grid=(depth, n // tn),
        in_specs=[pl.BlockSpec((8, d), lambda l, j: (0, 0)),
                  pl.BlockSpec((1, d, tn), lambda l, j: (l, 0, j)),
                  pl.BlockSpec((1, 1, tn), lambda l, j: (l, 0, j))],
        out_specs=pl.BlockSpec((1, 8, tn), lambda l, j: (l, 0, j)),
        out_shape=jax.ShapeDtypeStruct((depth, 8, n), F32),
        compiler_params=_cparams("arbitrary", "arbitrary"),
        name="modulation",
    )(c_all, w_mod, b_mod.reshape(depth, 1, n))


def _proj_kernel(*refs, rope, n_chunk):
    if rope:
        x_ref, mod_ref, g_ref, w_ref, cos_ref, sin_ref, o_ref = refs
    else:
        x_ref, mod_ref, g_ref, w_ref, o_ref = refs
    h = _rms(x_ref[...], g_ref[...]) * (1.0 + mod_ref[0, 1:2, :]) + mod_ref[0, 0:1, :]
    h = h.astype(BF16)
    rope_blocks = (Z_QD, Z_KD) if rope else ()
    for j in range(ZW // n_chunk):
        lo = j * n_chunk
        r = _dot(h, w_ref[:, lo:lo + n_chunk])
        blocks = range(lo // BW, (lo + n_chunk) // BW)
        if not any(b in rope_blocks for b in blocks):
            o_ref[:, lo:lo + n_chunk] = r.astype(BF16)
            continue
        for b in blocks:
            t = r[:, b * BW - lo:(b + 1) * BW - lo]
            if b in rope_blocks:
                lane = lax.broadcasted_iota(jnp.int32, (1, BW), 1)
                low_half = (lane % (HEAD_DIM // 2)) < (HEAD_DIM // 4)
                partner = jnp.where(low_half, pltpu.roll(t, BW - HEAD_DIM // 4, 1),
                                    pltpu.roll(t, HEAD_DIM // 4, 1))
                t = t * cos_ref[...] + partner * sin_ref[...]
            o_ref[:, b * BW:(b + 1) * BW] = t.astype(BF16)


def _project(x2, mod, g, w_z, tiles_per_mod, rope_tabs, tm):
    m, d = x2.shape
    rope = rope_tabs is not None
    in_specs = [pl.BlockSpec((tm, d), lambda i: (i, 0)),
                pl.BlockSpec((1, 8, d), lambda i: (i // tiles_per_mod, 0, 0)),
                _const_spec((1, d)),
                _const_spec((d, ZW))]
    args = [x2, mod, g.reshape(1, d), w_z]
    if rope:
        n_rt = rope_tabs[0].shape[0] // tm
        in_specs += [pl.BlockSpec((tm, BW), lambda i: (i % n_rt, 0))] * 2
        args += list(rope_tabs)
    return pl.pallas_call(
        functools.partial(_proj_kernel, rope=rope, n_chunk=1280),
        grid=(m // tm,),
        in_specs=in_specs,
        out_specs=pl.BlockSpec((tm, ZW), lambda i: (i, 0)),
        out_shape=jax.ShapeDtypeStruct((m, ZW), BF16),
        compiler_params=_cparams("arbitrary"),
        name="project_in",
    )(*args)


def _attn_kernel(*refs, n_local, bias_heads, use_sink):
    q_ref = refs[0]
    kv = refs[1:1 + 2 * n_local]
    kc_ref, vc_ref = refs[1 + 2 * n_local], refs[2 + 2 * n_local]
    pos = 3 + 2 * n_local
    bias_ref = sink_ref = None
    if n_local:
        bias_ref = refs[pos]
        pos += 1
    if use_sink:
        sink_ref = refs[pos]
        pos += 1
    o_ref = refs[pos]

    q = q_ref[0] * (HEAD_DIM ** -0.5)
    tq = q.shape[0]
    lane_head = lax.broadcasted_iota(jnp.int32, (1, BW), 1) // HEAD_DIM
    acc = jnp.zeros((tq, BW), F32)
    for h in range(N_HEADS):
        sel = lane_head == h
        qh = jnp.where(sel, q, jnp.zeros_like(q))
        scores = []
        for j in range(n_local):
            b = bias_ref[0, h if bias_heads > 1 else 0, :, j * tq:(j + 1) * tq]
            scores.append(_dot_nt(qh, kv[2 * j][0]) + b)
        scores.append(_dot_nt(qh, kc_ref[0]))
        mx = scores[0].max(axis=-1, keepdims=True)
        for s in scores[1:]:
            mx = jnp.maximum(mx, s.max(axis=-1, keepdims=True))
        if use_sink:
            snk = sink_ref[h:h + 1, 0:1]
            mx = jnp.maximum(mx, snk)
        den = jnp.exp(snk - mx) if use_sink else jnp.zeros_like(mx)
        o = jnp.zeros((tq, BW), F32)
        vals = [kv[2 * j + 1] for j in range(n_local)] + [vc_ref]
        for s, v_ref in zip(scores, vals):
            p = jnp.exp(s - mx)
            den = den + p.sum(axis=-1, keepdims=True)
            o = o + _dot(p.astype(BF16), v_ref[0])
        acc = acc + jnp.where(sel, o / den, 0.0)
    o_ref[0] = acc.astype(BF16)


def _local_attention(z, z_c, q_col, k_col, v_col, bias, sink_rows):
    b, s, _ = z.shape
    l = z_c.shape[1]
    tq = ATT_TQ
    nq = s // tq
    assert nq >= 3 and l == BW

    def kv_spec(col, j):
        return pl.BlockSpec((1, tq, BW), lambda bi, i: (bi, jnp.clip(i - 1, 0, nq - 3) + j, col))

    def pat(i):
        return jnp.where(i == 0, 0, jnp.where(i == nq - 1, 2, 1))

    in_specs = [pl.BlockSpec((1, tq, BW), lambda bi, i: (bi, i, q_col))]
    args = [z]
    for j in range(3):
        in_specs += [kv_spec(k_col, j), kv_spec(v_col, j)]
        args += [z, z]
    in_specs += [pl.BlockSpec((1, l, BW), lambda bi, i: (bi, 0, k_col)),
                 pl.BlockSpec((1, l, BW), lambda bi, i: (bi, 0, v_col)),
                 pl.BlockSpec((1,) + bias.shape[1:], lambda bi, i: (pat(i), 0, 0, 0))]
    args += [z_c, z_c, bias]
    if sink_rows is not None:
        in_specs.append(_const_spec(sink_rows.shape))
        args.append(sink_rows)
    return pl.pallas_call(
        functools.partial(_attn_kernel, n_local=3, bias_heads=bias.shape[1], use_sink=sink_rows is not None),
        grid=(b, nq),
        in_specs=in_specs,
        out_specs=pl.BlockSpec((1, tq, BW), lambda bi, i: (bi, i, 0)),
        out_shape=jax.ShapeDtypeStruct((b, s, BW), BF16),
        compiler_params=_cparams("arbitrary", "arbitrary"),
        name="local_attention",
    )(*args)


def _ctx_attention(z_c, q_col, k_col, v_col, sink_rows):
    b, l, _ = z_c.shape
    in_specs = [pl.BlockSpec((1, l, BW), lambda bi, c=col: (bi, 0, c)) for col in (q_col, k_col, v_col)]
    args = [z_c, z_c, z_c]
    if sink_rows is not None:
        in_specs.append(_const_spec(sink_rows.shape))
        args.append(sink_rows)
    return pl.pallas_call(
        functools.partial(_attn_kernel, n_local=0, bias_heads=1, use_sink=sink_rows is not None),
        grid=(b,),
        in_specs=in_specs,
        out_specs=pl.BlockSpec((1, l, BW), lambda bi: (bi, 0, 0)),
        out_shape=jax.ShapeDtypeStruct((b, l, BW), BF16),
        compiler_params=_cparams("arbitrary"),
        name="ctx_attention",
    )(*args)


def _na_bias(rpb, rows):
    tile_rows = ATT_TQ // GRID_W
    nq = rows // tile_rows
    tq = np.arange(ATT_TQ)
    tk = np.arange(3 * ATT_TQ)
    out = []
    for i in (0, 1, nq - 1):
        base = min(max(i - 1, 0), nq - 3)
        qr = (tile_rows * i + tq // GRID_W)[:, None]
        qc = (tq % GRID_W)[:, None]
        kr = (tile_rows * base + tk // GRID_W)[None, :]
        kc = (tk % GRID_W)[None, :]
        rs = np.clip(qr - NA_ROWS // 2, 0, rows - NA_ROWS)
        cs = np.clip(qc - NA_COLS // 2, 0, GRID_W - NA_COLS)
        valid = (kr >= rs) & (kr < rs + NA_ROWS) & (kc >= cs) & (kc < cs + NA_COLS)
        ri = np.clip(kr - qr + NA_ROWS - 1, 0, 2 * NA_ROWS - 2)
        ci = np.clip(kc - qc + NA_COLS - 1, 0, 2 * NA_COLS - 2)
        out.append(jnp.where(valid[None], rpb.astype(F32)[:, ri, ci], NEG_INF))
    return jnp.stack(out)


def _swa_bias(s):
    nq = s // ATT_TQ
    tq = np.arange(ATT_TQ)[:, None]
    tk = np.arange(3 * ATT_TQ)[None, :]
    out = []
    for i in (0, 1, nq - 1):
        base = min(max(i - 1, 0), nq - 3)
        valid = np.abs((ATT_TQ * base + tk) - (ATT_TQ * i + tq)) <= SWA_WINDOW
        out.append(np.where(valid, 0.0, NEG_INF).astype(np.float32)[None])
    return jnp.asarray(np.stack(out))


def _rope_tables(s):
    nq = HEAD_DIM // 4
    t = jnp.arange(s)
    inv = ROPE_BASE ** (-jnp.arange(nq, dtype=F32) / nq)
    ang_r = (t // GRID_W).astype(F32)[:, None] * inv[None, :]
    ang_c = (t % GRID_W).astype(F32)[:, None] * inv[None, :]
    cos = jnp.concatenate([jnp.cos(ang_r)] * 2 + [jnp.cos(ang_c)] * 2, axis=-1)
    sin = jnp.concatenate([-jnp.sin(ang_r), jnp.sin(ang_r), -jnp.sin(ang_c), jnp.sin(ang_c)], axis=-1)
    return jnp.tile(cos, (1, N_HEADS)), jnp.tile(sin, (1, N_HEADS))


def _gmlp_kernel(u_ref, v_ref, g_ref, b_ref, w_ref, bs_ref, o_ref):
    lane_grp = lax.broadcasted_iota(jnp.int32, (1, BW), 1) // (BW // GMLP_GROUPS)
    for c in range(u_ref.shape[0] // GMLP_CHUNK):
        rows = slice(c * GMLP_CHUNK, (c + 1) * GMLP_CHUNK)
        u = jax.nn.gelu(u_ref[rows, :].astype(F32))
        v = jax.nn.gelu(v_ref[rows, :].astype(F32))
        mu = jnp.mean(v, axis=-1, keepdims=True)
        vc = v - mu
        v = vc * lax.rsqrt(jnp.mean(vc * vc, axis=-1, keepdims=True) + EPS) * g_ref[...] + b_ref[...]
        v = v.astype(BF16)
        stack = jnp.concatenate([jnp.where(lane_grp == g, v, jnp.zeros_like(v)) for g in range(GMLP_GROUPS)], axis=0)
        sg = _dot(w_ref[...], stack) + bs_ref[...]
        o_ref[rows, :] = (u * sg).astype(BF16)


def _gmlp(z2, ln_g, ln_b, ws, bs, tm):
    m = z2.shape[0]
    w_cat = jnp.concatenate([ws[g] for g in range(GMLP_GROUPS)], axis=1).astype(BF16)
    bs_l = jnp.repeat(bs.T.astype(F32), BW // GMLP_GROUPS, axis=1)
    return pl.pallas_call(
        _gmlp_kernel,
        grid=(m // tm,),
        in_specs=[pl.BlockSpec((tm, BW), lambda i: (i, Z_GU)),
                  pl.BlockSpec((tm, BW), lambda i: (i, Z_GV)),
                  _const_spec((1, BW)), _const_spec((1, BW)),
                  _const_spec(w_cat.shape), _const_spec(bs_l.shape)],
        out_specs=pl.BlockSpec((tm, BW), lambda i: (i, 0)),
        out_shape=jax.ShapeDtypeStruct((m, BW), BF16),
        compiler_params=_cparams("arbitrary"),
        name="gmlp",
    )(z2, z2, ln_g.reshape(1, BW).astype(F32), ln_b.reshape(1, BW).astype(F32), w_cat, bs_l)


def _s5_kernel(u_ref, wend_ref, toep_ref, wc_ref, a16_ref, o_ref, st_ref, *, n_ctx, n_chunks, row_blk):
    r_total = u_ref.shape[0]
    for r0 in range(0, r_total, row_blk):
        st_ref[r0:r0 + row_blk, :] = _dot(u_ref[r0:r0 + row_blk, :], wend_ref[0])

    a = [jnp.broadcast_to(a16_ref[0, k:k + 1, :], (S5_ROWS, 128)) for k in range(4)]

    def step(k, carry):
        fr, fi, rr, ri = carry
        jf = pl.multiple_of(k * S5_ROWS, S5_ROWS)
        kr = jnp.where(k < n_ctx, n_ctx - 1 - k, n_chunks - 1 - (k - n_ctx))
        jr = pl.multiple_of(kr * S5_ROWS, S5_ROWS)
        e_fr = st_ref[pl.ds(jf, S5_ROWS), 0:128]
        e_fi = st_ref[pl.ds(jf, S5_ROWS), 128:256]
        e_rr = st_ref[pl.ds(jr, S5_ROWS), 256:384]
        e_ri = st_ref[pl.ds(jr, S5_ROWS), 384:512]
        st_ref[pl.ds(jf, S5_ROWS), 0:128] = fr
        st_ref[pl.ds(jf, S5_ROWS), 128:256] = fi
        st_ref[pl.ds(jr, S5_ROWS), 256:384] = rr
        st_ref[pl.ds(jr, S5_ROWS), 384:512] = ri
        return (a[0] * fr - a[1] * fi + e_fr, a[0] * fi + a[1] * fr + e_fi,
                a[2] * rr - a[3] * ri + e_rr, a[2] * ri + a[3] * rr + e_ri)

    zero = jnp.zeros((S5_ROWS, 128), F32)
    lax.fori_loop(0, n_chunks, step, (zero, zero, zero, zero), unroll=4)

    for r0 in range(0, r_total, row_blk):
        rows = slice(r0, r0 + row_blk)
        y = _dot(u_ref[rows, :], toep_ref[0]) + _dot(st_ref[rows, :].astype(BF16), wc_ref[0])
        o_ref[rows, :] = y.astype(BF16)


def _s5_matrices(a_re, a_im, log_dt, b_re, b_im, c_re, c_im, d):
    g, p, c, t = S5_GROUPS, S5_STATE, S5_GROUP, S5_T
    lam_re = jnp.minimum(a_re.astype(F32), -1e-4)
    lam_im = a_im.astype(F32)
    dt = jnp.exp(log_dt.astype(F32))[..., None]

    def a_pow(n):
        n = jnp.asarray(n, F32)[:, None, None, None]
        mag = jnp.exp(n * (lam_re * dt)[None])
        ang = n * (lam_im * dt)[None]
        return mag * jnp.cos(ang), mag * jnp.sin(ang)

    ab_re, ab_im = (v[0] for v in a_pow([1.0]))
    den = lam_re * lam_re + lam_im * lam_im
    k_re = ((ab_re - 1.0) * lam_re + ab_im * lam_im) / den
    k_im = (ab_im * lam_re - (ab_re - 1.0) * lam_im) / den
    br, bi = b_re.astype(F32), b_im.astype(F32)
    bb_re = k_re[..., None] * br - k_im[..., None] * bi
    bb_im = k_re[..., None] * bi + k_im[..., None] * br
    cr, ci = c_re.astype(F32), c_im.astype(F32)

    ein = functools.partial(jnp.einsum, precision=HIGHEST)
    pw_re, pw_im = a_pow(np.arange(t + 1))
    ca_re = cr[None] * pw_re[:t, :, :, None, :] - ci[None] * pw_im[:t, :, :, None, :]
    ca_im = cr[None] * pw_im[:t, :, :, None, :] + ci[None] * pw_re[:t, :, :, None, :]
    lag = ein('ldgop,dgpi->ldgoi', ca_re, bb_re) - ein('ldgop,dgpi->ldgoi', ca_im, bb_im)
    s_i = np.arange(t)[:, None]
    t_i = np.arange(t)[None, :]
    fwd = jnp.where((t_i >= s_i)[:, :, None, None, None], lag[np.clip(t_i - s_i, 0, t - 1), 0], 0.0)
    rev = jnp.where((s_i >= t_i)[:, :, None, None, None], lag[np.clip(s_i - t_i, 0, t - 1), 1], 0.0)
    toep = jnp.transpose(fwd + rev, (2, 0, 4, 1, 3))
    skip = jnp.einsum('st,io->sito', jnp.eye(t, dtype=F32), jnp.eye(c, dtype=F32))[None] \
        * d.astype(F32).reshape(g, 1, 1, 1, c)
    toep = (toep + skip).reshape(g, t * c, t * c)

    def end_w(pr, pi, dirn):
        w_re = pr[..., None] * bb_re[dirn][None] - pi[..., None] * bb_im[dirn][None]
        w_im = pr[..., None] * bb_im[dirn][None] + pi[..., None] * bb_re[dirn][None]
        to = lambda w: jnp.transpose(w, (1, 0, 3, 2)).reshape(g, t * c, p)
        return to(w_re), to(w_im)

    wf_re, wf_im = end_w(pw_re[t - 1::-1, 0][:t], pw_im[t - 1::-1, 0][:t], 0)
    wr_re, wr_im = end_w(pw_re[:t, 1], pw_im[:t, 1], 1)

    def carry_w(pr, pi, dirn):
        m_re = cr[dirn][None] * pr[:, :, None, :] - ci[dirn][None] * pi[:, :, None, :]
        m_im = cr[dirn][None] * pi[:, :, None, :] + ci[dirn][None] * pr[:, :, None, :]
        to = lambda w: jnp.transpose(w, (1, 3, 0, 2)).reshape(g, p, t * c)
        return to(m_re), to(-m_im)

    cf_re, cf_im = carry_w(pw_re[1:t + 1, 0], pw_im[1:t + 1, 0], 0)
    cr_re, cr_im = carry_w(pw_re[t:0:-1, 1], pw_im[t:0:-1, 1], 1)

    eye2 = jnp.eye(2, dtype=F32)
    half = g // 2
    toep_p = jnp.einsum('xgab,gh->xgahb', toep.reshape(half, 2, t * c, t * c), eye2).reshape(half, 2 * t * c, 2 * t * c)
    wend = jnp.stack([wf_re, wf_im, wr_re, wr_im], axis=1)
    wend_p = jnp.einsum('xgkap,gh->xgakhp', wend.reshape(half, 2, 4, t * c, p), eye2).reshape(half, 2 * t * c, 8 * p)
    wc = jnp.stack([cf_re, cf_im, cr_re, cr_im], axis=1)
    wc_p = jnp.einsum('xgkpb,gh->xkgphb', wc.reshape(half, 2, 4, p, t * c), eye2).reshape(half, 8 * p, 2 * t * c)
    a16 = jnp.stack([pw_re[t, 0], pw_im[t, 0], pw_re[t, 1], pw_im[t, 1]], axis=0)
    a16_p = a16.reshape(4, half, 2 * p).transpose(1, 0, 2)
    return toep_p.astype(BF16), wend_p.astype(BF16), wc_p.astype(BF16), a16_p


def _s5_mix(su_c, su_x, mats):
    toep_p, wend_p, wc_p, a16_p = mats
    b, l, _ = su_c.shape
    s = su_x.shape[1]
    n_ctx, n_chunks = l // S5_T, (l + s) // S5_T
    u = jnp.concatenate([su_c, su_x], axis=1).reshape(b, n_chunks, S5_T, S5_GROUPS, S5_GROUP)
    u = jnp.transpose(u, (1, 0, 3, 2, 4))
    u = jnp.pad(u, ((0, 0), (0, S5_ROWS - b), (0, 0), (0, 0), (0, 0))).reshape(n_chunks * S5_ROWS, S5_GROUPS * BW)
    r = n_chunks * S5_ROWS
    pw = 2 * BW
    row_blk = r // 8
    assert r % 8 == 0 and row_blk % 8 == 0
    y = pl.pallas_call(
        functools.partial(_s5_kernel, n_ctx=n_ctx, n_chunks=n_chunks, row_blk=row_blk),
        grid=(S5_GROUPS // 2,),
        in_specs=[pl.BlockSpec((r, pw), lambda i: (0, i)),
                  pl.BlockSpec((1, pw, pw), lambda i: (i, 0, 0)),
                  pl.BlockSpec((1, pw, pw), lambda i: (i, 0, 0)),
                  pl.BlockSpec((1, pw, pw), lambda i: (i, 0, 0)),
                  pl.BlockSpec((1, 4, 128), lambda i: (i, 0, 0))],
        out_specs=pl.BlockSpec((r, pw), lambda i: (0, i)),
        out_shape=jax.ShapeDtypeStruct((r, S5_GROUPS * BW), BF16),
        scratch_shapes=[pltpu.VMEM((r, pw), F32)],
        compiler_params=_cparams("arbitrary"),
        name="s5_mixer",
    )(u, wend_p, toep_p, wc_p, a16_p)
    y = y.reshape(n_chunks, S5_ROWS, S5_GROUPS, S5_T, S5_GROUP)[:, :b]
    y = jnp.transpose(y, (1, 0, 3, 2, 4)).reshape(b, l + s, BW)
    return y[:, :l], y[:, l:]


def _merge_kernel(zg_ref, oa_ref, ob_ref, ys_ref, od_ref, x_ref, mod_ref, g_ref, wb_ref, wo_ref, gw_ref, gb_ref,
                  o_ref):
    d = x_ref.shape[1]
    y = jax.nn.gelu(ys_ref[...].astype(F32))
    oc = (y * jax.nn.sigmoid(_dot(y.astype(BF16), gw_ref[...]) + gb_ref[...])).astype(BF16)
    outs = (oa_ref[...], ob_ref[...], oc, od_ref[...])
    m = None
    for i, o in enumerate(outs):
        term = jax.nn.sigmoid(zg_ref[:, i * d:(i + 1) * d].astype(F32)) * _dot(o, wb_ref[i])
        m = term if m is None else m + term
    mo = _dot(m.astype(BF16), wo_ref[...])
    o_ref[...] = x_ref[...] + mod_ref[0, 2:3, :] * _rms(mo, g_ref[...])


def _merge(z2, o_a, o_b, y_s, o_d, x2, mod, g_post, wb, wo, glu_w, glu_b, tiles_per_mod, tm):
    m, d = x2.shape
    row = lambda i: (i, 0)
    return pl.pallas_call(
        _merge_kernel,
        grid=(m // tm,),
        in_specs=[pl.BlockSpec((tm, Z_GATES_W), row)] + [pl.BlockSpec((tm, BW), row)] * 4
        + [pl.BlockSpec((tm, d), row),
           pl.BlockSpec((1, 8, d), lambda i: (i // tiles_per_mod, 0, 0)),
           _const_spec((1, d)), _const_spec(wb.shape), _const_spec(wo.shape),
           _const_spec(glu_w.shape), _const_spec((1, BW))],
        out_specs=pl.BlockSpec((tm, d), row),
        out_shape=jax.ShapeDtypeStruct((m, d), F32),
        compiler_params=_cparams("arbitrary"),
        name="merge",
    )(z2, o_a, o_b, y_s, o_d, x2, mod, g_post.reshape(1, d), wb, wo, glu_w, glu_b.reshape(1, BW).astype(F32))


def _ffn_dense_kernel(x_ref, mod_ref, gpre_ref, gpost_ref, w1_ref, w3_ref, w2_ref, o_ref, h_ref, acc_ref):
    j = pl.program_id(1)

    @pl.when(j == 0)
    def _():
        h = _rms(x_ref[...], gpre_ref[...]) * (1.0 + mod_ref[0, 4:5, :]) + mod_ref[0, 3:4, :]
        h_ref[...] = h.astype(BF16)
        acc_ref[...] = jnp.zeros_like(acc_ref)

    h = h_ref[...]
    a = _dot(h, w1_ref[...])
    a = (a * jax.nn.sigmoid(a)) * _dot(h, w3_ref[...])
    acc_ref[...] += _dot(a.astype(BF16), w2_ref[...])

    @pl.when(j == pl.num_programs(1) - 1)
    def _():
        o_ref[...] = x_ref[...] + mod_ref[0, 5:6, :] * _rms(acc_ref[...], gpost_ref[...])


def _ffn_dense(x2, mod, g_pre, g_post, w1, w3, w2, tiles_per_mod, tm, fc):
    m, d = x2.shape
    f = w1.shape[1]
    return pl.pallas_call(
        _ffn_dense_kernel,
        grid=(m // tm, f // fc),
        in_specs=[pl.BlockSpec((tm, d), lambda i, j: (i, 0)),
                  pl.BlockSpec((1, 8, d), lambda i, j: (i // tiles_per_mod, 0, 0)),
                  _const_spec((1, d)), _const_spec((1, d)),
                  pl.BlockSpec((d, fc), lambda i, j: (0, j)),
                  pl.BlockSpec((d, fc), lambda i, j: (0, j)),
                  pl.BlockSpec((fc, d), lambda i, j: (j, 0))],
        out_specs=pl.BlockSpec((tm, d), lambda i, j: (i, 0)),
        out_shape=jax.ShapeDtypeStruct((m, d), F32),
        scratch_shapes=[pltpu.VMEM((tm, d), BF16), pltpu.VMEM((tm, d), F32)],
        compiler_params=_cparams("arbitrary", "arbitrary"),
        name="ffn_dense",
    )(x2, mod, g_pre.reshape(1, d), g_post.reshape(1, d), w1, w3, w2)


def _route_kernel(x_ref, mod_ref, gpre_ref, r_ref, h_ref, idx_ref, wt_ref):
    h = _rms(x_ref[...], gpre_ref[...]) * (1.0 + mod_ref[0, 4:5, :]) + mod_ref[0, 3:4, :]
    h_ref[...] = h.astype(BF16)
    logits = jnp.dot(h, r_ref[...], preferred_element_type=F32, precision=HIGHEST)
    lane = lax.broadcasted_iota(jnp.int32, logits.shape, 1)
    logits = jnp.where(lane < N_EXPERTS, logits, 2.0 * NEG_INF)
    m0 = logits.max(axis=-1, keepdims=True)
    i0 = jnp.where(logits == m0, lane, 128).min(axis=-1, keepdims=True)
    rest = jnp.where(lane == i0, NEG_INF, logits)
    m1 = rest.max(axis=-1, keepdims=True)
    i1 = jnp.where(rest == m1, lane, 128).min(axis=-1, keepdims=True)
    e = jnp.exp(m1 - m0)
    w0 = 1.0 / (1.0 + e)
    idx_ref[...] = jnp.where(lane == 0, i0, jnp.where(lane == 1, i1, 0))
    wt_ref[...] = jnp.where(lane == 0, w0, jnp.where(lane == 1, e * w0, 0.0))


def _route(x2, mod, g_pre, router, tiles_per_mod, tm):
    m, d = x2.shape
    r_pad = jnp.pad(router.astype(F32), ((0, 0), (0, 128 - N_EXPERTS)))
    row = lambda i: (i, 0)
    return pl.pallas_call(
        _route_kernel,
        grid=(m // tm,),
        in_specs=[pl.BlockSpec((tm, d), row),
                  pl.BlockSpec((1, 8, d), lambda i: (i // tiles_per_mod, 0, 0)),
                  _const_spec((1, d)), _const_spec((d, 128))],
        out_specs=[pl.BlockSpec((tm, d), row), pl.BlockSpec((tm, 128), row), pl.BlockSpec((tm, 128), row)],
        out_shape=[jax.ShapeDtypeStruct((m, d), BF16), jax.ShapeDtypeStruct((m, 128), jnp.int32),
                   jax.ShapeDtypeStruct((m, 128), F32)],
        compiler_params=_cparams("arbitrary"),
        name="moe_route",
    )(x2, mod, g_pre.reshape(1, d), r_pad)


def _ffn_grouped_kernel(te_ref, nv_ref, h_ref, w1_ref, w3_ref, w2_ref, o_ref, acc_ref):
    i, j = pl.program_id(0), pl.program_id(1)
    last = pl.num_programs(1) - 1
    valid = i < nv_ref[0]

    @pl.when(j == 0)
    def _():
        acc_ref[...] = jnp.zeros_like(acc_ref)

    @pl.when(valid)
    def _():
        h = h_ref[...]
        a = _dot(h, w1_ref[0])
        a = (a * jax.nn.sigmoid(a)) * _dot(h, w3_ref[0])
        acc_ref[...] += _dot(a.astype(BF16), w2_ref[0])

    @pl.when(j == last)
    def _():
        o_ref[...] = acc_ref[...].astype(BF16)


def _ffn_grouped(h_sorted, tile_expert, n_valid, w1, w3, w2, tm, fc):
    r, d = h_sorted.shape
    f = w1.shape[2]
    nj = f // fc

    def col(i, j, nv):
        return jnp.where(i < nv[0], j, nj - 1)

    grid_spec = pltpu.PrefetchScalarGridSpec(
        num_scalar_prefetch=2,
        grid=(r // tm, nj),
        in_specs=[pl.BlockSpec((tm, d), lambda i, j, te, nv: (i, 0)),
                  pl.BlockSpec((1, d, fc), lambda i, j, te, nv: (te[i], 0, col(i, j, nv))),
                  pl.BlockSpec((1, d, fc), lambda i, j, te, nv: (te[i], 0, col(i, j, nv))),
                  pl.BlockSpec((1, fc, d), lambda i, j, te, nv: (te[i], col(i, j, nv), 0))],
        out_specs=pl.BlockSpec((tm, d), lambda i, j, te, nv: (i, 0)),
        scratch_shapes=[pltpu.VMEM((tm, d), F32)],
    )
    return pl.pallas_call(
        _ffn_grouped_kernel,
        grid_spec=grid_spec,
        out_shape=jax.ShapeDtypeStruct((r, d), BF16),
        compiler_params=_cparams("arbitrary", "arbitrary"),
        name="ffn_grouped",
    )(tile_expert, n_valid, h_sorted, w1, w3, w2)


def _combine_kernel(o0_ref, o1_ref, wt_ref, x_ref, mod_ref, gpost_ref, o_ref):
    y = wt_ref[:, 0:1] * o0_ref[...].astype(F32) + wt_ref[:, 1:2] * o1_ref[...].astype(F32)
    o_ref[...] = x_ref[...] + mod_ref[0, 5:6, :] * _rms(y, gpost_ref[...])


def _combine(o0, o1, wts, x2, mod, g_post, tiles_per_mod, tm):
    m, d = x2.shape
    row = lambda i: (i, 0)
    return pl.pallas_call(
        _combine_kernel,
        grid=(m // tm,),
        in_specs=[pl.BlockSpec((tm, d), row), pl.BlockSpec((tm, d), row), pl.BlockSpec((tm, 128), row),
                  pl.BlockSpec((tm, d), row),
                  pl.BlockSpec((1, 8, d), lambda i: (i // tiles_per_mod, 0, 0)),
                  _const_spec((1, d))],
        out_specs=pl.BlockSpec((tm, d), row),
        out_shape=jax.ShapeDtypeStruct((m, d), F32),
        compiler_params=_cparams("arbitrary"),
        name="moe_combine",
    )(o0, o1, wts, x2, mod, g_post.reshape(1, d))


def _moe(x2, mod, g_pre, g_post, router, w1, w3, w2, tiles_per_mod, tm):
    m, d = x2.shape
    h, idx, wts = _route(x2, mod, g_pre, router, tiles_per_mod, tm)
    e_flat = idx[:, :2].reshape(-1)
    onehot = (e_flat[:, None] == jnp.arange(N_EXPERTS)[None, :]).astype(jnp.int32)
    rank = jnp.take_along_axis(jnp.cumsum(onehot, axis=0), e_flat[:, None], axis=1)[:, 0] - 1
    counts = onehot.sum(axis=0)
    padded = (counts + tm - 1) // tm * tm
    ends = jnp.cumsum(padded)
    pos = (ends - padded)[e_flat] + rank
    n_tiles = (2 * m) // tm + N_EXPERTS
    src = jnp.zeros((n_tiles * tm,), jnp.int32).at[pos].set(jnp.arange(2 * m, dtype=jnp.int32) // 2)
    tile_expert = jnp.minimum(jnp.searchsorted(ends, jnp.arange(n_tiles, dtype=jnp.int32) * tm, side='right'),
                              N_EXPERTS - 1).astype(jnp.int32)
    n_valid = (ends[-1] // tm).astype(jnp.int32).reshape(1)
    out = _ffn_grouped(jnp.take(h, src, axis=0), tile_expert, n_valid, w1, w3, w2, tm, 896)
    pos2 = pos.reshape(m, 2)
    return _combine(jnp.take(out, pos2[:, 0], axis=0), jnp.take(out, pos2[:, 1], axis=0), wts, x2, mod, g_post,
                    tiles_per_mod, tm)


def _z_weights(w_in):
    d = w_in.shape[0]
    hw = N_HEADS * HEAD_DIM
    kvw = SWA_KV_HEADS * HEAD_DIM
    sizes = (hw, hw, hw, hw, kvw, kvw, BW, BW, BW, Z_GATES_W)
    qa, ka, va, qd, kd, vd, gu, gv, su, gates = jnp.split(w_in, np.cumsum(sizes)[:-1].tolist(), axis=1)
    rep = N_HEADS // SWA_KV_HEADS
    dup = lambda w: jnp.repeat(w.reshape(d, SWA_KV_HEADS, 1, HEAD_DIM), rep, axis=2).reshape(d, hw)
    return jnp.concatenate([gates, qa, ka, va, qd, dup(kd), dup(vd), gu, gv, su], axis=1).astype(BF16)


def kernel(x, c, ctx, c_ctx, w_mod, b_mod, g_pre_mix, g_post_mix, g_pre_ffn, g_post_ffn, w_in, na_rpb, swa_sink, gmlp_ln_g, gmlp_ln_b, gmlp_ws, gmlp_bs, s5_a_re, s5_a_im, s5_log_dt, s5_b_re, s5_b_im, s5_c_re, s5_c_im, s5_d, s5_glu_w, s5_glu_b, w_branch, w_out, ffn_w1, ffn_w3, ffn_w2, moe_router, moe_w1, moe_w3, moe_w2):
    bsz, s, d = x.shape
    l = ctx.shape[1]
    depth = w_mod.shape[0]
    tm = 512
    rows = s // GRID_W

    c_all = jnp.zeros((8, d), F32).at[:bsz].set(c).at[bsz].set(c_ctx)
    mods = _modulation(c_all, w_mod, b_mod).reshape(depth, 8, 6, d)
    mods = jnp.pad(mods, ((0, 0), (0, 0), (0, 2), (0, 0)))
    rope_tabs = _rope_tables(s)
    swa_bias = _swa_bias(s)

    x2 = x.reshape(bsz * s, d)
    c2 = ctx.reshape(bsz * l, d)
    for i in range(depth):
        ctx_out = i < depth - 1
        mod_x = mods[i, :bsz]
        mod_c = mods[i, bsz:bsz + 1]
        w_z = _z_weights(w_in[i])
        zx = _project(x2, mod_x, g_pre_mix[i], w_z, s // tm, rope_tabs, tm)
        zc = _project(c2, mod_c, g_pre_mix[i], w_z, (bsz * l) // tm, None, tm)
        zx3 = zx.reshape(bsz, s, ZW)
        zc3 = zc.reshape(bsz, l, ZW)
        sink_rows = jnp.broadcast_to(jnp.pad(swa_sink[i].astype(F32), (0, 8 - N_HEADS))[:, None], (8, 128))

        o_a = _local_attention(zx3, zc3, Z_QA, Z_KA, Z_VA, _na_bias(na_rpb[i], rows), None)
        o_d = _local_attention(zx3, zc3, Z_QD, Z_KD, Z_VD, swa_bias, sink_rows)
        o_b = _gmlp(zx, gmlp_ln_g[i], gmlp_ln_b[i], gmlp_ws[i], gmlp_bs[i], tm)
        s5_mats = _s5_matrices(s5_a_re[i], s5_a_im[i], s5_log_dt[i], s5_b_re[i], s5_b_im[i],
                               s5_c_re[i], s5_c_im[i], s5_d[i])
        su_c = zc3[:, :, Z_SU * BW:(Z_SU + 1) * BW]
        su_x = zx3[:, :, Z_SU * BW:(Z_SU + 1) * BW]
        y_c, y_x = _s5_mix(su_c, su_x, s5_mats)

        wb = w_branch[i].astype(BF16)
        wo = w_out[i].astype(BF16)
        glu_w = s5_glu_w[i].astype(BF16)
        x2 = _merge(zx, o_a.reshape(bsz * s, BW), o_b, y_x.reshape(bsz * s, BW), o_d.reshape(bsz * s, BW),
                    x2, mod_x, g_post_mix[i], wb, wo, glu_w, s5_glu_b[i], s // tm, tm)
        if ctx_out:
            o_a_c = _ctx_attention(zc3, Z_QA, Z_KA, Z_VA, None)
            o_d_c = _ctx_attention(zc3, Z_QD, Z_KD, Z_VD, sink_rows)
            o_b_c = _gmlp(zc, gmlp_ln_g[i], gmlp_ln_b[i], gmlp_ws[i], gmlp_bs[i], tm)
            c2 = _merge(zc, o_a_c.reshape(bsz * l, BW), o_b_c, y_c.reshape(bsz * l, BW), o_d_c.reshape(bsz * l, BW),
                        c2, mod_c, g_post_mix[i], wb, wo, glu_w, s5_glu_b[i], (bsz * l) // tm, tm)

        j = i // 2
        if i % 2 == 0:
            w1, w3, w2 = ffn_w1[j].astype(BF16), ffn_w3[j].astype(BF16), ffn_w2[j].astype(BF16)
            x2 = _ffn_dense(x2, mod_x, g_pre_ffn[i], g_post_ffn[i], w1, w3, w2, s // tm, tm, 1408)
            if ctx_out:
                c2 = _ffn_dense(c2, mod_c, g_pre_ffn[i], g_post_ffn[i], w1, w3, w2, (bsz * l) // tm, tm, 1408)
        else:
            w1, w3, w2 = moe_w1[j].astype(BF16), moe_w3[j].astype(BF16), moe_w2[j].astype(BF16)
            x2 = _moe(x2, mod_x, g_pre_ffn[i], g_post_ffn[i], moe_router[j], w1, w3, w2, s // tm, tm)
            if ctx_out:
                c2 = _moe(c2, mod_c, g_pre_ffn[i], g_post_ffn[i], moe_router[j], w1, w3, w2, (bsz * l) // tm, tm)
    return x2.reshape(bsz, s, d)
```

```python
import functools
import math

import numpy as np
import jax
import jax.numpy as jnp
from jax import lax
from jax.experimental import pallas as pl
from jax.experimental.pallas import tpu as pltpu

F32 = jnp.float32
BF16 = jnp.bfloat16
HIGHEST = lax.Precision.HIGHEST

GRID_W = 64
HEAD_DIM = 64
N_HEADS = 4
NA_ROWS = 8
NA_COLS = 16
SWA_KV_HEADS = 2
SWA_WINDOW = 128
GMLP_GROUPS = 4
GMLP_CHUNK = 128
S5_GROUP = 16
S5_GROUPS = 16
S5_STATE = 64
N_EXPERTS = 8
ROPE_BASE = 10000.0
EPS = 1e-6
NEG_INF = -1e30

BW = 256
S5_T = 16
S5_ROWS = 8
Z_GATES_W = 4096
(Z_QA, Z_KA, Z_VA, Z_QD, Z_KD, Z_VD, Z_GU, Z_GV, Z_SU) = range(Z_GATES_W // BW, Z_GATES_W // BW + 9)
ZW = Z_GATES_W + 9 * BW

V7X_VMEM_LIMIT = 56 * 1024 * 1024
ATT_TQ = 256


def _cparams(*sem):
    return pltpu.CompilerParams(dimension_semantics=sem, vmem_limit_bytes=V7X_VMEM_LIMIT)


def _const_spec(shape):
    nd = len(shape)
    return pl.BlockSpec(shape, lambda *_: (0,) * nd, pipeline_mode=pl.Buffered(1))


def _dot(a, b):
    return jnp.dot(a, b, preferred_element_type=F32)


def _dot_nt(a, b):
    return lax.dot_general(a, b, (((1,), (1,)), ((), ())), preferred_element_type=F32)


def _rms(x, g):
    return x * lax.rsqrt(jnp.mean(x * x, axis=-1, keepdims=True) + EPS) * g


def _mod_kernel(c_ref, w_ref, b_ref, o_ref):
    c = c_ref[...]
    a = c * jax.nn.sigmoid(c)
    o_ref[0] = jnp.dot(a, w_ref[0], preferred_element_type=F32, precision=HIGHEST) + b_ref[0]


def _modulation(c_all, w_mod, b_mod):
    depth, d, n = w_mod.shape
    tn = 1536
    return pl.pallas_call(
        _mod_kernel,
        grid=(depth, n // tn),
        in_specs=[pl.BlockSpec((8, d), lambda l, j: (0, 0)),
                  pl.BlockSpec((1, d, tn), lambda l, j: (l, 0, j)),
                  pl.BlockSpec((1, 1, tn), lambda l, j: (l, 0, j))],
        out_specs=pl.BlockSpec((1, 8, tn), lambda l, j: (l, 0, j)),
        out_shape=jax.ShapeDtypeStruct((depth, 8, n), F32),
        compiler_params=_cparams("arbitrary", "arbitrary"),
        name="modulation",
    )(c_all, w_mod, b_mod.reshape(depth, 1, n))


def _proj_kernel(*refs, rope, n_chunk):
    if rope:
        x_ref, mod_ref, g_ref, w_ref, cos_ref, sin_ref, o_ref, su_ref = refs
    else:
        x_ref, mod_ref, g_ref, w_ref, o_ref, su_ref = refs
    h = _rms(x_ref[...], g_ref[...]) * (1.0 + mod_ref[0, 1:2, :]) + mod_ref[0, 0:1, :]
    h = h.astype(BF16)
    rope_blocks = (Z_QD, Z_KD) if rope else ()
    for j in range(ZW // n_chunk):
        lo = j * n_chunk
        r = _dot(h, w_ref[:, lo:lo + n_chunk])
        blocks = range(lo // BW, (lo + n_chunk) // BW)
        if Z_SU in blocks:
            su_ref[...] = r[:, Z_SU * BW - lo:(Z_SU + 1) * BW - lo]
        if not any(b in rope_blocks for b in blocks):
            o_ref[:, lo:lo + n_chunk] = r.astype(BF16)
            continue
        for b in blocks:
            t = r[:, b * BW - lo:(b + 1) * BW - lo]
            if b in rope_blocks:
                lane = lax.broadcasted_iota(jnp.int32, (1, BW), 1)
                low_half = (lane % (HEAD_DIM // 2)) < (HEAD_DIM // 4)
                partner = jnp.where(low_half, pltpu.roll(t, BW - HEAD_DIM // 4, 1),
                                    pltpu.roll(t, HEAD_DIM // 4, 1))
                t = t * cos_ref[...] + partner * sin_ref[...]
            o_ref[:, b * BW:(b + 1) * BW] = t.astype(BF16)


def _project(x2, mod, g, w_z, tiles_per_mod, rope_tabs, tm):
    m, d = x2.shape
    rope = rope_tabs is not None
    in_specs = [pl.BlockSpec((tm, d), lambda i: (i, 0)),
                pl.BlockSpec((1, 8, d), lambda i: (i // tiles_per_mod, 0, 0)),
                _const_spec((1, d)),
                _const_spec((d, ZW))]
    args = [x2, mod, g.reshape(1, d), w_z]
    if rope:
        n_rt = rope_tabs[0].shape[0] // tm
        in_specs += [pl.BlockSpec((tm, BW), lambda i: (i % n_rt, 0))] * 2
        args += list(rope_tabs)
    return pl.pallas_call(
        functools.partial(_proj_kernel, rope=rope, n_chunk=1280),
        grid=(m // tm,),
        in_specs=in_specs,
        out_specs=[pl.BlockSpec((tm, ZW), lambda i: (i, 0)), pl.BlockSpec((tm, BW), lambda i: (i, 0))],
        out_shape=[jax.ShapeDtypeStruct((m, ZW), BF16), jax.ShapeDtypeStruct((m, BW), F32)],
        compiler_params=_cparams("arbitrary"),
        name="project_in",
    )(*args)


def _attn_kernel(*refs, n_local, bias_heads, use_sink):
    q_ref = refs[0]
    kv = refs[1:1 + 2 * n_local]
    kc_ref, vc_ref = refs[1 + 2 * n_local], refs[2 + 2 * n_local]
    pos = 3 + 2 * n_local
    bias_ref = sink_ref = None
    if n_local:
        bias_ref = refs[pos]
        pos += 1
    if use_sink:
        sink_ref = refs[pos]
        pos += 1
    o_ref = refs[pos]

    q = q_ref[0] * (HEAD_DIM ** -0.5)
    tq = q.shape[0]
    lane_head = lax.broadcasted_iota(jnp.int32, (1, BW), 1) // HEAD_DIM
    acc = jnp.zeros((tq, BW), F32)
    for h in range(N_HEADS):
        sel = lane_head == h
        qh = jnp.where(sel, q, jnp.zeros_like(q))
        scores = []
        for j in range(n_local):
            b = bias_ref[0, h if bias_heads > 1 else 0, :, j * tq:(j + 1) * tq]
            scores.append(_dot_nt(qh, kv[2 * j][0]) + b)
        scores.append(_dot_nt(qh, kc_ref[0]))
        mx = scores[0].max(axis=-1, keepdims=True)
        for s in scores[1:]:
            mx = jnp.maximum(mx, s.max(axis=-1, keepdims=True))
        if use_sink:
            snk = sink_ref[h:h + 1, 0:1]
            mx = jnp.maximum(mx, snk)
        den = jnp.exp(snk - mx) if use_sink else jnp.zeros_like(mx)
        o = jnp.zeros((tq, BW), F32)
        vals = [kv[2 * j + 1] for j in range(n_local)] + [vc_ref]
        for s, v_ref in zip(scores, vals):
            p = jnp.exp(s - mx)
            den = den + p.sum(axis=-1, keepdims=True)
            o = o + _dot(p.astype(BF16), v_ref[0])
        acc = acc + jnp.where(sel, o / den, 0.0)
    o_ref[0] = acc.astype(BF16)


def _local_attention(z, z_c, q_col, k_col, v_col, bias, sink_rows):
    b, s, _ = z.shape
    l = z_c.shape[1]
    tq = ATT_TQ
    nq = s // tq
    assert nq >= 3 and l == BW

    def kv_spec(col, j):
        return pl.BlockSpec((1, tq, BW), lambda bi, i: (bi, jnp.clip(i - 1, 0, nq - 3) + j, col))

    def pat(i):
        return jnp.where(i == 0, 0, jnp.where(i == nq - 1, 2, 1))

    in_specs = [pl.BlockSpec((1, tq, BW), lambda bi, i: (bi, i, q_col))]
    args = [z]
    for j in range(3):
        in_specs += [kv_spec(k_col, j), kv_spec(v_col, j)]
        args += [z, z]
    in_specs += [pl.BlockSpec((1, l, BW), lambda bi, i: (bi, 0, k_col)),
                 pl.BlockSpec((1, l, BW), lambda bi, i: (bi, 0, v_col)),
                 pl.BlockSpec((1,) + bias.shape[1:], lambda bi, i: (pat(i), 0, 0, 0))]
    args += [z_c, z_c, bias]
    if sink_rows is not None:
        in_specs.append(_const_spec(sink_rows.shape))
        args.append(sink_rows)
    return pl.pallas_call(
        functools.partial(_attn_kernel, n_local=3, bias_heads=bias.shape[1], use_sink=sink_rows is not None),
        grid=(b, nq),
        in_specs=in_specs,
        out_specs=pl.BlockSpec((1, tq, BW), lambda bi, i: (bi, i, 0)),
        out_shape=jax.ShapeDtypeStruct((b, s, BW), BF16),
        compiler_params=_cparams("arbitrary", "arbitrary"),
        name="local_attention",
    )(*args)


def _ctx_attention(z_c, q_col, k_col, v_col, sink_rows):
    b, l, _ = z_c.shape
    in_specs = [pl.BlockSpec((1, l, BW), lambda bi, c=col: (bi, 0, c)) for col in (q_col, k_col, v_col)]
    args = [z_c, z_c, z_c]
    if sink_rows is not None:
        in_specs.append(_const_spec(sink_rows.shape))
        args.append(sink_rows)
    return pl.pallas_call(
        functools.partial(_attn_kernel, n_local=0, bias_heads=1, use_sink=sink_rows is not None),
        grid=(b,),
        in_specs=in_specs,
        out_specs=pl.BlockSpec((1, l, BW), lambda bi: (bi, 0, 0)),
        out_shape=jax.ShapeDtypeStruct((b, l, BW), BF16),
        compiler_params=_cparams("arbitrary"),
        name="ctx_attention",
    )(*args)


def _na_bias(rpb, rows):
    tile_rows = ATT_TQ // GRID_W
    nq = rows // tile_rows
    col = np.arange(GRID_W)
    cs = np.clip(col - NA_COLS // 2, 0, GRID_W - NA_COLS)[:, None]
    kc = col[None, :]
    sel_c = ((kc - col[:, None] + NA_COLS - 1)[None] == np.arange(2 * NA_COLS - 1)[:, None, None]) \
        & ((kc >= cs) & (kc < cs + NA_COLS))[None]
    sel_r = []
    for i in (0, 1, nq - 1):
        base = min(max(i - 1, 0), nq - 3)
        qr = (tile_rows * i + np.arange(tile_rows))[:, None]
        kr = (tile_rows * base + np.arange(3 * tile_rows))[None, :]
        rs = np.clip(qr - NA_ROWS // 2, 0, rows - NA_ROWS)
        sel_r.append(((kr - qr + NA_ROWS - 1)[None] == np.arange(2 * NA_ROWS - 1)[:, None, None])
                     & ((kr >= rs) & (kr < rs + NA_ROWS))[None])
    sel_r = np.stack(sel_r).astype(np.float32)
    sel_c = sel_c.astype(np.float32)
    t1 = jnp.einsum('paqk,hab->phqkb', sel_r, rpb.astype(F32), precision=HIGHEST)
    bias = jnp.einsum('phqkb,bcd->phqckd', t1, sel_c, precision=HIGHEST)
    valid = np.einsum('paqk,bcd->pqckd', sel_r, sel_c) > 0.5
    mask = np.where(valid, 0.0, NEG_INF).astype(np.float32)[:, None]
    return (bias + mask).reshape(3, rpb.shape[0], ATT_TQ, 3 * ATT_TQ)


def _swa_bias(s):
    nq = s // ATT_TQ
    tq = np.arange(ATT_TQ)[:, None]
    tk = np.arange(3 * ATT_TQ)[None, :]
    out = []
    for i in (0, 1, nq - 1):
        base = min(max(i - 1, 0), nq - 3)
        valid = np.abs((ATT_TQ * base + tk) - (ATT_TQ * i + tq)) <= SWA_WINDOW
        out.append(np.where(valid, 0.0, NEG_INF).astype(np.float32)[None])
    return jnp.asarray(np.stack(out))


def _rope_tables(s):
    nq = HEAD_DIM // 4
    t = jnp.arange(s)
    inv = ROPE_BASE ** (-jnp.arange(nq, dtype=F32) / nq)
    ang_r = (t // GRID_W).astype(F32)[:, None] * inv[None, :]
    ang_c = (t % GRID_W).astype(F32)[:, None] * inv[None, :]
    cos = jnp.concatenate([jnp.cos(ang_r)] * 2 + [jnp.cos(ang_c)] * 2, axis=-1)
    sin = jnp.concatenate([-jnp.sin(ang_r), jnp.sin(ang_r), -jnp.sin(ang_c), jnp.sin(ang_c)], axis=-1)
    return jnp.tile(cos, (1, N_HEADS)), jnp.tile(sin, (1, N_HEADS))


def _gmlp_kernel(u_ref, v_ref, g_ref, b_ref, w_ref, bs_ref, o_ref):
    lane_grp = lax.broadcasted_iota(jnp.int32, (1, BW), 1) // (BW // GMLP_GROUPS)
    for c in range(u_ref.shape[0] // GMLP_CHUNK):
        rows = slice(c * GMLP_CHUNK, (c + 1) * GMLP_CHUNK)
        u = jax.nn.gelu(u_ref[rows, :].astype(F32))
        v = jax.nn.gelu(v_ref[rows, :].astype(F32))
        mu = jnp.mean(v, axis=-1, keepdims=True)
        vc = v - mu
        v = vc * lax.rsqrt(jnp.mean(vc * vc, axis=-1, keepdims=True) + EPS) * g_ref[...] + b_ref[...]
        v = v.astype(BF16)
        stack = jnp.concatenate([jnp.where(lane_grp == g, v, jnp.zeros_like(v)) for g in range(GMLP_GROUPS)], axis=0)
        sg = _dot(w_ref[...], stack) + bs_ref[...]
        o_ref[rows, :] = (u * sg).astype(BF16)


def _gmlp(z2, ln_g, ln_b, ws, bs, tm):
    m = z2.shape[0]
    w_cat = jnp.concatenate([ws[g] for g in range(GMLP_GROUPS)], axis=1).astype(BF16)
    bs_l = jnp.repeat(bs.T.astype(F32), BW // GMLP_GROUPS, axis=1)
    return pl.pallas_call(
        _gmlp_kernel,
        grid=(m // tm,),
        in_specs=[pl.BlockSpec((tm, BW), lambda i: (i, Z_GU)),
                  pl.BlockSpec((tm, BW), lambda i: (i, Z_GV)),
                  _const_spec((1, BW)), _const_spec((1, BW)),
                  _const_spec(w_cat.shape), _const_spec(bs_l.shape)],
        out_specs=pl.BlockSpec((tm, BW), lambda i: (i, 0)),
        out_shape=jax.ShapeDtypeStruct((m, BW), BF16),
        compiler_params=_cparams("arbitrary"),
        name="gmlp",
    )(z2, z2, ln_g.reshape(1, BW).astype(F32), ln_b.reshape(1, BW).astype(F32), w_cat, bs_l)


def _s5_kernel(su_ref, suc_ref, wend_ref, toep_ref, wc_ref, a16_ref, *rest, ctx_out):
    if ctx_out:
        y_ref, yc_ref, ut_ref, utc_ref, up_ref, st_ref = rest
    else:
        y_ref, ut_ref, utc_ref, up_ref, st_ref = rest
    nj, nc = su_ref.shape[1] // S5_T, suc_ref.shape[1] // S5_T
    lanes = su_ref.shape[2]
    ng = lanes // S5_GROUP
    hw = ng * S5_STATE
    cpad = utc_ref.shape[1]

    def grp(g, t):
        return slice(g * BW + t * S5_GROUP, g * BW + (t + 1) * S5_GROUP)

    for t in range(S5_T):
        at = su_ref[0, pl.ds(t, nj, stride=S5_T), :].T
        ct = jnp.concatenate([suc_ref[0, pl.ds(t, nc, stride=S5_T), :],
                              jnp.zeros((cpad - nc, lanes), F32)], axis=0).T
        for g in range(ng):
            ut_ref[grp(g, t), :] = at[g * S5_GROUP:(g + 1) * S5_GROUP, :]
            utc_ref[grp(g, t), :] = ct[g * S5_GROUP:(g + 1) * S5_GROUP, :]

    for p in range(ng // 2):
        blocks = []
        for g in (2 * p, 2 * p + 1):
            xt = ut_ref[g * BW:(g + 1) * BW, :].T
            ct = utc_ref[g * BW:(g + 1) * BW, :].T[:nc]
            blocks.append(jnp.concatenate([ct, xt], axis=0))
        up = jnp.concatenate(blocks, axis=1).astype(BF16)
        up_ref[:, p * 2 * BW:(p + 1) * 2 * BW] = up
        e = _dot(up, wend_ref[p])
        for k in range(4):
            st_ref[:, k * hw + p * 128:k * hw + (p + 1) * 128] = e[:, k * 128:(k + 1) * 128]

    n_tiles, n_ctiles = (nc + nj) // 8, nc // 8
    a = [a16_ref[k:k + 1, :] for k in range(4)]

    def tile_step(k, carry):
        fr, fi, rr, ri = carry
        kf = pl.multiple_of(k * 8, 8)
        kr = jnp.where(k < n_ctiles, n_ctiles - 1 - k, n_tiles - 1 - (k - n_ctiles))
        kr = pl.multiple_of(kr * 8, 8)
        ef_r, ef_i = st_ref[pl.ds(kf, 8), 0:hw], st_ref[pl.ds(kf, 8), hw:2 * hw]
        er_r, er_i = st_ref[pl.ds(kr, 8), 2 * hw:3 * hw], st_ref[pl.ds(kr, 8), 3 * hw:4 * hw]
        hf_r, hf_i, hr_r, hr_i = [], [], [None] * 8, [None] * 8
        for r in range(8):
            hf_r.append(fr)
            hf_i.append(fi)
            fr, fi = a[0] * fr - a[1] * fi + ef_r[r:r + 1], a[0] * fi + a[1] * fr + ef_i[r:r + 1]
            q = 7 - r
            hr_r[q], hr_i[q] = rr, ri
            rr, ri = a[2] * rr - a[3] * ri + er_r[q:q + 1], a[2] * ri + a[3] * rr + er_i[q:q + 1]
        st_ref[pl.ds(kf, 8), 0:hw] = jnp.concatenate(hf_r, axis=0)
        st_ref[pl.ds(kf, 8), hw:2 * hw] = jnp.concatenate(hf_i, axis=0)
        st_ref[pl.ds(kr, 8), 2 * hw:3 * hw] = jnp.concatenate(hr_r, axis=0)
        st_ref[pl.ds(kr, 8), 3 * hw:4 * hw] = jnp.concatenate(hr_i, axis=0)
        return fr, fi, rr, ri

    zero = jnp.zeros((1, hw), F32)
    lax.fori_loop(0, n_tiles, tile_step, (zero, zero, zero, zero))

    for p in range(ng // 2):
        h = jnp.concatenate([st_ref[:, k * hw + p * 128:k * hw + (p + 1) * 128] for k in range(4)], axis=1)
        y = _dot(up_ref[:, p * 2 * BW:(p + 1) * 2 * BW], toep_ref[p]) + _dot(h.astype(BF16), wc_ref[p])
        for gg in range(2):
            g = 2 * p + gg
            yg = y[:, gg * BW:(gg + 1) * BW]
            ut_ref[g * BW:(g + 1) * BW, :] = yg[nc:, :].T
            if ctx_out:
                utc_ref[g * BW:(g + 1) * BW, :] = jnp.concatenate(
                    [yg[:nc, :], jnp.zeros((cpad - nc, BW), F32)], axis=0).T

    for t in range(S5_T):
        z = jnp.concatenate([ut_ref[grp(g, t), :] for g in range(ng)], axis=0)
        y_ref[0, pl.ds(t, nj, stride=S5_T), :] = z.T
        if ctx_out:
            zc = jnp.concatenate([utc_ref[grp(g, t), :] for g in range(ng)], axis=0)
            yc_ref[0, pl.ds(t, nc, stride=S5_T), :] = zc.T[:nc]


def _s5_matrices(a_re, a_im, log_dt, b_re, b_im, c_re, c_im, d):
    g, p, c, t = S5_GROUPS, S5_STATE, S5_GROUP, S5_T
    lam_re = jnp.minimum(a_re.astype(F32), -1e-4)
    lam_im = a_im.astype(F32)
    dt = jnp.exp(log_dt.astype(F32))[..., None]

    def a_pow(n):
        n = jnp.asarray(n, F32)[:, None, None, None]
        mag = jnp.exp(n * (lam_re * dt)[None])
        ang = n * (lam_im * dt)[None]
        return mag * jnp.cos(ang), mag * jnp.sin(ang)

    ab_re, ab_im = (v[0] for v in a_pow([1.0]))
    den = lam_re * lam_re + lam_im * lam_im
    k_re = ((ab_re - 1.0) * lam_re + ab_im * lam_im) / den
    k_im = (ab_im * lam_re - (ab_re - 1.0) * lam_im) / den
    br, bi = b_re.astype(F32), b_im.astype(F32)
    bb_re = k_re[..., None] * br - k_im[..., None] * bi
    bb_im = k_re[..., None] * bi + k_im[..., None] * br
    cr, ci = c_re.astype(F32), c_im.astype(F32)

    ein = functools.partial(jnp.einsum, precision=HIGHEST)
    pw_re, pw_im = a_pow(np.arange(t + 1))
    ca_re = cr[None] * pw_re[:t, :, :, None, :] - ci[None] * pw_im[:t, :, :, None, :]
    ca_im = cr[None] * pw_im[:t, :, :, None, :] + ci[None] * pw_re[:t, :, :, None, :]
    lag = ein('ldgop,dgpi->ldgoi', ca_re, bb_re) - ein('ldgop,dgpi->ldgoi', ca_im, bb_im)
    s_i = np.arange(t)[:, None]
    t_i = np.arange(t)[None, :]
    fwd = jnp.where((t_i >= s_i)[:, :, None, None, None], lag[np.clip(t_i - s_i, 0, t - 1), 0], 0.0)
    rev = jnp.where((s_i >= t_i)[:, :, None, None, None], lag[np.clip(s_i - t_i, 0, t - 1), 1], 0.0)
    toep = jnp.transpose(fwd + rev, (2, 0, 4, 1, 3))
    skip = jnp.einsum('st,io->sito', jnp.eye(t, dtype=F32), jnp.eye(c, dtype=F32))[None] \
        * d.astype(F32).reshape(g, 1, 1, 1, c)
    toep = (toep + skip).reshape(g, t * c, t * c)

    def end_w(pr, pi, dirn):
        w_re = pr[..., None] * bb_re[dirn][None] - pi[..., None] * bb_im[dirn][None]
        w_im = pr[..., None] * bb_im[dirn][None] + pi[..., None] * bb_re[dirn][None]
        to = lambda w: jnp.transpose(w, (1, 0, 3, 2)).reshape(g, t * c, p)
        return to(w_re), to(w_im)

    wf_re, wf_im = end_w(pw_re[t - 1::-1, 0][:t], pw_im[t - 1::-1, 0][:t], 0)
    wr_re, wr_im = end_w(pw_re[:t, 1], pw_im[:t, 1], 1)

    def carry_w(pr, pi, dirn):
        m_re = cr[dirn][None] * pr[:, :, None, :] - ci[dirn][None] * pi[:, :, None, :]
        m_im = cr[dirn][None] * pi[:, :, None, :] + ci[dirn][None] * pr[:, :, None, :]
        to = lambda w: jnp.transpose(w, (1, 3, 0, 2)).reshape(g, p, t * c)
        return to(m_re), to(-m_im)

    cf_re, cf_im = carry_w(pw_re[1:t + 1, 0], pw_im[1:t + 1, 0], 0)
    cr_re, cr_im = carry_w(pw_re[t:0:-1, 1], pw_im[t:0:-1, 1], 1)

    eye2 = jnp.eye(2, dtype=F32)
    half = g // 2
    toep_p = jnp.einsum('xgab,gh->xgahb', toep.reshape(half, 2, t * c, t * c), eye2).reshape(half, 2 * t * c, 2 * t * c)
    wend = jnp.stack([wf_re, wf_im, wr_re, wr_im], axis=1)
    wend_p = jnp.einsum('xgkap,gh->xgakhp', wend.reshape(half, 2, 4, t * c, p), eye2).reshape(half, 2 * t * c, 8 * p)
    wc = jnp.stack([cf_re, cf_im, cr_re, cr_im], axis=1)
    wc_p = jnp.einsum('xgkpb,gh->xkgphb', wc.reshape(half, 2, 4, p, t * c), eye2).reshape(half, 8 * p, 2 * t * c)
    a16 = jnp.stack([pw_re[t, 0], pw_im[t, 0], pw_re[t, 1], pw_im[t, 1]], axis=0)
    return toep_p.astype(BF16), wend_p.astype(BF16), wc_p.astype(BF16), a16.reshape(4, g * p)


def _s5_mix(su_c, su_x, mats, ctx_out):
    toep_p, wend_p, wc_p, a16 = mats
    b, l, _ = su_c.shape
    s = su_x.shape[1]
    n_rows = (l + s) // S5_T
    halves = 2
    lanes = BW // halves
    ng = lanes // S5_GROUP
    pw = 2 * BW
    assert (s // S5_T) % 128 == 0 and (l // S5_T) % 8 == 0 and l // S5_T <= 128
    wspec = pl.BlockSpec((ng // 2, pw, pw), lambda bi, h: (h, 0, 0))
    out_specs = [pl.BlockSpec((1, s, lanes), lambda bi, h: (bi, 0, h))]
    out_shape = [jax.ShapeDtypeStruct((b, s, BW), F32)]
    if ctx_out:
        out_specs.append(pl.BlockSpec((1, l, lanes), lambda bi, h: (bi, 0, h)))
        out_shape.append(jax.ShapeDtypeStruct((b, l, BW), F32))
    res = pl.pallas_call(
        functools.partial(_s5_kernel, ctx_out=ctx_out),
        grid=(b, halves),
        in_specs=[pl.BlockSpec((1, s, lanes), lambda bi, h: (bi, 0, h)),
                  pl.BlockSpec((1, l, lanes), lambda bi, h: (bi, 0, h)),
                  wspec, wspec, wspec,
                  pl.BlockSpec((4, ng * S5_STATE), lambda bi, h: (0, h))],
        out_specs=out_specs,
        out_shape=out_shape,
        scratch_shapes=[pltpu.VMEM((ng * BW, s // S5_T), F32), pltpu.VMEM((ng * BW, 128), F32),
                        pltpu.VMEM((n_rows, ng * BW), BF16), pltpu.VMEM((n_rows, 4 * ng * S5_STATE), F32)],
        compiler_params=_cparams("arbitrary", "arbitrary"),
        name="s5_mixer",
    )(su_x, su_c, wend_p, toep_p, wc_p, a16)
    return (res[1] if ctx_out else None), res[0]


def _merge_kernel(zg_ref, oa_ref, ob_ref, ys_ref, od_ref, x_ref, mod_ref, g_ref, wb_ref, wo_ref, gw_ref, gb_ref,
                  o_ref):
    d = x_ref.shape[1]
    y = jax.nn.gelu(ys_ref[...].astype(F32))
    oc = (y * jax.nn.sigmoid(_dot(y.astype(BF16), gw_ref[...]) + gb_ref[...])).astype(BF16)
    outs = (oa_ref[...], ob_ref[...], oc, od_ref[...])
    m = None
    for i, o in enumerate(outs):
        term = jax.nn.sigmoid(zg_ref[:, i * d:(i + 1) * d].astype(F32)) * _dot(o, wb_ref[i])
        m = term if m is None else m + term
    mo = _dot(m.astype(BF16), wo_ref[...])
    o_ref[...] = x_ref[...] + mod_ref[0, 2:3, :] * _rms(mo, g_ref[...])


def _merge(z2, o_a, o_b, y_s, o_d, x2, mod, g_post, wb, wo, glu_w, glu_b, tiles_per_mod, tm):
    m, d = x2.shape
    row = lambda i: (i, 0)
    return pl.pallas_call(
        _merge_kernel,
        grid=(m // tm,),
        in_specs=[pl.BlockSpec((tm, Z_GATES_W), row)] + [pl.BlockSpec((tm, BW), row)] * 4
        + [pl.BlockSpec((tm, d), row),
           pl.BlockSpec((1, 8, d), lambda i: (i // tiles_per_mod, 0, 0)),
           _const_spec((1, d)), _const_spec(wb.shape), _const_spec(wo.shape),
           _const_spec(glu_w.shape), _const_spec((1, BW))],
        out_specs=pl.BlockSpec((tm, d), row),
        out_shape=jax.ShapeDtypeStruct((m, d), F32),
        compiler_params=_cparams("arbitrary"),
        name="merge",
    )(z2, o_a, o_b, y_s, o_d, x2, mod, g_post.reshape(1, d), wb, wo, glu_w, glu_b.reshape(1, BW).astype(F32))


def _ffn_dense_kernel(x_ref, mod_ref, gpre_ref, gpost_ref, w1_ref, w3_ref, w2_ref, o_ref, h_ref, acc_ref):
    j = pl.program_id(1)

    @pl.when(j == 0)
    def _():
        h = _rms(x_ref[...], gpre_ref[...]) * (1.0 + mod_ref[0, 4:5, :]) + mod_ref[0, 3:4, :]
        h_ref[...] = h.astype(BF16)
        acc_ref[...] = jnp.zeros_like(acc_ref)

    h = h_ref[...]
    a = _dot(h, w1_ref[...])
    a = (a * jax.nn.sigmoid(a)) * _dot(h, w3_ref[...])
    acc_ref[...] += _dot(a.astype(BF16), w2_ref[...])

    @pl.when(j == pl.num_programs(1) - 1)
    def _():
        o_ref[...] = x_ref[...] + mod_ref[0, 5:6, :] * _rms(acc_ref[...], gpost_ref[...])


def _ffn_dense(x2, mod, g_pre, g_post, w1, w3, w2, tiles_per_mod, tm, fc):
    m, d = x2.shape
    f = w1.shape[1]
    return pl.pallas_call(
        _ffn_dense_kernel,
        grid=(m // tm, f // fc),
        in_specs=[pl.BlockSpec((tm, d), lambda i, j: (i, 0)),
                  pl.BlockSpec((1, 8, d), lambda i, j: (i // tiles_per_mod, 0, 0)),
                  _const_spec((1, d)), _const_spec((1, d)),
                  pl.BlockSpec((d, fc), lambda i, j: (0, j)),
                  pl.BlockSpec((d, fc), lambda i, j: (0, j)),
                  pl.BlockSpec((fc, d), lambda i, j: (j, 0))],
        out_specs=pl.BlockSpec((tm, d), lambda i, j: (i, 0)),
        out_shape=jax.ShapeDtypeStruct((m, d), F32),
        scratch_shapes=[pltpu.VMEM((tm, d), BF16), pltpu.VMEM((tm, d), F32)],
        compiler_params=_cparams("arbitrary", "arbitrary"),
        name="ffn_dense",
    )(x2, mod, g_pre.reshape(1, d), g_post.reshape(1, d), w1, w3, w2)


def _route_kernel(x_ref, mod_ref, gpre_ref, r_ref, h_ref, idx_ref, wt_ref):
    h = _rms(x_ref[...], gpre_ref[...]) * (1.0 + mod_ref[0, 4:5, :]) + mod_ref[0, 3:4, :]
    h_ref[...] = h.astype(BF16)
    logits = jnp.dot(h, r_ref[...], preferred_element_type=F32, precision=HIGHEST)
    lane = lax.broadcasted_iota(jnp.int32, logits.shape, 1)
    logits = jnp.where(lane < N_EXPERTS, logits, 2.0 * NEG_INF)
    m0 = logits.max(axis=-1, keepdims=True)
    i0 = jnp.where(logits == m0, lane, 128).min(axis=-1, keepdims=True)
    rest = jnp.where(lane == i0, NEG_INF, logits)
    m1 = rest.max(axis=-1, keepdims=True)
    i1 = jnp.where(rest == m1, lane, 128).min(axis=-1, keepdims=True)
    e = jnp.exp(m1 - m0)
    w0 = 1.0 / (1.0 + e)
    idx_ref[...] = jnp.where(lane == 0, i0, jnp.where(lane == 1, i1, 0))
    wt_ref[...] = jnp.where(lane == 0, w0, jnp.where(lane == 1, e * w0, 0.0))


def _route(x2, mod, g_pre, router, tiles_per_mod, tm):
    m, d = x2.shape
    r_pad = jnp.pad(router.astype(F32), ((0, 0), (0, 128 - N_EXPERTS)))
    row = lambda i: (i, 0)
    return pl.pallas_call(
        _route_kernel,
        grid=(m // tm,),
        in_specs=[pl.BlockSpec((tm, d), row),
                  pl.BlockSpec((1, 8, d), lambda i: (i // tiles_per_mod, 0, 0)),
                  _const_spec((1, d)), _const_spec((d, 128))],
        out_specs=[pl.BlockSpec((tm, d), row), pl.BlockSpec((tm, 128), row), pl.BlockSpec((tm, 128), row)],
        out_shape=[jax.ShapeDtypeStruct((m, d), BF16), jax.ShapeDtypeStruct((m, 128), jnp.int32),
                   jax.ShapeDtypeStruct((m, 128), F32)],
        compiler_params=_cparams("arbitrary"),
        name="moe_route",
    )(x2, mod, g_pre.reshape(1, d), r_pad)


def _ffn_grouped_kernel(te_ref, nv_ref, h_ref, w1_ref, w3_ref, w2_ref, o_ref, acc_ref):
    i, j = pl.program_id(0), pl.program_id(1)
    last = pl.num_programs(1) - 1
    valid = i < nv_ref[0]

    @pl.when(j == 0)
    def _():
        acc_ref[...] = jnp.zeros_like(acc_ref)

    @pl.when(valid)
    def _():
        h = h_ref[...]
        a = _dot(h, w1_ref[0])
        a = (a * jax.nn.sigmoid(a)) * _dot(h, w3_ref[0])
        acc_ref[...] += _dot(a.astype(BF16), w2_ref[0])

    @pl.when(j == last)
    def _():
        o_ref[...] = acc_ref[...].astype(BF16)


def _ffn_grouped(h_sorted, tile_expert, n_valid, w1, w3, w2, tm, fc):
    r, d = h_sorted.shape
    f = w1.shape[2]
    nj = f // fc

    def col(i, j, nv):
        return jnp.where(i < nv[0], j, nj - 1)

    grid_spec = pltpu.PrefetchScalarGridSpec(
        num_scalar_prefetch=2,
        grid=(r // tm, nj),
        in_specs=[pl.BlockSpec((tm, d), lambda i, j, te, nv: (i, 0)),
                  pl.BlockSpec((1, d, fc), lambda i, j, te, nv: (te[i], 0, col(i, j, nv))),
                  pl.BlockSpec((1, d, fc), lambda i, j, te, nv: (te[i], 0, col(i, j, nv))),
                  pl.BlockSpec((1, fc, d), lambda i, j, te, nv: (te[i], col(i, j, nv), 0))],
        out_specs=pl.BlockSpec((tm, d), lambda i, j, te, nv: (i, 0)),
        scratch_shapes=[pltpu.VMEM((tm, d), F32)],
    )
    return pl.pallas_call(
        _ffn_grouped_kernel,
        grid_spec=grid_spec,
        out_shape=jax.ShapeDtypeStruct((r, d), BF16),
        compiler_params=_cparams("arbitrary", "arbitrary"),
        name="ffn_grouped",
    )(tile_expert, n_valid, h_sorted, w1, w3, w2)


def _combine_kernel(o0_ref, o1_ref, wt_ref, x_ref, mod_ref, gpost_ref, o_ref):
    y = wt_ref[:, 0:1] * o0_ref[...].astype(F32) + wt_ref[:, 1:2] * o1_ref[...].astype(F32)
    o_ref[...] = x_ref[...] + mod_ref[0, 5:6, :] * _rms(y, gpost_ref[...])


def _combine(o0, o1, wts, x2, mod, g_post, tiles_per_mod, tm):
    m, d = x2.shape
    row = lambda i: (i, 0)
    return pl.pallas_call(
        _combine_kernel,
        grid=(m // tm,),
        in_specs=[pl.BlockSpec((tm, d), row), pl.BlockSpec((tm, d), row), pl.BlockSpec((tm, 128), row),
                  pl.BlockSpec((tm, d), row),
                  pl.BlockSpec((1, 8, d), lambda i: (i // tiles_per_mod, 0, 0)),
                  _const_spec((1, d))],
        out_specs=pl.BlockSpec((tm, d), row),
        out_shape=jax.ShapeDtypeStruct((m, d), F32),
        compiler_params=_cparams("arbitrary"),
        name="moe_combine",
    )(o0, o1, wts, x2, mod, g_post.reshape(1, d))


def _moe(x2, mod, g_pre, g_post, router, w1, w3, w2, tiles_per_mod, tm):
    m, d = x2.shape
    h, idx, wts = _route(x2, mod, g_pre, router, tiles_per_mod, tm)
    e_flat = idx[:, :2].reshape(-1)
    onehot = (e_flat[:, None] == jnp.arange(N_EXPERTS)[None, :]).astype(jnp.int32)
    rank = jnp.take_along_axis(jnp.cumsum(onehot, axis=0), e_flat[:, None], axis=1)[:, 0] - 1
    counts = onehot.sum(axis=0)
    padded = (counts + tm - 1) // tm * tm
    ends = jnp.cumsum(padded)
    pos = (ends - padded)[e_flat] + rank
    n_tiles = (2 * m) // tm + N_EXPERTS
    src = jnp.zeros((n_tiles * tm,), jnp.int32).at[pos].set(jnp.arange(2 * m, dtype=jnp.int32) // 2,
                                                            unique_indices=True, mode='promise_in_bounds')
    tile_start = jnp.arange(n_tiles, dtype=jnp.int32) * tm
    tile_expert = jnp.minimum((ends[None, :] <= tile_start[:, None]).sum(axis=1), N_EXPERTS - 1).astype(jnp.int32)
    n_valid = (ends[-1] // tm).astype(jnp.int32).reshape(1)
    rows = lambda a, i: a.at[i].get(mode='promise_in_bounds')
    out = _ffn_grouped(rows(h, src), tile_expert, n_valid, w1, w3, w2, tm, 896)
    pos2 = pos.reshape(m, 2)
    return _combine(rows(out, pos2[:, 0]), rows(out, pos2[:, 1]), wts, x2, mod, g_post, tiles_per_mod, tm)


def _z_weights(w_in):
    d = w_in.shape[0]
    hw = N_HEADS * HEAD_DIM
    kvw = SWA_KV_HEADS * HEAD_DIM
    sizes = (hw, hw, hw, hw, kvw, kvw, BW, BW, BW, Z_GATES_W)
    qa, ka, va, qd, kd, vd, gu, gv, su, gates = jnp.split(w_in, np.cumsum(sizes)[:-1].tolist(), axis=1)
    rep = N_HEADS // SWA_KV_HEADS
    dup = lambda w: jnp.repeat(w.reshape(d, SWA_KV_HEADS, 1, HEAD_DIM), rep, axis=2).reshape(d, hw)
    return jnp.concatenate([gates, qa, ka, va, qd, dup(kd), dup(vd), gu, gv, su], axis=1).astype(BF16)


def kernel(x, c, ctx, c_ctx, w_mod, b_mod, g_pre_mix, g_post_mix, g_pre_ffn, g_post_ffn, w_in, na_rpb, swa_sink, gmlp_ln_g, gmlp_ln_b, gmlp_ws, gmlp_bs, s5_a_re, s5_a_im, s5_log_dt, s5_b_re, s5_b_im, s5_c_re, s5_c_im, s5_d, s5_glu_w, s5_glu_b, w_branch, w_out, ffn_w1, ffn_w3, ffn_w2, moe_router, moe_w1, moe_w3, moe_w2):
    bsz, s, d = x.shape
    l = ctx.shape[1]
    depth = w_mod.shape[0]
    tm = 512
    rows = s // GRID_W

    c_all = jnp.zeros((8, d), F32).at[:bsz].set(c).at[bsz].set(c_ctx)
    mods = _modulation(c_all, w_mod, b_mod).reshape(depth, 8, 6, d)
    mods = jnp.pad(mods, ((0, 0), (0, 0), (0, 2), (0, 0)))
    rope_tabs = _rope_tables(s)
    swa_bias = _swa_bias(s)

    x2 = x.reshape(bsz * s, d)
    c2 = ctx.reshape(bsz * l, d)
    for i in range(depth):
        ctx_out = i < depth - 1
        mod_x = mods[i, :bsz]
        mod_c = mods[i, bsz:bsz + 1]
        w_z = _z_weights(w_in[i])
        zx, su_x = _project(x2, mod_x, g_pre_mix[i], w_z, s // tm, rope_tabs, tm)
        zc, su_c = _project(c2, mod_c, g_pre_mix[i], w_z, (bsz * l) // tm, None, tm)
        zx3 = zx.reshape(bsz, s, ZW)
        zc3 = zc.reshape(bsz, l, ZW)
        sink_rows = jnp.broadcast_to(jnp.pad(swa_sink[i].astype(F32), (0, 8 - N_HEADS))[:, None], (8, 128))

        o_a = _local_attention(zx3, zc3, Z_QA, Z_KA, Z_VA, _na_bias(na_rpb[i], rows), None)
        o_d = _local_attention(zx3, zc3, Z_QD, Z_KD, Z_VD, swa_bias, sink_rows)
        o_b = _gmlp(zx, gmlp_ln_g[i], gmlp_ln_b[i], gmlp_ws[i], gmlp_bs[i], tm)
        s5_mats = _s5_matrices(s5_a_re[i], s5_a_im[i], s5_log_dt[i], s5_b_re[i], s5_b_im[i],
                               s5_c_re[i], s5_c_im[i], s5_d[i])
        y_c, y_x = _s5_mix(su_c.reshape(bsz, l, BW), su_x.reshape(bsz, s, BW), s5_mats, ctx_out)

        wb = w_branch[i].astype(BF16)
        wo = w_out[i].astype(BF16)
        glu_w = s5_glu_w[i].astype(BF16)
        x2 = _merge(zx, o_a.reshape(bsz * s, BW), o_b, y_x.reshape(bsz * s, BW), o_d.reshape(bsz * s, BW),
                    x2, mod_x, g_post_mix[i], wb, wo, glu_w, s5_glu_b[i], s // tm, tm)
        if ctx_out:
            o_a_c = _ctx_attention(zc3, Z_QA, Z_KA, Z_VA, None)
            o_d_c = _ctx_attention(zc3, Z_QD, Z_KD, Z_VD, sink_rows)
            o_b_c = _gmlp(zc, gmlp_ln_g[i], gmlp_ln_b[i], gmlp_ws[i], gmlp_bs[i], tm)
            c2 = _merge(zc, o_a_c.reshape(bsz * l, BW), o_b_c, y_c.reshape(bsz * l, BW), o_d_c.reshape(bsz * l, BW),
                        c2, mod_c, g_post_mix[i], wb, wo, glu_w, s5_glu_b[i], (bsz * l) // tm, tm)

        j = i // 2
        if i % 2 == 0:
            w1, w3, w2 = ffn_w1[j].astype(BF16), ffn_w3[j].astype(BF16), ffn_w2[j].astype(BF16)
            x2 = _ffn_dense(x2, mod_x, g_pre_ffn[i], g_post_ffn[i], w1, w3, w2, s // tm, tm, 1408)
            if ctx_out:
                c2 = _ffn_dense(c2, mod_c, g_pre_ffn[i], g_post_ffn[i], w1, w3, w2, (bsz * l) // tm, tm, 1408)
        else:
            w1, w3, w2 = moe_w1[j].astype(BF16), moe_w3[j].astype(BF16), moe_w2[j].astype(BF16)
            x2 = _moe(x2, mod_x, g_pre_ffn[i], g_post_ffn[i], moe_router[j], w1, w3, w2, s // tm, tm)
            if ctx_out:
                c2 = _moe(c2, mod_c, g_pre_ffn[i], g_post_ffn[i], moe_router[j], w1, w3, w2, (bsz * l) // tm, tm)
    return x2.reshape(bsz, s, d)
```

```python
import functools
import math

import numpy as np
import jax
import jax.numpy as jnp
from jax import lax
from jax.experimental import pallas as pl
from jax.experimental.pallas import tpu as pltpu

F32 = jnp.float32
BF16 = jnp.bfloat16
HIGHEST = lax.Precision.HIGHEST

GRID_W = 64
HEAD_DIM = 64
N_HEADS = 4
NA_ROWS = 8
NA_COLS = 16
SWA_KV_HEADS = 2
SWA_WINDOW = 128
GMLP_GROUPS = 4
GMLP_CHUNK = 128
S5_GROUP = 16
S5_GROUPS = 16
S5_STATE = 64
N_EXPERTS = 8
ROPE_BASE = 10000.0
EPS = 1e-6
NEG_INF = -1e30
LOG2E = math.log2(math.e)
Q_SCALE = HEAD_DIM ** -0.5 * LOG2E

BW = 256
S5_T = 16
S5_ROWS = 8
Z_GATES_W = 4096
(Z_QA, Z_KA, Z_VA, Z_QD, Z_KD, Z_VD, Z_GU, Z_GV, Z_SU) = range(Z_GATES_W // BW, Z_GATES_W // BW + 9)
ZW = Z_GATES_W + 9 * BW

V7X_VMEM_LIMIT = 56 * 1024 * 1024
ATT_TQ = 256
ONES_ROWS = 16
ATT_CHAIN_HEADS = 2
SWA_LOCAL_BLOCKS = ATT_TQ // SWA_WINDOW + 2
MOE_GROUP_TILE = 1024
MOE_F_CHUNK = 512


def _cparams(*sem):
    return pltpu.CompilerParams(dimension_semantics=sem, vmem_limit_bytes=V7X_VMEM_LIMIT)


def _const_spec(shape):
    nd = len(shape)
    return pl.BlockSpec(shape, lambda *_: (0,) * nd, pipeline_mode=pl.Buffered(1))


def _dot(a, b):
    return jnp.dot(a, b, preferred_element_type=F32)


def _dot_nt(a, b):
    return lax.dot_general(a, b, (((1,), (1,)), ((), ())), preferred_element_type=F32)


def _sigmoid(x):
    return 0.5 * jnp.tanh(0.5 * x) + 0.5


def _rms(x, g):
    return x * lax.rsqrt(jnp.mean(x * x, axis=-1, keepdims=True) + EPS) * g


def _mod_kernel(c_ref, w_ref, b_ref, o_ref):
    c = c_ref[...]
    a = c * _sigmoid(c)
    o_ref[0] = jnp.dot(a, w_ref[0], preferred_element_type=F32, precision=HIGHEST) + b_ref[0]


def _modulation(c_all, w_mod, b_mod):
    depth, d, n = w_mod.shape
    tn = 1536
    return pl.pallas_call(
        _mod_kernel,
        grid=(depth, n // tn),
        in_specs=[pl.BlockSpec((8, d), lambda l, j: (0, 0)),
                  pl.BlockSpec((1, d, tn), lambda l, j: (l, 0, j)),
                  pl.BlockSpec((1, 1, tn), lambda l, j: (l, 0, j))],
        out_specs=pl.BlockSpec((1, 8, tn), lambda l, j: (l, 0, j)),
        out_shape=jax.ShapeDtypeStruct((depth, 8, n), F32),
        compiler_params=_cparams("arbitrary", "arbitrary"),
        name="modulation",
    )(c_all, w_mod, b_mod.reshape(depth, 1, n))


def _proj_kernel(*refs, rope, n_chunk):
    if rope:
        x_ref, mod_ref, g_ref, w_ref, cos_ref, sin_ref, o_ref, su_ref = refs
    else:
        x_ref, mod_ref, g_ref, w_ref, o_ref, su_ref = refs
    h = _rms(x_ref[...], g_ref[...]) * (1.0 + mod_ref[0, 1:2, :]) + mod_ref[0, 0:1, :]
    h = h.astype(BF16)
    rope_blocks = (Z_QD, Z_KD) if rope else ()
    for j in range(ZW // n_chunk):
        lo = j * n_chunk
        r = _dot(h, w_ref[:, lo:lo + n_chunk])
        blocks = range(lo // BW, (lo + n_chunk) // BW)
        if Z_SU in blocks:
            su_ref[...] = r[:, Z_SU * BW - lo:(Z_SU + 1) * BW - lo]
        if not any(b in rope_blocks or b in (Z_QA, Z_QD) for b in blocks):
            o_ref[:, lo:lo + n_chunk] = r.astype(BF16)
            continue
        for b in blocks:
            t = r[:, b * BW - lo:(b + 1) * BW - lo]
            if b in rope_blocks:
                lane = lax.broadcasted_iota(jnp.int32, (1, BW), 1)
                low_half = (lane % (HEAD_DIM // 2)) < (HEAD_DIM // 4)
                partner = jnp.where(low_half, pltpu.roll(t, BW - HEAD_DIM // 4, 1),
                                    pltpu.roll(t, HEAD_DIM // 4, 1))
                t = t * cos_ref[...] + partner * sin_ref[...]
            if b in (Z_QA, Z_QD):
                t = t * Q_SCALE
            o_ref[:, b * BW:(b + 1) * BW] = t.astype(BF16)


def _project(x2, mod, g, w_z, tiles_per_mod, rope_tabs, tm):
    m, d = x2.shape
    rope = rope_tabs is not None
    in_specs = [pl.BlockSpec((tm, d), lambda i: (i, 0)),
                pl.BlockSpec((1, 8, d), lambda i: (i // tiles_per_mod, 0, 0)),
                _const_spec((1, d)),
                _const_spec((d, ZW))]
    args = [x2, mod, g.reshape(1, d), w_z]
    if rope:
        n_rt = rope_tabs[0].shape[0] // tm
        in_specs += [pl.BlockSpec((tm, BW), lambda i: (i % n_rt, 0))] * 2
        args += list(rope_tabs)
    return pl.pallas_call(
        functools.partial(_proj_kernel, rope=rope, n_chunk=1280),
        grid=(m // tm,),
        in_specs=in_specs,
        out_specs=[pl.BlockSpec((tm, ZW), lambda i: (i, 0)), pl.BlockSpec((tm, BW), lambda i: (i, 0))],
        out_shape=[jax.ShapeDtypeStruct((m, ZW), BF16), jax.ShapeDtypeStruct((m, BW), F32)],
        compiler_params=_cparams("arbitrary"),
        name="project_in",
    )(*args)


def _attn_kernel(*refs, n_local, use_sink):
    q_ref = refs[0]
    kv = refs[1:1 + 2 * n_local]
    kc_ref, vc_ref = refs[1 + 2 * n_local], refs[2 + 2 * n_local]
    pos = 3 + 2 * n_local
    bias_ref = sink_ref = None
    if n_local:
        bias_ref = refs[pos]
        pos += 1
    if use_sink:
        sink_ref = refs[pos]
        pos += 1
    o_ref = refs[pos]

    q = q_ref[0]
    tq = q.shape[0]
    lane_head = lax.broadcasted_iota(jnp.int32, (1, BW), 1) // HEAD_DIM
    k_refs = [kv[2 * j] for j in range(n_local)] + [kc_ref]
    v_t = [r[0].T for r in [kv[2 * j + 1] for j in range(n_local)] + [vc_ref]]
    chains = [range(h0, h0 + ATT_CHAIN_HEADS) for h0 in range(0, N_HEADS, ATT_CHAIN_HEADS)]

    def score_stage(heads):
        cols = slice(heads[0] * tq, (heads[-1] + 1) * tq)
        q_heads = jnp.concatenate([jnp.where(lane_head == h, q, jnp.zeros_like(q)) for h in heads], axis=0)
        scores = []
        for j, k_ref in enumerate(k_refs):
            s = _dot_nt(k_ref[0], q_heads)
            if j < n_local:
                kb = k_ref.shape[1]
                s = s + bias_ref[0, j * kb:(j + 1) * kb, cols]
            scores.append(s)
        return scores

    def softmax_value_stage(heads, scores):
        cols = slice(heads[0] * tq, (heads[-1] + 1) * tq)
        mx = scores[0].max(axis=0, keepdims=True)
        for s in scores[1:]:
            mx = jnp.maximum(mx, s.max(axis=0, keepdims=True))
        if use_sink:
            mx = jnp.maximum(mx, sink_ref[:, cols])
        o = [jnp.zeros((HEAD_DIM + ONES_ROWS, tq), F32) for _ in heads]
        for s, vt in zip(scores, v_t):
            p = jnp.exp2(s - mx).astype(BF16)
            ones = jnp.ones((ONES_ROWS, vt.shape[1]), BF16)
            for i, h in enumerate(heads):
                lhs = jnp.concatenate([vt[h * HEAD_DIM:(h + 1) * HEAD_DIM, :], ones], axis=0)
                o[i] = o[i] + _dot(lhs, p[:, i * tq:(i + 1) * tq])
        res = []
        for i, h in enumerate(heads):
            den = o[i][HEAD_DIM:HEAD_DIM + 1, :]
            if use_sink:
                den = den + jnp.exp2(sink_ref[:, h * tq:(h + 1) * tq] - mx[:, i * tq:(i + 1) * tq])
            res.append(o[i][:HEAD_DIM, :] / den)
        return res

    outs = []
    pending = score_stage(chains[0])
    for c, heads in enumerate(chains):
        scores = pending
        if c + 1 < len(chains):
            pending = score_stage(chains[c + 1])
        outs += softmax_value_stage(heads, scores)
    o_ref[0] = jnp.concatenate(outs, axis=0).T.astype(BF16)


def _local_attention(z, z_c, q_col, k_col, v_col, bias, sink_rows, key_block, key_index):
    b, s, _ = z.shape
    l = z_c.shape[1]
    tq = ATT_TQ
    nq = s // tq
    n_local = bias.shape[1] // key_block
    assert nq >= 3 and l == BW

    def kv_spec(col, j):
        return pl.BlockSpec((1, key_block, BW), lambda bi, i: (bi, key_index(i, j), col))

    def pat(i):
        return jnp.where(i == 0, 0, jnp.where(i == nq - 1, 2, 1))

    in_specs = [pl.BlockSpec((1, tq, BW), lambda bi, i: (bi, i, q_col))]
    args = [z]
    for j in range(n_local):
        in_specs += [kv_spec(k_col, j), kv_spec(v_col, j)]
        args += [z, z]
    in_specs += [pl.BlockSpec((1, l, BW), lambda bi, i: (bi, 0, k_col)),
                 pl.BlockSpec((1, l, BW), lambda bi, i: (bi, 0, v_col)),
                 pl.BlockSpec((1,) + bias.shape[1:], lambda bi, i: (pat(i), 0, 0))]
    args += [z_c, z_c, bias]
    if sink_rows is not None:
        in_specs.append(_const_spec(sink_rows.shape))
        args.append(sink_rows)
    return pl.pallas_call(
        functools.partial(_attn_kernel, n_local=n_local, use_sink=sink_rows is not None),
        grid=(b, nq),
        in_specs=in_specs,
        out_specs=pl.BlockSpec((1, tq, BW), lambda bi, i: (bi, i, 0)),
        out_shape=jax.ShapeDtypeStruct((b, s, BW), BF16),
        compiler_params=_cparams("arbitrary", "arbitrary"),
        name="local_attention",
    )(*args)


def _ctx_attention(z_c, q_col, k_col, v_col, sink_rows):
    b, l, _ = z_c.shape
    in_specs = [pl.BlockSpec((1, l, BW), lambda bi, c=col: (bi, 0, c)) for col in (q_col, k_col, v_col)]
    args = [z_c, z_c, z_c]
    if sink_rows is not None:
        in_specs.append(_const_spec(sink_rows.shape))
        args.append(sink_rows)
    return pl.pallas_call(
        functools.partial(_attn_kernel, n_local=0, use_sink=sink_rows is not None),
        grid=(b,),
        in_specs=in_specs,
        out_specs=pl.BlockSpec((1, l, BW), lambda bi: (bi, 0, 0)),
        out_shape=jax.ShapeDtypeStruct((b, l, BW), BF16),
        compiler_params=_cparams("arbitrary"),
        name="ctx_attention",
    )(*args)


def _na_bias(rpb, rows):
    tile_rows = ATT_TQ // GRID_W
    nq = rows // tile_rows
    col = np.arange(GRID_W)
    cs = np.clip(col - NA_COLS // 2, 0, GRID_W - NA_COLS)[:, None]
    kc = col[None, :]
    sel_c = ((kc - col[:, None] + NA_COLS - 1)[None] == np.arange(2 * NA_COLS - 1)[:, None, None]) \
        & ((kc >= cs) & (kc < cs + NA_COLS))[None]
    sel_r = []
    for i in (0, 1, nq - 1):
        base = min(max(i - 1, 0), nq - 3)
        qr = (tile_rows * i + np.arange(tile_rows))[:, None]
        kr = (tile_rows * base + np.arange(3 * tile_rows))[None, :]
        rs = np.clip(qr - NA_ROWS // 2, 0, rows - NA_ROWS)
        sel_r.append(((kr - qr + NA_ROWS - 1)[None] == np.arange(2 * NA_ROWS - 1)[:, None, None])
                     & ((kr >= rs) & (kr < rs + NA_ROWS))[None])
    sel_r = np.stack(sel_r).astype(np.float32)
    sel_c = sel_c.astype(np.float32)
    t1 = jnp.einsum('paqk,hab->phqkb', sel_r, rpb.astype(F32), precision=HIGHEST)
    bias = jnp.einsum('phqkb,bcd->pkdhqc', t1, sel_c, precision=HIGHEST)
    valid = np.einsum('paqk,bcd->pkdqc', sel_r, sel_c) > 0.5
    mask = np.where(valid, 0.0, NEG_INF).astype(np.float32)[:, :, :, None]
    return (bias * LOG2E + mask).reshape(3, 3 * ATT_TQ, rpb.shape[0] * ATT_TQ)


def _swa_key_index(i, j, n_blocks):
    return jnp.clip(i * (ATT_TQ // SWA_WINDOW) - 1 + j, 0, n_blocks - 1)


def _swa_bias(s):
    nq = s // ATT_TQ
    tq = np.arange(ATT_TQ)[:, None]
    tk = np.arange(SWA_LOCAL_BLOCKS * SWA_WINDOW)[None, :]
    out = []
    for i in (0, 1, nq - 1):
        kpos = ATT_TQ * i - SWA_WINDOW + tk
        valid = (np.abs(kpos - (ATT_TQ * i + tq)) <= SWA_WINDOW) & (kpos >= 0) & (kpos < s)
        out.append(np.tile(np.where(valid, 0.0, NEG_INF).astype(np.float32).T, (1, N_HEADS)))
    return jnp.asarray(np.stack(out))


def _rope_tables(s):
    nq = HEAD_DIM // 4
    t = jnp.arange(s)
    inv = ROPE_BASE ** (-jnp.arange(nq, dtype=F32) / nq)
    ang_r = (t // GRID_W).astype(F32)[:, None] * inv[None, :]
    ang_c = (t % GRID_W).astype(F32)[:, None] * inv[None, :]
    cos = jnp.concatenate([jnp.cos(ang_r)] * 2 + [jnp.cos(ang_c)] * 2, axis=-1)
    sin = jnp.concatenate([-jnp.sin(ang_r), jnp.sin(ang_r), -jnp.sin(ang_c), jnp.sin(ang_c)], axis=-1)
    return jnp.tile(cos, (1, N_HEADS)), jnp.tile(sin, (1, N_HEADS))


def _gmlp_kernel(u_ref, v_ref, g_ref, b_ref, w_ref, bs_ref, o_ref):
    lane_grp = lax.broadcasted_iota(jnp.int32, (1, BW), 1) // (BW // GMLP_GROUPS)
    for c in range(u_ref.shape[0] // GMLP_CHUNK):
        rows = slice(c * GMLP_CHUNK, (c + 1) * GMLP_CHUNK)
        u = jax.nn.gelu(u_ref[rows, :].astype(F32))
        v = jax.nn.gelu(v_ref[rows, :].astype(F32))
        mu = jnp.mean(v, axis=-1, keepdims=True)
        vc = v - mu
        v = vc * lax.rsqrt(jnp.mean(vc * vc, axis=-1, keepdims=True) + EPS) * g_ref[...] + b_ref[...]
        v = v.astype(BF16)
        stack = jnp.concatenate([jnp.where(lane_grp == g, v, jnp.zeros_like(v)) for g in range(GMLP_GROUPS)], axis=0)
        sg = _dot(w_ref[...], stack) + bs_ref[...]
        o_ref[rows, :] = (u * sg).astype(BF16)


def _gmlp(z2, ln_g, ln_b, ws, bs, tm):
    m = z2.shape[0]
    w_cat = jnp.concatenate([ws[g] for g in range(GMLP_GROUPS)], axis=1).astype(BF16)
    bs_l = jnp.repeat(bs.T.astype(F32), BW // GMLP_GROUPS, axis=1)
    return pl.pallas_call(
        _gmlp_kernel,
        grid=(m // tm,),
        in_specs=[pl.BlockSpec((tm, BW), lambda i: (i, Z_GU)),
                  pl.BlockSpec((tm, BW), lambda i: (i, Z_GV)),
                  _const_spec((1, BW)), _const_spec((1, BW)),
                  _const_spec(w_cat.shape), _const_spec(bs_l.shape)],
        out_specs=pl.BlockSpec((tm, BW), lambda i: (i, 0)),
        out_shape=jax.ShapeDtypeStruct((m, BW), BF16),
        compiler_params=_cparams("arbitrary"),
        name="gmlp",
    )(z2, z2, ln_g.reshape(1, BW).astype(F32), ln_b.reshape(1, BW).astype(F32), w_cat, bs_l)


def _s5_kernel(su_ref, suc_ref, wend_ref, toep_ref, wc_ref, a16_ref, *rest, ctx_out):
    if ctx_out:
        y_ref, yc_ref, ut_ref, utc_ref, up_ref, st_ref = rest
    else:
        y_ref, ut_ref, utc_ref, up_ref, st_ref = rest
    nj, nc = su_ref.shape[1] // S5_T, suc_ref.shape[1] // S5_T
    lanes = su_ref.shape[2]
    ng = lanes // S5_GROUP
    hw = ng * S5_STATE
    cpad = utc_ref.shape[1]

    def grp(g, t):
        return slice(g * BW + t * S5_GROUP, g * BW + (t + 1) * S5_GROUP)

    for t in range(S5_T):
        at = su_ref[0, pl.ds(t, nj, stride=S5_T), :].T
        ct = jnp.concatenate([suc_ref[0, pl.ds(t, nc, stride=S5_T), :],
                              jnp.zeros((cpad - nc, lanes), F32)], axis=0).T
        for g in range(ng):
            ut_ref[grp(g, t), :] = at[g * S5_GROUP:(g + 1) * S5_GROUP, :]
            utc_ref[grp(g, t), :] = ct[g * S5_GROUP:(g + 1) * S5_GROUP, :]

    for p in range(ng // 2):
        blocks = []
        for g in (2 * p, 2 * p + 1):
            xt = ut_ref[g * BW:(g + 1) * BW, :].T
            ct = utc_ref[g * BW:(g + 1) * BW, :].T[:nc]
            blocks.append(jnp.concatenate([ct, xt], axis=0))
        up = jnp.concatenate(blocks, axis=1).astype(BF16)
        up_ref[:, p * 2 * BW:(p + 1) * 2 * BW] = up
        e = _dot(up, wend_ref[p])
        for k in range(4):
            st_ref[:, k * hw + p * 128:k * hw + (p + 1) * 128] = e[:, k * 128:(k + 1) * 128]

    n_tiles, n_ctiles = (nc + nj) // 8, nc // 8
    a = [a16_ref[k:k + 1, :] for k in range(4)]

    def tile_step(k, carry):
        fr, fi, rr, ri = carry
        kf = pl.multiple_of(k * 8, 8)
        kr = jnp.where(k < n_ctiles, n_ctiles - 1 - k, n_tiles - 1 - (k - n_ctiles))
        kr = pl.multiple_of(kr * 8, 8)
        ef_r, ef_i = st_ref[pl.ds(kf, 8), 0:hw], st_ref[pl.ds(kf, 8), hw:2 * hw]
        er_r, er_i = st_ref[pl.ds(kr, 8), 2 * hw:3 * hw], st_ref[pl.ds(kr, 8), 3 * hw:4 * hw]
        hf_r, hf_i, hr_r, hr_i = [], [], [None] * 8, [None] * 8
        for r in range(8):
            hf_r.append(fr)
            hf_i.append(fi)
            fr, fi = a[0] * fr - a[1] * fi + ef_r[r:r + 1], a[0] * fi + a[1] * fr + ef_i[r:r + 1]
            q = 7 - r
            hr_r[q], hr_i[q] = rr, ri
            rr, ri = a[2] * rr - a[3] * ri + er_r[q:q + 1], a[2] * ri + a[3] * rr + er_i[q:q + 1]
        st_ref[pl.ds(kf, 8), 0:hw] = jnp.concatenate(hf_r, axis=0)
        st_ref[pl.ds(kf, 8), hw:2 * hw] = jnp.concatenate(hf_i, axis=0)
        st_ref[pl.ds(kr, 8), 2 * hw:3 * hw] = jnp.concatenate(hr_r, axis=0)
        st_ref[pl.ds(kr, 8), 3 * hw:4 * hw] = jnp.concatenate(hr_i, axis=0)
        return fr, fi, rr, ri

    zero = jnp.zeros((1, hw), F32)
    lax.fori_loop(0, n_tiles, tile_step, (zero, zero, zero, zero))

    for p in range(ng // 2):
        h = jnp.concatenate([st_ref[:, k * hw + p * 128:k * hw + (p + 1) * 128] for k in range(4)], axis=1)
        y = _dot(up_ref[:, p * 2 * BW:(p + 1) * 2 * BW], toep_ref[p]) + _dot(h.astype(BF16), wc_ref[p])
        for gg in range(2):
            g = 2 * p + gg
            yg = y[:, gg * BW:(gg + 1) * BW]
            ut_ref[g * BW:(g + 1) * BW, :] = yg[nc:, :].T
            if ctx_out:
                utc_ref[g * BW:(g + 1) * BW, :] = jnp.concatenate(
                    [yg[:nc, :], jnp.zeros((cpad - nc, BW), F32)], axis=0).T

    for t in range(S5_T):
        z = jnp.concatenate([ut_ref[grp(g, t), :] for g in range(ng)], axis=0)
        y_ref[0, pl.ds(t, nj, stride=S5_T), :] = z.T
        if ctx_out:
            zc = jnp.concatenate([utc_ref[grp(g, t), :] for g in range(ng)], axis=0)
            yc_ref[0, pl.ds(t, nc, stride=S5_T), :] = zc.T[:nc]


def _s5_matrices(a_re, a_im, log_dt, b_re, b_im, c_re, c_im, d):
    g, p, c, t = S5_GROUPS, S5_STATE, S5_GROUP, S5_T
    tc = t * c
    lam_re = jnp.minimum(a_re.astype(F32), -1e-4)
    lam_im = a_im.astype(F32)
    dt = jnp.exp(log_dt.astype(F32))[..., None]
    lr, li = lam_re * dt, lam_im * dt

    def a_pow(n, x_re, x_im):
        mag = jnp.exp(n * x_re)
        return mag * jnp.cos(n * x_im), mag * jnp.sin(n * x_im)

    ab_re, ab_im = a_pow(1.0, lr, li)
    den = lam_re * lam_re + lam_im * lam_im
    k_re = ((ab_re - 1.0) * lam_re + ab_im * lam_im) / den
    k_im = (ab_im * lam_re - (ab_re - 1.0) * lam_im) / den
    br, bi = b_re.astype(F32), b_im.astype(F32)
    bb_re = k_re[..., None] * br - k_im[..., None] * bi
    bb_im = k_re[..., None] * bi + k_im[..., None] * br
    bbt_re, bbt_im = jnp.swapaxes(bb_re, 2, 3), jnp.swapaxes(bb_im, 2, 3)
    ct_re = jnp.swapaxes(c_re.astype(F32), 2, 3)
    ct_im = jnp.swapaxes(c_im.astype(F32), 2, 3)

    n_up = np.repeat(np.arange(t + 1, dtype=np.float32), c)
    ca = []
    for dirn, n_lane in ((0, n_up), (1, t - n_up)):
        pr, pi = a_pow(n_lane[None, None, :], lr[dirn][:, :, None], li[dirn][:, :, None])
        cr_l = jnp.tile(ct_re[dirn], (1, 1, t + 1))
        ci_l = jnp.tile(ct_im[dirn], (1, 1, t + 1))
        ca.append(jnp.concatenate([cr_l * pr - ci_l * pi, -(cr_l * pi + ci_l * pr)], axis=1))
    wc_f, wc_r = ca[0][:, :, c:], ca[1][:, :, :tc]
    bb_cat = jnp.concatenate([bbt_re, bbt_im], axis=-1)
    lag_f = jnp.einsum('gik,gkl->gil', bb_cat[0], ca[0][:, :, :tc], precision=HIGHEST)
    lag_r = jnp.einsum('gik,gkl->gil', bb_cat[1], ca[1][:, :, c:], precision=HIGHEST)
    zeros = jnp.zeros_like(lag_f)
    ext_f = jnp.concatenate([zeros, lag_f], axis=-1)
    ext_r = jnp.concatenate([lag_r, zeros], axis=-1)
    toep = jnp.stack([ext_f[:, :, tc - c * s:2 * tc - c * s] + ext_r[:, :, c * (t - 1 - s):c * (t - 1 - s) + tc]
                      for s in range(t)], axis=1)
    skip = jnp.eye(tc, dtype=F32)[None] * jnp.tile(d.astype(F32).reshape(g, 1, c), (1, 1, t))
    toep = toep.reshape(g, tc, tc) + skip

    n_row = np.repeat(np.arange(t, dtype=np.float32), c)[None, :, None]
    ends = []
    for dirn, n in ((0, t - 1 - n_row), (1, n_row)):
        pr, pi = a_pow(n, lr[dirn][:, None, :], li[dirn][:, None, :])
        b_r, b_i = jnp.tile(bbt_re[dirn], (1, t, 1)), jnp.tile(bbt_im[dirn], (1, t, 1))
        ends += [pr * b_r - pi * b_i, pr * b_i + pi * b_r]
    wend = jnp.concatenate(ends, axis=-1)

    half = g // 2
    z2 = jnp.zeros((half, tc, tc), F32)
    toep_p = jnp.concatenate([jnp.concatenate([toep[0::2], z2], axis=-1),
                              jnp.concatenate([z2, toep[1::2]], axis=-1)], axis=1)
    w4 = wend.reshape(g, tc, 4, p)
    z4 = jnp.zeros((half, tc, 4, p), F32)
    wend_p = jnp.concatenate([jnp.stack([w4[0::2], z4], axis=3).reshape(half, tc, 8 * p),
                              jnp.stack([z4, w4[1::2]], axis=3).reshape(half, tc, 8 * p)], axis=1)
    wc4 = jnp.concatenate([wc_f, wc_r], axis=1).reshape(g, 4, p, tc)
    zc = jnp.zeros((half, 4, p, tc), F32)
    wc_p = jnp.stack([jnp.concatenate([wc4[0::2], zc], axis=-1),
                      jnp.concatenate([zc, wc4[1::2]], axis=-1)], axis=2).reshape(half, 8 * p, 2 * tc)
    a16 = jnp.stack([v for dirn in (0, 1) for v in a_pow(float(t), lr[dirn], li[dirn])], axis=0)
    return toep_p.astype(BF16), wend_p.astype(BF16), wc_p.astype(BF16), a16.reshape(4, g * p)


def _s5_mix(su_c, su_x, mats, ctx_out):
    toep_p, wend_p, wc_p, a16 = mats
    b, l, _ = su_c.shape
    s = su_x.shape[1]
    n_rows = (l + s) // S5_T
    halves = 2
    lanes = BW // halves
    ng = lanes // S5_GROUP
    pw = 2 * BW
    assert (s // S5_T) % 128 == 0 and (l // S5_T) % 8 == 0 and l // S5_T <= 128
    wspec = pl.BlockSpec((ng // 2, pw, pw), lambda bi, h: (h, 0, 0))
    out_specs = [pl.BlockSpec((1, s, lanes), lambda bi, h: (bi, 0, h))]
    out_shape = [jax.ShapeDtypeStruct((b, s, BW), F32)]
    if ctx_out:
        out_specs.append(pl.BlockSpec((1, l, lanes), lambda bi, h: (bi, 0, h)))
        out_shape.append(jax.ShapeDtypeStruct((b, l, BW), F32))
    res = pl.pallas_call(
        functools.partial(_s5_kernel, ctx_out=ctx_out),
        grid=(b, halves),
        in_specs=[pl.BlockSpec((1, s, lanes), lambda bi, h: (bi, 0, h)),
                  pl.BlockSpec((1, l, lanes), lambda bi, h: (bi, 0, h)),
                  wspec, wspec, wspec,
                  pl.BlockSpec((4, ng * S5_STATE), lambda bi, h: (0, h))],
        out_specs=out_specs,
        out_shape=out_shape,
        scratch_shapes=[pltpu.VMEM((ng * BW, s // S5_T), F32), pltpu.VMEM((ng * BW, 128), F32),
                        pltpu.VMEM((n_rows, ng * BW), BF16), pltpu.VMEM((n_rows, 4 * ng * S5_STATE), F32)],
        compiler_params=_cparams("arbitrary", "arbitrary"),
        name="s5_mixer",
    )(su_x, su_c, wend_p, toep_p, wc_p, a16)
    return (res[1] if ctx_out else None), res[0]


def _merge_kernel(zg_ref, oa_ref, ob_ref, ys_ref, od_ref, x_ref, mod_ref, g_ref, wb_ref, wo_ref, gw_ref, gb_ref,
                  o_ref):
    d = x_ref.shape[1]
    y = jax.nn.gelu(ys_ref[...].astype(F32))
    oc = (y * _sigmoid(_dot(y.astype(BF16), gw_ref[...]) + gb_ref[...])).astype(BF16)
    outs = (oa_ref[...], ob_ref[...], oc, od_ref[...])
    m = None
    for i, o in enumerate(outs):
        term = _sigmoid(zg_ref[:, i * d:(i + 1) * d].astype(F32)) * _dot(o, wb_ref[i])
        m = term if m is None else m + term
    mo = _dot(m.astype(BF16), wo_ref[...])
    o_ref[...] = x_ref[...] + mod_ref[0, 2:3, :] * _rms(mo, g_ref[...])


def _merge(z2, o_a, o_b, y_s, o_d, x2, mod, g_post, wb, wo, glu_w, glu_b, tiles_per_mod, tm):
    m, d = x2.shape
    row = lambda i: (i, 0)
    return pl.pallas_call(
        _merge_kernel,
        grid=(m // tm,),
        in_specs=[pl.BlockSpec((tm, Z_GATES_W), row)] + [pl.BlockSpec((tm, BW), row)] * 4
        + [pl.BlockSpec((tm, d), row),
           pl.BlockSpec((1, 8, d), lambda i: (i // tiles_per_mod, 0, 0)),
           _const_spec((1, d)), _const_spec(wb.shape), _const_spec(wo.shape),
           _const_spec(glu_w.shape), _const_spec((1, BW))],
        out_specs=pl.BlockSpec((tm, d), row),
        out_shape=jax.ShapeDtypeStruct((m, d), F32),
        compiler_params=_cparams("arbitrary"),
        name="merge",
    )(z2, o_a, o_b, y_s, o_d, x2, mod, g_post.reshape(1, d), wb, wo, glu_w, glu_b.reshape(1, BW).astype(F32))


def _ffn_dense_kernel(x_ref, mod_ref, gpre_ref, gpost_ref, w1_ref, w3_ref, w2_ref, o_ref):
    x = x_ref[...]
    h = (_rms(x, gpre_ref[...]) * (1.0 + mod_ref[0, 4:5, :]) + mod_ref[0, 3:4, :]).astype(BF16)
    a = _dot(h, w1_ref[...])
    a = (a * _sigmoid(a)) * _dot(h, w3_ref[...])
    f = _dot(a.astype(BF16), w2_ref[...])
    o_ref[...] = x + mod_ref[0, 5:6, :] * _rms(f, gpost_ref[...])


def _ffn_dense(x2, mod, g_pre, g_post, w1, w3, w2, tiles_per_mod, tm):
    m, d = x2.shape
    return pl.pallas_call(
        _ffn_dense_kernel,
        grid=(m // tm,),
        in_specs=[pl.BlockSpec((tm, d), lambda i: (i, 0)),
                  pl.BlockSpec((1, 8, d), lambda i: (i // tiles_per_mod, 0, 0)),
                  _const_spec((1, d)), _const_spec((1, d)),
                  _const_spec(w1.shape), _const_spec(w3.shape), _const_spec(w2.shape)],
        out_specs=pl.BlockSpec((tm, d), lambda i: (i, 0)),
        out_shape=jax.ShapeDtypeStruct((m, d), F32),
        compiler_params=_cparams("arbitrary"),
        name="ffn_dense",
    )(x2, mod, g_pre.reshape(1, d), g_post.reshape(1, d), w1, w3, w2)


def _route_kernel(x_ref, mod_ref, gpre_ref, r_ref, h_ref, idx_ref, wt_ref):
    h = _rms(x_ref[...], gpre_ref[...]) * (1.0 + mod_ref[0, 4:5, :]) + mod_ref[0, 3:4, :]
    h_hi = h.astype(BF16)
    h_ref[...] = h_hi
    h_lo = (h - h_hi.astype(F32)).astype(BF16)
    a = _dot(h_hi, r_ref[...])
    logits = a + pltpu.roll(a, 128 - N_EXPERTS, 1) + _dot(h_lo, r_ref[...])
    lane = lax.broadcasted_iota(jnp.int32, logits.shape, 1)
    logits = jnp.where(lane < N_EXPERTS, logits, 2.0 * NEG_INF)
    m0 = logits.max(axis=-1, keepdims=True)
    i0 = jnp.where(logits == m0, lane, 128).min(axis=-1, keepdims=True)
    rest = jnp.where(lane == i0, NEG_INF, logits)
    m1 = rest.max(axis=-1, keepdims=True)
    i1 = jnp.where(rest == m1, lane, 128).min(axis=-1, keepdims=True)
    e = jnp.exp(m1 - m0)
    w0 = 1.0 / (1.0 + e)
    idx_ref[...] = jnp.where(lane == 0, i0, jnp.where(lane == 1, i1, 0))
    wt_ref[...] = jnp.where(lane == 0, w0, jnp.where(lane == 1, e * w0, 0.0))


def _route(x2, mod, g_pre, router, tiles_per_mod, tm):
    m, d = x2.shape
    r_hi = router.astype(BF16)
    r_lo = (router.astype(F32) - r_hi.astype(F32)).astype(BF16)
    r_pad = jnp.pad(jnp.concatenate([r_hi, r_lo], axis=1), ((0, 0), (0, 128 - 2 * N_EXPERTS)))
    row = lambda i: (i, 0)
    return pl.pallas_call(
        _route_kernel,
        grid=(m // tm,),
        in_specs=[pl.BlockSpec((tm, d), row),
                  pl.BlockSpec((1, 8, d), lambda i: (i // tiles_per_mod, 0, 0)),
                  _const_spec((1, d)), _const_spec((d, 128))],
        out_specs=[pl.BlockSpec((tm, d), row), pl.BlockSpec((tm, 128), row), pl.BlockSpec((tm, 128), row)],
        out_shape=[jax.ShapeDtypeStruct((m, d), BF16), jax.ShapeDtypeStruct((m, 128), jnp.int32),
                   jax.ShapeDtypeStruct((m, 128), F32)],
        compiler_params=_cparams("arbitrary"),
        name="moe_route",
    )(x2, mod, g_pre.reshape(1, d), r_pad)


def _ffn_grouped_kernel(te_ref, nv_ref, h_ref, w1_ref, w3_ref, w2_ref, o_ref, acc_ref):
    i, j = pl.program_id(0), pl.program_id(1)
    last = pl.num_programs(1) - 1
    valid = i < nv_ref[0]

    @pl.when(j == 0)
    def _():
        acc_ref[...] = jnp.zeros_like(acc_ref)

    @pl.when(valid)
    def _():
        w1, w3, w2 = w1_ref[0].astype(BF16), w3_ref[0].astype(BF16), w2_ref[0].astype(BF16)
        half = h_ref.shape[0] // 2
        for r in range(2):
            rows = slice(r * half, (r + 1) * half)
            h = h_ref[rows, :]
            a = _dot(h, w1)
            a = (a * _sigmoid(a)) * _dot(h, w3)
            acc_ref[rows, :] += _dot(a.astype(BF16), w2)

    @pl.when(j == last)
    def _():
        o_ref[...] = acc_ref[...].astype(BF16)


def _ffn_grouped(h_sorted, tile_expert, n_valid, w1, w3, w2, tm, fc):
    r, d = h_sorted.shape
    f = w1.shape[2]
    nj = f // fc

    def col(i, j, nv):
        return jnp.where(i < nv[0], j, nj - 1)

    grid_spec = pltpu.PrefetchScalarGridSpec(
        num_scalar_prefetch=2,
        grid=(r // tm, nj),
        in_specs=[pl.BlockSpec((tm, d), lambda i, j, te, nv: (i, 0)),
                  pl.BlockSpec((1, d, fc), lambda i, j, te, nv: (te[i], 0, col(i, j, nv))),
                  pl.BlockSpec((1, d, fc), lambda i, j, te, nv: (te[i], 0, col(i, j, nv))),
                  pl.BlockSpec((1, fc, d), lambda i, j, te, nv: (te[i], col(i, j, nv), 0))],
        out_specs=pl.BlockSpec((tm, d), lambda i, j, te, nv: (i, 0)),
        scratch_shapes=[pltpu.VMEM((tm, d), F32)],
    )
    return pl.pallas_call(
        _ffn_grouped_kernel,
        grid_spec=grid_spec,
        out_shape=jax.ShapeDtypeStruct((r, d), BF16),
        compiler_params=_cparams("arbitrary", "arbitrary"),
        name="ffn_grouped",
    )(tile_expert, n_valid, h_sorted, w1, w3, w2)


def _combine_kernel(o0_ref, o1_ref, wt_ref, x_ref, mod_ref, gpost_ref, o_ref):
    y = wt_ref[:, 0:1] * o0_ref[...].astype(F32) + wt_ref[:, 1:2] * o1_ref[...].astype(F32)
    o_ref[...] = x_ref[...] + mod_ref[0, 5:6, :] * _rms(y, gpost_ref[...])


def _combine(o0, o1, wts, x2, mod, g_post, tiles_per_mod, tm):
    m, d = x2.shape
    row = lambda i: (i, 0)
    return pl.pallas_call(
        _combine_kernel,
        grid=(m // tm,),
        in_specs=[pl.BlockSpec((tm, d), row), pl.BlockSpec((tm, d), row), pl.BlockSpec((tm, 128), row),
                  pl.BlockSpec((tm, d), row),
                  pl.BlockSpec((1, 8, d), lambda i: (i // tiles_per_mod, 0, 0)),
                  _const_spec((1, d))],
        out_specs=pl.BlockSpec((tm, d), row),
        out_shape=jax.ShapeDtypeStruct((m, d), F32),
        compiler_params=_cparams("arbitrary"),
        name="moe_combine",
    )(o0, o1, wts, x2, mod, g_post.reshape(1, d))


def _moe(x2, mod, g_pre, g_post, router, w1, w3, w2, tiles_per_mod, tm):
    m, d = x2.shape
    h, idx, wts = _route(x2, mod, g_pre, router, tiles_per_mod, tm)
    gt = MOE_GROUP_TILE
    e_flat = idx[:, :2].reshape(-1)
    onehot = (e_flat[:, None] == jnp.arange(N_EXPERTS)[None, :]).astype(jnp.int32)
    rank = jnp.take_along_axis(jnp.cumsum(onehot, axis=0), e_flat[:, None], axis=1)[:, 0] - 1
    counts = onehot.sum(axis=0)
    padded = (counts + gt - 1) // gt * gt
    ends = jnp.cumsum(padded)
    pos = (ends - padded)[e_flat] + rank
    n_tiles = (2 * m) // gt + N_EXPERTS
    src = jnp.zeros((n_tiles * gt,), jnp.int32).at[pos].set(jnp.arange(2 * m, dtype=jnp.int32) // 2,
                                                            unique_indices=True, mode='promise_in_bounds')
    tile_start = jnp.arange(n_tiles, dtype=jnp.int32) * gt
    tile_expert = jnp.minimum((ends[None, :] <= tile_start[:, None]).sum(axis=1), N_EXPERTS - 1).astype(jnp.int32)
    n_valid = (ends[-1] // gt).astype(jnp.int32).reshape(1)
    rows = lambda a, i: a.at[i].get(mode='promise_in_bounds')
    out = _ffn_grouped(rows(h, src), tile_expert, n_valid, w1, w3, w2, gt, MOE_F_CHUNK)
    pos2 = pos.reshape(m, 2)
    return _combine(rows(out, pos2[:, 0]), rows(out, pos2[:, 1]), wts, x2, mod, g_post, tiles_per_mod, tm)


def _z_weights(w_in):
    d = w_in.shape[0]
    hw = N_HEADS * HEAD_DIM
    kvw = SWA_KV_HEADS * HEAD_DIM
    sizes = (hw, hw, hw, hw, kvw, kvw, BW, BW, BW, Z_GATES_W)
    qa, ka, va, qd, kd, vd, gu, gv, su, gates = jnp.split(w_in, np.cumsum(sizes)[:-1].tolist(), axis=1)
    rep = N_HEADS // SWA_KV_HEADS
    dup = lambda w: jnp.repeat(w.reshape(d, SWA_KV_HEADS, 1, HEAD_DIM), rep, axis=2).reshape(d, hw)
    return jnp.concatenate([gates, qa, ka, va, qd, dup(kd), dup(vd), gu, gv, su], axis=1).astype(BF16)


def kernel(x, c, ctx, c_ctx, w_mod, b_mod, g_pre_mix, g_post_mix, g_pre_ffn, g_post_ffn, w_in, na_rpb, swa_sink, gmlp_ln_g, gmlp_ln_b, gmlp_ws, gmlp_bs, s5_a_re, s5_a_im, s5_log_dt, s5_b_re, s5_b_im, s5_c_re, s5_c_im, s5_d, s5_glu_w, s5_glu_b, w_branch, w_out, ffn_w1, ffn_w3, ffn_w2, moe_router, moe_w1, moe_w3, moe_w2):
    bsz, s, d = x.shape
    l = ctx.shape[1]
    depth = w_mod.shape[0]
    tm = 512
    rows = s // GRID_W

    c_all = jnp.zeros((8, d), F32).at[:bsz].set(c).at[bsz].set(c_ctx)
    mods = _modulation(c_all, w_mod, b_mod).reshape(depth, 8, 6, d)
    mods = jnp.pad(mods, ((0, 0), (0, 0), (0, 2), (0, 0)))
    rope_tabs = _rope_tables(s)
    swa_bias = _swa_bias(s)

    x2 = x.reshape(bsz * s, d)
    c2 = ctx.reshape(bsz * l, d)
    for i in range(depth):
        ctx_out = i < depth - 1
        mod_x = mods[i, :bsz]
        mod_c = mods[i, bsz:bsz + 1]
        w_z = _z_weights(w_in[i])
        zx, su_x = _project(x2, mod_x, g_pre_mix[i], w_z, s // tm, rope_tabs, tm)
        zc, su_c = _project(c2, mod_c, g_pre_mix[i], w_z, (bsz * l) // tm, None, tm)
        zx3 = zx.reshape(bsz, s, ZW)
        zc3 = zc.reshape(bsz, l, ZW)
        sink_rows = jnp.repeat(swa_sink[i].astype(F32) * LOG2E, ATT_TQ).reshape(1, N_HEADS * ATT_TQ)

        nq = s // ATT_TQ
        o_a = _local_attention(zx3, zc3, Z_QA, Z_KA, Z_VA, _na_bias(na_rpb[i], rows), None, ATT_TQ,
                               lambda qi, j: jnp.clip(qi - 1, 0, nq - 3) + j)
        o_d = _local_attention(zx3, zc3, Z_QD, Z_KD, Z_VD, swa_bias, sink_rows, SWA_WINDOW,
                               functools.partial(_swa_key_index, n_blocks=s // SWA_WINDOW))
        o_b = _gmlp(zx, gmlp_ln_g[i], gmlp_ln_b[i], gmlp_ws[i], gmlp_bs[i], tm)
        s5_mats = _s5_matrices(s5_a_re[i], s5_a_im[i], s5_log_dt[i], s5_b_re[i], s5_b_im[i],
                               s5_c_re[i], s5_c_im[i], s5_d[i])
        y_c, y_x = _s5_mix(su_c.reshape(bsz, l, BW), su_x.reshape(bsz, s, BW), s5_mats, ctx_out)

        wb = w_branch[i].astype(BF16)
        wo = w_out[i].astype(BF16)
        glu_w = s5_glu_w[i].astype(BF16)
        x2 = _merge(zx, o_a.reshape(bsz * s, BW), o_b, y_x.reshape(bsz * s, BW), o_d.reshape(bsz * s, BW),
                    x2, mod_x, g_post_mix[i], wb, wo, glu_w, s5_glu_b[i], s // tm, tm)
        if ctx_out:
            o_a_c = _ctx_attention(zc3, Z_QA, Z_KA, Z_VA, None)
            o_d_c = _ctx_attention(zc3, Z_QD, Z_KD, Z_VD, sink_rows)
            o_b_c = _gmlp(zc, gmlp_ln_g[i], gmlp_ln_b[i], gmlp_ws[i], gmlp_bs[i], tm)
            c2 = _merge(zc, o_a_c.reshape(bsz * l, BW), o_b_c, y_c.reshape(bsz * l, BW), o_d_c.reshape(bsz * l, BW),
                        c2, mod_c, g_post_mix[i], wb, wo, glu_w, s5_glu_b[i], (bsz * l) // tm, tm)

        j = i // 2
        if i % 2 == 0:
            w1, w3, w2 = ffn_w1[j].astype(BF16), ffn_w3[j].astype(BF16), ffn_w2[j].astype(BF16)
            x2 = _ffn_dense(x2, mod_x, g_pre_ffn[i], g_post_ffn[i], w1, w3, w2, s // tm, tm)
            if ctx_out:
                c2 = _ffn_dense(c2, mod_c, g_pre_ffn[i], g_post_ffn[i], w1, w3, w2, (bsz * l) // tm, tm)
        else:
            w1, w3, w2 = moe_w1[j], moe_w3[j], moe_w2[j]
            x2 = _moe(x2, mod_x, g_pre_ffn[i], g_post_ffn[i], moe_router[j], w1, w3, w2, s // tm, tm)
            if ctx_out:
                c2 = _moe(c2, mod_c, g_pre_ffn[i], g_post_ffn[i], moe_router[j], w1, w3, w2, (bsz * l) // tm, tm)
    return x2.reshape(bsz, s, d)
```

```python
import functools
import math

import numpy as np
import jax
import jax.numpy as jnp
from jax import lax
from jax.experimental import pallas as pl
from jax.experimental.pallas import tpu as pltpu

F32 = jnp.float32
BF16 = jnp.bfloat16
HIGHEST = lax.Precision.HIGHEST

GRID_W = 64
HEAD_DIM = 64
N_HEADS = 4
NA_ROWS = 8
NA_COLS = 16
SWA_KV_HEADS = 2
SWA_WINDOW = 128
GMLP_GROUPS = 4
GMLP_CHUNK = 128
S5_GROUP = 16
S5_GROUPS = 16
S5_STATE = 64
N_EXPERTS = 8
ROPE_BASE = 10000.0
EPS = 1e-6
NEG_INF = -1e30
LOG2E = math.log2(math.e)
Q_SCALE = HEAD_DIM ** -0.5 * LOG2E

BW = 256
S5_T = 16
S5_ROWS = 8
Z_GATES_W = 4096
(Z_QA, Z_KA, Z_VA, Z_QD, Z_KD, Z_VD, Z_GU, Z_GV, Z_SU) = range(Z_GATES_W // BW, Z_GATES_W // BW + 9)
ZW = Z_GATES_W + 9 * BW

V7X_VMEM_LIMIT = 56 * 1024 * 1024
ATT_TQ = 256
ONES_ROWS = 16
ATT_CHAIN_HEADS = 2
SWA_LOCAL_BLOCKS = ATT_TQ // SWA_WINDOW + 2
MOE_GROUP_TILE = 1024
MOE_F_CHUNK = 512
MOE_DISPATCH_CHUNKS = 4


def _cparams(*sem):
    return pltpu.CompilerParams(dimension_semantics=sem, vmem_limit_bytes=V7X_VMEM_LIMIT)


def _const_spec(shape):
    nd = len(shape)
    return pl.BlockSpec(shape, lambda *_: (0,) * nd, pipeline_mode=pl.Buffered(1))


def _dot(a, b):
    return jnp.dot(a, b, preferred_element_type=F32)


def _dot_nt(a, b):
    return lax.dot_general(a, b, (((1,), (1,)), ((), ())), preferred_element_type=F32)


def _sigmoid(x):
    return 0.5 * jnp.tanh(0.5 * x) + 0.5


def _rms(x, g):
    return x * lax.rsqrt(jnp.mean(x * x, axis=-1, keepdims=True) + EPS) * g


def _mod_kernel(c_ref, w_ref, b_ref, o_ref):
    c = c_ref[...]
    a = c * _sigmoid(c)
    o_ref[0] = jnp.dot(a, w_ref[0], preferred_element_type=F32, precision=HIGHEST) + b_ref[0]


def _modulation(c_all, w_mod, b_mod):
    depth, d, n = w_mod.shape
    tn = 1536
    return pl.pallas_call(
        _mod_kernel,
        grid=(depth, n // tn),
        in_specs=[pl.BlockSpec((8, d), lambda l, j: (0, 0)),
                  pl.BlockSpec((1, d, tn), lambda l, j: (l, 0, j)),
                  pl.BlockSpec((1, 1, tn), lambda l, j: (l, 0, j))],
        out_specs=pl.BlockSpec((1, 8, tn), lambda l, j: (l, 0, j)),
        out_shape=jax.ShapeDtypeStruct((depth, 8, n), F32),
        compiler_params=_cparams("arbitrary", "arbitrary"),
        name="modulation",
    )(c_all, w_mod, b_mod.reshape(depth, 1, n))


def _proj_kernel(*refs, rope, n_chunk):
    if rope:
        x_ref, mod_ref, g_ref, w_ref, cos_ref, sin_ref, o_ref, su_ref = refs
    else:
        x_ref, mod_ref, g_ref, w_ref, o_ref, su_ref = refs
    h = _rms(x_ref[...], g_ref[...]) * (1.0 + mod_ref[0, 1:2, :]) + mod_ref[0, 0:1, :]
    h = h.astype(BF16)
    rope_blocks = (Z_QD, Z_KD) if rope else ()
    for j in range(ZW // n_chunk):
        lo = j * n_chunk
        r = _dot(h, w_ref[:, lo:lo + n_chunk])
        blocks = range(lo // BW, (lo + n_chunk) // BW)
        if Z_SU in blocks:
            su_ref[...] = r[:, Z_SU * BW - lo:(Z_SU + 1) * BW - lo]
        if not any(b in rope_blocks or b in (Z_QA, Z_QD) for b in blocks):
            o_ref[:, lo:lo + n_chunk] = r.astype(BF16)
            continue
        for b in blocks:
            t = r[:, b * BW - lo:(b + 1) * BW - lo]
            if b in rope_blocks:
                lane = lax.broadcasted_iota(jnp.int32, (1, BW), 1)
                low_half = (lane % (HEAD_DIM // 2)) < (HEAD_DIM // 4)
                partner = jnp.where(low_half, pltpu.roll(t, BW - HEAD_DIM // 4, 1),
                                    pltpu.roll(t, HEAD_DIM // 4, 1))
                t = t * cos_ref[...] + partner * sin_ref[...]
            if b in (Z_QA, Z_QD):
                t = t * Q_SCALE
            o_ref[:, b * BW:(b + 1) * BW] = t.astype(BF16)


def _project(x2, mod, g, w_z, tiles_per_mod, rope_tabs, tm):
    m, d = x2.shape
    rope = rope_tabs is not None
    in_specs = [pl.BlockSpec((tm, d), lambda i: (i, 0)),
                pl.BlockSpec((1, 8, d), lambda i: (i // tiles_per_mod, 0, 0)),
                _const_spec((1, d)),
                _const_spec((d, ZW))]
    args = [x2, mod, g.reshape(1, d), w_z]
    if rope:
        n_rt = rope_tabs[0].shape[0] // tm
        in_specs += [pl.BlockSpec((tm, BW), lambda i: (i % n_rt, 0))] * 2
        args += list(rope_tabs)
    return pl.pallas_call(
        functools.partial(_proj_kernel, rope=rope, n_chunk=1280),
        grid=(m // tm,),
        in_specs=in_specs,
        out_specs=[pl.BlockSpec((tm, ZW), lambda i: (i, 0)), pl.BlockSpec((tm, BW), lambda i: (i, 0))],
        out_shape=[jax.ShapeDtypeStruct((m, ZW), BF16), jax.ShapeDtypeStruct((m, BW), F32)],
        compiler_params=_cparams("arbitrary"),
        name="project_in",
    )(*args)


def _attn_kernel(*refs, n_local, use_sink):
    q_ref = refs[0]
    kv = refs[1:1 + 2 * n_local]
    kc_ref, vc_ref = refs[1 + 2 * n_local], refs[2 + 2 * n_local]
    pos = 3 + 2 * n_local
    bias_ref = sink_ref = None
    if n_local:
        bias_ref = refs[pos]
        pos += 1
    if use_sink:
        sink_ref = refs[pos]
        pos += 1
    o_ref = refs[pos]

    q = q_ref[0]
    tq = q.shape[0]
    lane_head = lax.broadcasted_iota(jnp.int32, (1, BW), 1) // HEAD_DIM
    k_refs = [kv[2 * j] for j in range(n_local)] + [kc_ref]
    v_t = [r[0].T for r in [kv[2 * j + 1] for j in range(n_local)] + [vc_ref]]
    chains = [range(h0, h0 + ATT_CHAIN_HEADS) for h0 in range(0, N_HEADS, ATT_CHAIN_HEADS)]

    def score_stage(heads):
        cols = slice(heads[0] * tq, (heads[-1] + 1) * tq)
        q_heads = jnp.concatenate([jnp.where(lane_head == h, q, jnp.zeros_like(q)) for h in heads], axis=0)
        scores = []
        for j, k_ref in enumerate(k_refs):
            s = _dot_nt(k_ref[0], q_heads)
            if j < n_local:
                kb = k_ref.shape[1]
                s = s + bias_ref[0, j * kb:(j + 1) * kb, cols]
            scores.append(s)
        return scores

    def softmax_value_stage(heads, scores):
        cols = slice(heads[0] * tq, (heads[-1] + 1) * tq)
        mx = scores[0].max(axis=0, keepdims=True)
        for s in scores[1:]:
            mx = jnp.maximum(mx, s.max(axis=0, keepdims=True))
        if use_sink:
            mx = jnp.maximum(mx, sink_ref[:, cols])
        o = [jnp.zeros((HEAD_DIM + ONES_ROWS, tq), F32) for _ in heads]
        for s, vt in zip(scores, v_t):
            p = jnp.exp2(s - mx).astype(BF16)
            ones = jnp.ones((ONES_ROWS, vt.shape[1]), BF16)
            for i, h in enumerate(heads):
                lhs = jnp.concatenate([vt[h * HEAD_DIM:(h + 1) * HEAD_DIM, :], ones], axis=0)
                o[i] = o[i] + _dot(lhs, p[:, i * tq:(i + 1) * tq])
        res = []
        for i, h in enumerate(heads):
            den = o[i][HEAD_DIM:HEAD_DIM + 1, :]
            if use_sink:
                den = den + jnp.exp2(sink_ref[:, h * tq:(h + 1) * tq] - mx[:, i * tq:(i + 1) * tq])
            res.append(o[i][:HEAD_DIM, :] / den)
        return res

    outs = []
    pending = score_stage(chains[0])
    for c, heads in enumerate(chains):
        scores = pending
        if c + 1 < len(chains):
            pending = score_stage(chains[c + 1])
        outs += softmax_value_stage(heads, scores)
    o_ref[0] = jnp.concatenate(outs, axis=0).T.astype(BF16)


def _local_attention(z, z_c, q_col, k_col, v_col, bias, sink_rows, key_block, key_index):
    b, s, _ = z.shape
    l = z_c.shape[1]
    tq = ATT_TQ
    nq = s // tq
    n_local = bias.shape[1] // key_block
    assert nq >= 3 and l == BW

    def kv_spec(col, j):
        return pl.BlockSpec((1, key_block, BW), lambda bi, i: (bi, key_index(i, j), col))

    def pat(i):
        return jnp.where(i == 0, 0, jnp.where(i == nq - 1, 2, 1))

    in_specs = [pl.BlockSpec((1, tq, BW), lambda bi, i: (bi, i, q_col))]
    args = [z]
    for j in range(n_local):
        in_specs += [kv_spec(k_col, j), kv_spec(v_col, j)]
        args += [z, z]
    in_specs += [pl.BlockSpec((1, l, BW), lambda bi, i: (bi, 0, k_col)),
                 pl.BlockSpec((1, l, BW), lambda bi, i: (bi, 0, v_col)),
                 pl.BlockSpec((1,) + bias.shape[1:], lambda bi, i: (pat(i), 0, 0))]
    args += [z_c, z_c, bias]
    if sink_rows is not None:
        in_specs.append(_const_spec(sink_rows.shape))
        args.append(sink_rows)
    return pl.pallas_call(
        functools.partial(_attn_kernel, n_local=n_local, use_sink=sink_rows is not None),
        grid=(b, nq),
        in_specs=in_specs,
        out_specs=pl.BlockSpec((1, tq, BW), lambda bi, i: (bi, i, 0)),
        out_shape=jax.ShapeDtypeStruct((b, s, BW), BF16),
        compiler_params=_cparams("arbitrary", "arbitrary"),
        name="local_attention",
    )(*args)


def _ctx_attention(z_c, q_col, k_col, v_col, sink_rows):
    b, l, _ = z_c.shape
    in_specs = [pl.BlockSpec((1, l, BW), lambda bi, c=col: (bi, 0, c)) for col in (q_col, k_col, v_col)]
    args = [z_c, z_c, z_c]
    if sink_rows is not None:
        in_specs.append(_const_spec(sink_rows.shape))
        args.append(sink_rows)
    return pl.pallas_call(
        functools.partial(_attn_kernel, n_local=0, use_sink=sink_rows is not None),
        grid=(b,),
        in_specs=in_specs,
        out_specs=pl.BlockSpec((1, l, BW), lambda bi: (bi, 0, 0)),
        out_shape=jax.ShapeDtypeStruct((b, l, BW), BF16),
        compiler_params=_cparams("arbitrary"),
        name="ctx_attention",
    )(*args)


def _na_bias(rpb, rows):
    tile_rows = ATT_TQ // GRID_W
    nq = rows // tile_rows
    col = np.arange(GRID_W)
    cs = np.clip(col - NA_COLS // 2, 0, GRID_W - NA_COLS)[:, None]
    kc = col[None, :]
    sel_c = ((kc - col[:, None] + NA_COLS - 1)[None] == np.arange(2 * NA_COLS - 1)[:, None, None]) \
        & ((kc >= cs) & (kc < cs + NA_COLS))[None]
    sel_r = []
    for i in (0, 1, nq - 1):
        base = min(max(i - 1, 0), nq - 3)
        qr = (tile_rows * i + np.arange(tile_rows))[:, None]
        kr = (tile_rows * base + np.arange(3 * tile_rows))[None, :]
        rs = np.clip(qr - NA_ROWS // 2, 0, rows - NA_ROWS)
        sel_r.append(((kr - qr + NA_ROWS - 1)[None] == np.arange(2 * NA_ROWS - 1)[:, None, None])
                     & ((kr >= rs) & (kr < rs + NA_ROWS))[None])
    sel_r = np.stack(sel_r).astype(np.float32)
    sel_c = sel_c.astype(np.float32)
    t1 = jnp.einsum('paqk,hab->phqkb', sel_r, rpb.astype(F32), precision=HIGHEST)
    bias = jnp.einsum('phqkb,bcd->pkdhqc', t1, sel_c, precision=HIGHEST)
    valid = np.einsum('paqk,bcd->pkdqc', sel_r, sel_c) > 0.5
    mask = np.where(valid, 0.0, NEG_INF).astype(np.float32)[:, :, :, None]
    return (bias * LOG2E + mask).reshape(3, 3 * ATT_TQ, rpb.shape[0] * ATT_TQ)


def _swa_key_index(i, j, n_blocks):
    return jnp.clip(i * (ATT_TQ // SWA_WINDOW) - 1 + j, 0, n_blocks - 1)


def _swa_bias(s):
    nq = s // ATT_TQ
    tq = np.arange(ATT_TQ)[:, None]
    tk = np.arange(SWA_LOCAL_BLOCKS * SWA_WINDOW)[None, :]
    out = []
    for i in (0, 1, nq - 1):
        kpos = ATT_TQ * i - SWA_WINDOW + tk
        valid = (np.abs(kpos - (ATT_TQ * i + tq)) <= SWA_WINDOW) & (kpos >= 0) & (kpos < s)
        out.append(np.tile(np.where(valid, 0.0, NEG_INF).astype(np.float32).T, (1, N_HEADS)))
    return jnp.asarray(np.stack(out))


def _rope_tables(s):
    nq = HEAD_DIM // 4
    t = jnp.arange(s)
    inv = ROPE_BASE ** (-jnp.arange(nq, dtype=F32) / nq)
    ang_r = (t // GRID_W).astype(F32)[:, None] * inv[None, :]
    ang_c = (t % GRID_W).astype(F32)[:, None] * inv[None, :]
    cos = jnp.concatenate([jnp.cos(ang_r)] * 2 + [jnp.cos(ang_c)] * 2, axis=-1)
    sin = jnp.concatenate([-jnp.sin(ang_r), jnp.sin(ang_r), -jnp.sin(ang_c), jnp.sin(ang_c)], axis=-1)
    return jnp.tile(cos, (1, N_HEADS)), jnp.tile(sin, (1, N_HEADS))


def _gmlp_kernel(u_ref, v_ref, g_ref, b_ref, w_ref, bs_ref, o_ref):
    lane_grp = lax.broadcasted_iota(jnp.int32, (1, BW), 1) // (BW // GMLP_GROUPS)
    for c in range(u_ref.shape[0] // GMLP_CHUNK):
        rows = slice(c * GMLP_CHUNK, (c + 1) * GMLP_CHUNK)
        u = jax.nn.gelu(u_ref[rows, :].astype(F32))
        v = jax.nn.gelu(v_ref[rows, :].astype(F32))
        mu = jnp.mean(v, axis=-1, keepdims=True)
        vc = v - mu
        v = vc * lax.rsqrt(jnp.mean(vc * vc, axis=-1, keepdims=True) + EPS) * g_ref[...] + b_ref[...]
        v = v.astype(BF16)
        stack = jnp.concatenate([jnp.where(lane_grp == g, v, jnp.zeros_like(v)) for g in range(GMLP_GROUPS)], axis=0)
        sg = _dot(w_ref[...], stack) + bs_ref[...]
        o_ref[rows, :] = (u * sg).astype(BF16)


def _gmlp(z2, ln_g, ln_b, ws, bs, tm):
    m = z2.shape[0]
    w_cat = jnp.concatenate([ws[g] for g in range(GMLP_GROUPS)], axis=1).astype(BF16)
    bs_l = jnp.repeat(bs.T.astype(F32), BW // GMLP_GROUPS, axis=1)
    return pl.pallas_call(
        _gmlp_kernel,
        grid=(m // tm,),
        in_specs=[pl.BlockSpec((tm, BW), lambda i: (i, Z_GU)),
                  pl.BlockSpec((tm, BW), lambda i: (i, Z_GV)),
                  _const_spec((1, BW)), _const_spec((1, BW)),
                  _const_spec(w_cat.shape), _const_spec(bs_l.shape)],
        out_specs=pl.BlockSpec((tm, BW), lambda i: (i, 0)),
        out_shape=jax.ShapeDtypeStruct((m, BW), BF16),
        compiler_params=_cparams("arbitrary"),
        name="gmlp",
    )(z2, z2, ln_g.reshape(1, BW).astype(F32), ln_b.reshape(1, BW).astype(F32), w_cat, bs_l)


def _s5_kernel(su_ref, suc_ref, wend_ref, toep_ref, wc_ref, a16_ref, *rest, ctx_out):
    if ctx_out:
        y_ref, yc_ref, ut_ref, utc_ref, up_ref, st_ref = rest
    else:
        y_ref, ut_ref, utc_ref, up_ref, st_ref = rest
    nj, nc = su_ref.shape[1] // S5_T, suc_ref.shape[1] // S5_T
    lanes = su_ref.shape[2]
    ng = lanes // S5_GROUP
    hw = ng * S5_STATE
    cpad = utc_ref.shape[1]

    def grp(g, t):
        return slice(g * BW + t * S5_GROUP, g * BW + (t + 1) * S5_GROUP)

    for t in range(S5_T):
        at = su_ref[0, pl.ds(t, nj, stride=S5_T), :].T
        ct = jnp.concatenate([suc_ref[0, pl.ds(t, nc, stride=S5_T), :],
                              jnp.zeros((cpad - nc, lanes), F32)], axis=0).T
        for g in range(ng):
            ut_ref[grp(g, t), :] = at[g * S5_GROUP:(g + 1) * S5_GROUP, :]
            utc_ref[grp(g, t), :] = ct[g * S5_GROUP:(g + 1) * S5_GROUP, :]

    for p in range(ng // 2):
        blocks = []
        for g in (2 * p, 2 * p + 1):
            xt = ut_ref[g * BW:(g + 1) * BW, :].T
            ct = utc_ref[g * BW:(g + 1) * BW, :].T[:nc]
            blocks.append(jnp.concatenate([ct, xt], axis=0))
        up = jnp.concatenate(blocks, axis=1).astype(BF16)
        up_ref[:, p * 2 * BW:(p + 1) * 2 * BW] = up
        e = _dot(up, wend_ref[p])
        for k in range(4):
            st_ref[:, k * hw + p * 128:k * hw + (p + 1) * 128] = e[:, k * 128:(k + 1) * 128]

    n_tiles, n_ctiles = (nc + nj) // 8, nc // 8
    a = [a16_ref[k:k + 1, :] for k in range(4)]

    def tile_step(k, carry):
        fr, fi, rr, ri = carry
        kf = pl.multiple_of(k * 8, 8)
        kr = jnp.where(k < n_ctiles, n_ctiles - 1 - k, n_tiles - 1 - (k - n_ctiles))
        kr = pl.multiple_of(kr * 8, 8)
        ef_r, ef_i = st_ref[pl.ds(kf, 8), 0:hw], st_ref[pl.ds(kf, 8), hw:2 * hw]
        er_r, er_i = st_ref[pl.ds(kr, 8), 2 * hw:3 * hw], st_ref[pl.ds(kr, 8), 3 * hw:4 * hw]
        hf_r, hf_i, hr_r, hr_i = [], [], [None] * 8, [None] * 8
        for r in range(8):
            hf_r.append(fr)
            hf_i.append(fi)
            fr, fi = a[0] * fr - a[1] * fi + ef_r[r:r + 1], a[0] * fi + a[1] * fr + ef_i[r:r + 1]
            q = 7 - r
            hr_r[q], hr_i[q] = rr, ri
            rr, ri = a[2] * rr - a[3] * ri + er_r[q:q + 1], a[2] * ri + a[3] * rr + er_i[q:q + 1]
        st_ref[pl.ds(kf, 8), 0:hw] = jnp.concatenate(hf_r, axis=0)
        st_ref[pl.ds(kf, 8), hw:2 * hw] = jnp.concatenate(hf_i, axis=0)
        st_ref[pl.ds(kr, 8), 2 * hw:3 * hw] = jnp.concatenate(hr_r, axis=0)
        st_ref[pl.ds(kr, 8), 3 * hw:4 * hw] = jnp.concatenate(hr_i, axis=0)
        return fr, fi, rr, ri

    zero = jnp.zeros((1, hw), F32)
    lax.fori_loop(0, n_tiles, tile_step, (zero, zero, zero, zero))

    for p in range(ng // 2):
        h = jnp.concatenate([st_ref[:, k * hw + p * 128:k * hw + (p + 1) * 128] for k in range(4)], axis=1)
        y = _dot(up_ref[:, p * 2 * BW:(p + 1) * 2 * BW], toep_ref[p]) + _dot(h.astype(BF16), wc_ref[p])
        for gg in range(2):
            g = 2 * p + gg
            yg = y[:, gg * BW:(gg + 1) * BW]
            ut_ref[g * BW:(g + 1) * BW, :] = yg[nc:, :].T
            if ctx_out:
                utc_ref[g * BW:(g + 1) * BW, :] = jnp.concatenate(
                    [yg[:nc, :], jnp.zeros((cpad - nc, BW), F32)], axis=0).T

    for t in range(S5_T):
        z = jnp.concatenate([ut_ref[grp(g, t), :] for g in range(ng)], axis=0)
        y_ref[0, pl.ds(t, nj, stride=S5_T), :] = z.T
        if ctx_out:
            zc = jnp.concatenate([utc_ref[grp(g, t), :] for g in range(ng)], axis=0)
            yc_ref[0, pl.ds(t, nc, stride=S5_T), :] = zc.T[:nc]


def _s5_matrices(a_re, a_im, log_dt, b_re, b_im, c_re, c_im, d):
    g, p, c, t = S5_GROUPS, S5_STATE, S5_GROUP, S5_T
    tc = t * c
    lam_re = jnp.minimum(a_re.astype(F32), -1e-4)
    lam_im = a_im.astype(F32)
    dt = jnp.exp(log_dt.astype(F32))[..., None]
    lr, li = lam_re * dt, lam_im * dt

    def a_pow(n, x_re, x_im):
        mag = jnp.exp(n * x_re)
        return mag * jnp.cos(n * x_im), mag * jnp.sin(n * x_im)

    ab_re, ab_im = a_pow(1.0, lr, li)
    den = lam_re * lam_re + lam_im * lam_im
    k_re = ((ab_re - 1.0) * lam_re + ab_im * lam_im) / den
    k_im = (ab_im * lam_re - (ab_re - 1.0) * lam_im) / den
    br, bi = b_re.astype(F32), b_im.astype(F32)
    bb_re = k_re[..., None] * br - k_im[..., None] * bi
    bb_im = k_re[..., None] * bi + k_im[..., None] * br
    bbt_re, bbt_im = jnp.swapaxes(bb_re, 2, 3), jnp.swapaxes(bb_im, 2, 3)
    ct_re = jnp.swapaxes(c_re.astype(F32), 2, 3)
    ct_im = jnp.swapaxes(c_im.astype(F32), 2, 3)

    n_up = np.repeat(np.arange(t + 1, dtype=np.float32), c)
    ca = []
    for dirn, n_lane in ((0, n_up), (1, t - n_up)):
        pr, pi = a_pow(n_lane[None, None, :], lr[dirn][:, :, None], li[dirn][:, :, None])
        cr_l = jnp.tile(ct_re[dirn], (1, 1, t + 1))
        ci_l = jnp.tile(ct_im[dirn], (1, 1, t + 1))
        ca.append(jnp.concatenate([cr_l * pr - ci_l * pi, -(cr_l * pi + ci_l * pr)], axis=1))
    wc_f, wc_r = ca[0][:, :, c:], ca[1][:, :, :tc]
    bb_cat = jnp.concatenate([bbt_re, bbt_im], axis=-1)
    lag_f = jnp.einsum('gik,gkl->gil', bb_cat[0], ca[0][:, :, :tc], precision=HIGHEST)
    lag_r = jnp.einsum('gik,gkl->gil', bb_cat[1], ca[1][:, :, c:], precision=HIGHEST)
    zeros = jnp.zeros_like(lag_f)
    ext_f = jnp.concatenate([zeros, lag_f], axis=-1)
    ext_r = jnp.concatenate([lag_r, zeros], axis=-1)
    toep = jnp.stack([ext_f[:, :, tc - c * s:2 * tc - c * s] + ext_r[:, :, c * (t - 1 - s):c * (t - 1 - s) + tc]
                      for s in range(t)], axis=1)
    skip = jnp.eye(tc, dtype=F32)[None] * jnp.tile(d.astype(F32).reshape(g, 1, c), (1, 1, t))
    toep = toep.reshape(g, tc, tc) + skip

    n_row = np.repeat(np.arange(t, dtype=np.float32), c)[None, :, None]
    ends = []
    for dirn, n in ((0, t - 1 - n_row), (1, n_row)):
        pr, pi = a_pow(n, lr[dirn][:, None, :], li[dirn][:, None, :])
        b_r, b_i = jnp.tile(bbt_re[dirn], (1, t, 1)), jnp.tile(bbt_im[dirn], (1, t, 1))
        ends += [pr * b_r - pi * b_i, pr * b_i + pi * b_r]
    wend = jnp.concatenate(ends, axis=-1)

    half = g // 2
    z2 = jnp.zeros((half, tc, tc), F32)
    toep_p = jnp.concatenate([jnp.concatenate([toep[0::2], z2], axis=-1),
                              jnp.concatenate([z2, toep[1::2]], axis=-1)], axis=1)
    w4 = wend.reshape(g, tc, 4, p)
    z4 = jnp.zeros((half, tc, 4, p), F32)
    wend_p = jnp.concatenate([jnp.stack([w4[0::2], z4], axis=3).reshape(half, tc, 8 * p),
                              jnp.stack([z4, w4[1::2]], axis=3).reshape(half, tc, 8 * p)], axis=1)
    wc4 = jnp.concatenate([wc_f, wc_r], axis=1).reshape(g, 4, p, tc)
    zc = jnp.zeros((half, 4, p, tc), F32)
    wc_p = jnp.stack([jnp.concatenate([wc4[0::2], zc], axis=-1),
                      jnp.concatenate([zc, wc4[1::2]], axis=-1)], axis=2).reshape(half, 8 * p, 2 * tc)
    a16 = jnp.stack([v for dirn in (0, 1) for v in a_pow(float(t), lr[dirn], li[dirn])], axis=0)
    return toep_p.astype(BF16), wend_p.astype(BF16), wc_p.astype(BF16), a16.reshape(4, g * p)


def _s5_mix(su_c, su_x, mats, ctx_out):
    toep_p, wend_p, wc_p, a16 = mats
    b, l, _ = su_c.shape
    s = su_x.shape[1]
    n_rows = (l + s) // S5_T
    halves = 2
    lanes = BW // halves
    ng = lanes // S5_GROUP
    pw = 2 * BW
    assert (s // S5_T) % 128 == 0 and (l // S5_T) % 8 == 0 and l // S5_T <= 128
    wspec = pl.BlockSpec((ng // 2, pw, pw), lambda bi, h: (h, 0, 0))
    out_specs = [pl.BlockSpec((1, s, lanes), lambda bi, h: (bi, 0, h))]
    out_shape = [jax.ShapeDtypeStruct((b, s, BW), F32)]
    if ctx_out:
        out_specs.append(pl.BlockSpec((1, l, lanes), lambda bi, h: (bi, 0, h)))
        out_shape.append(jax.ShapeDtypeStruct((b, l, BW), F32))
    res = pl.pallas_call(
        functools.partial(_s5_kernel, ctx_out=ctx_out),
        grid=(b, halves),
        in_specs=[pl.BlockSpec((1, s, lanes), lambda bi, h: (bi, 0, h)),
                  pl.BlockSpec((1, l, lanes), lambda bi, h: (bi, 0, h)),
                  wspec, wspec, wspec,
                  pl.BlockSpec((4, ng * S5_STATE), lambda bi, h: (0, h))],
        out_specs=out_specs,
        out_shape=out_shape,
        scratch_shapes=[pltpu.VMEM((ng * BW, s // S5_T), F32), pltpu.VMEM((ng * BW, 128), F32),
                        pltpu.VMEM((n_rows, ng * BW), BF16), pltpu.VMEM((n_rows, 4 * ng * S5_STATE), F32)],
        compiler_params=_cparams("arbitrary", "arbitrary"),
        name="s5_mixer",
    )(su_x, su_c, wend_p, toep_p, wc_p, a16)
    return (res[1] if ctx_out else None), res[0]


def _merge_kernel(zg_ref, oa_ref, ob_ref, ys_ref, od_ref, x_ref, mod_ref, g_ref, wb_ref, wo_ref, gw_ref, gb_ref,
                  o_ref):
    d = x_ref.shape[1]
    y = jax.nn.gelu(ys_ref[...].astype(F32))
    oc = (y * _sigmoid(_dot(y.astype(BF16), gw_ref[...]) + gb_ref[...])).astype(BF16)
    outs = (oa_ref[...], ob_ref[...], oc, od_ref[...])
    m = None
    for i, o in enumerate(outs):
        term = _sigmoid(zg_ref[:, i * d:(i + 1) * d].astype(F32)) * _dot(o, wb_ref[i])
        m = term if m is None else m + term
    mo = _dot(m.astype(BF16), wo_ref[...])
    o_ref[...] = x_ref[...] + mod_ref[0, 2:3, :] * _rms(mo, g_ref[...])


def _merge(z2, o_a, o_b, y_s, o_d, x2, mod, g_post, wb, wo, glu_w, glu_b, tiles_per_mod, tm):
    m, d = x2.shape
    row = lambda i: (i, 0)
    return pl.pallas_call(
        _merge_kernel,
        grid=(m // tm,),
        in_specs=[pl.BlockSpec((tm, Z_GATES_W), row)] + [pl.BlockSpec((tm, BW), row)] * 4
        + [pl.BlockSpec((tm, d), row),
           pl.BlockSpec((1, 8, d), lambda i: (i // tiles_per_mod, 0, 0)),
           _const_spec((1, d)), _const_spec(wb.shape), _const_spec(wo.shape),
           _const_spec(glu_w.shape), _const_spec((1, BW))],
        out_specs=pl.BlockSpec((tm, d), row),
        out_shape=jax.ShapeDtypeStruct((m, d), F32),
        compiler_params=_cparams("arbitrary"),
        name="merge",
    )(z2, o_a, o_b, y_s, o_d, x2, mod, g_post.reshape(1, d), wb, wo, glu_w, glu_b.reshape(1, BW).astype(F32))


def _ffn_dense_kernel(x_ref, mod_ref, gpre_ref, gpost_ref, w1_ref, w3_ref, w2_ref, o_ref):
    x = x_ref[...]
    h = (_rms(x, gpre_ref[...]) * (1.0 + mod_ref[0, 4:5, :]) + mod_ref[0, 3:4, :]).astype(BF16)
    a = _dot(h, w1_ref[...])
    a = (a * _sigmoid(a)) * _dot(h, w3_ref[...])
    f = _dot(a.astype(BF16), w2_ref[...])
    o_ref[...] = x + mod_ref[0, 5:6, :] * _rms(f, gpost_ref[...])


def _ffn_dense(x2, mod, g_pre, g_post, w1, w3, w2, tiles_per_mod, tm):
    m, d = x2.shape
    return pl.pallas_call(
        _ffn_dense_kernel,
        grid=(m // tm,),
        in_specs=[pl.BlockSpec((tm, d), lambda i: (i, 0)),
                  pl.BlockSpec((1, 8, d), lambda i: (i // tiles_per_mod, 0, 0)),
                  _const_spec((1, d)), _const_spec((1, d)),
                  _const_spec(w1.shape), _const_spec(w3.shape), _const_spec(w2.shape)],
        out_specs=pl.BlockSpec((tm, d), lambda i: (i, 0)),
        out_shape=jax.ShapeDtypeStruct((m, d), F32),
        compiler_params=_cparams("arbitrary"),
        name="ffn_dense",
    )(x2, mod, g_pre.reshape(1, d), g_post.reshape(1, d), w1, w3, w2)


def _route_kernel(x_ref, mod_ref, gpre_ref, r_ref, h_ref, idx_ref, wt_ref):
    h = _rms(x_ref[...], gpre_ref[...]) * (1.0 + mod_ref[0, 4:5, :]) + mod_ref[0, 3:4, :]
    h_hi = h.astype(BF16)
    h_ref[...] = h_hi
    h_lo = (h - h_hi.astype(F32)).astype(BF16)
    a = _dot(h_hi, r_ref[...])
    logits = a + pltpu.roll(a, 128 - N_EXPERTS, 1) + _dot(h_lo, r_ref[...])
    lane = lax.broadcasted_iota(jnp.int32, logits.shape, 1)
    logits = jnp.where(lane < N_EXPERTS, logits, 2.0 * NEG_INF)
    m0 = logits.max(axis=-1, keepdims=True)
    i0 = jnp.where(logits == m0, lane, 128).min(axis=-1, keepdims=True)
    rest = jnp.where(lane == i0, NEG_INF, logits)
    m1 = rest.max(axis=-1, keepdims=True)
    i1 = jnp.where(rest == m1, lane, 128).min(axis=-1, keepdims=True)
    e = jnp.exp(m1 - m0)
    w0 = 1.0 / (1.0 + e)
    idx_ref[...] = jnp.where(lane == 0, i0, jnp.where(lane == 1, i1, 0))
    wt_ref[...] = jnp.where(lane == 0, w0, jnp.where(lane == 1, e * w0, 0.0))


def _route(x2, mod, g_pre, router, tiles_per_mod, tm):
    m, d = x2.shape
    r_hi = router.astype(BF16)
    r_lo = (router.astype(F32) - r_hi.astype(F32)).astype(BF16)
    r_pad = jnp.pad(jnp.concatenate([r_hi, r_lo], axis=1), ((0, 0), (0, 128 - 2 * N_EXPERTS)))
    row = lambda i: (i, 0)
    return pl.pallas_call(
        _route_kernel,
        grid=(m // tm,),
        in_specs=[pl.BlockSpec((tm, d), row),
                  pl.BlockSpec((1, 8, d), lambda i: (i // tiles_per_mod, 0, 0)),
                  _const_spec((1, d)), _const_spec((d, 128))],
        out_specs=[pl.BlockSpec((tm, d), row), pl.BlockSpec((tm, 128), row), pl.BlockSpec((tm, 128), row)],
        out_shape=[jax.ShapeDtypeStruct((m, d), BF16), jax.ShapeDtypeStruct((m, 128), jnp.int32),
                   jax.ShapeDtypeStruct((m, 128), F32)],
        compiler_params=_cparams("arbitrary"),
        name="moe_route",
    )(x2, mod, g_pre.reshape(1, d), r_pad)


def _ffn_grouped_kernel(te_ref, nv_ref, h_ref, w1_ref, w3_ref, w2_ref, prev_ref, o_ref, acc_ref, *, tile_off):
    del prev_ref
    i, j = pl.program_id(0), pl.program_id(1)
    last = pl.num_programs(1) - 1

    @pl.when(j == 0)
    def _():
        acc_ref[...] = jnp.zeros_like(acc_ref)

    @pl.when(tile_off + i < nv_ref[0])
    def _():
        h = h_ref[...]
        a = _dot(h, w1_ref[0].astype(BF16))
        a = (a * _sigmoid(a)) * _dot(h, w3_ref[0].astype(BF16))
        acc_ref[...] += _dot(a.astype(BF16), w2_ref[0].astype(BF16))

    @pl.when(j == last)
    def _():
        o_ref[...] = acc_ref[...].astype(BF16)


def _ffn_grouped(h_chunk, tile_expert, n_valid, w1, w3, w2, prev, tile_off, tm, fc):
    r, d = h_chunk.shape
    f = w1.shape[2]
    nj = f // fc

    def col(i, j, nv):
        return jnp.where(tile_off + i < nv[0], j, nj - 1)

    grid_spec = pltpu.PrefetchScalarGridSpec(
        num_scalar_prefetch=2,
        grid=(r // tm, nj),
        in_specs=[pl.BlockSpec((tm, d), lambda i, j, te, nv: (i, 0)),
                  pl.BlockSpec((1, d, fc), lambda i, j, te, nv: (te[tile_off + i], 0, col(i, j, nv))),
                  pl.BlockSpec((1, d, fc), lambda i, j, te, nv: (te[tile_off + i], 0, col(i, j, nv))),
                  pl.BlockSpec((1, fc, d), lambda i, j, te, nv: (te[tile_off + i], col(i, j, nv), 0)),
                  pl.BlockSpec(memory_space=pl.ANY)],
        out_specs=pl.BlockSpec((tm, d), lambda i, j, te, nv: (tile_off + i, 0)),
        scratch_shapes=[pltpu.VMEM((tm, d), F32)],
    )
    return pl.pallas_call(
        functools.partial(_ffn_grouped_kernel, tile_off=tile_off),
        grid_spec=grid_spec,
        out_shape=jax.ShapeDtypeStruct(prev.shape, BF16),
        input_output_aliases={6: 0},
        compiler_params=_cparams("arbitrary", "arbitrary"),
        name="ffn_grouped",
    )(tile_expert, n_valid, h_chunk, w1, w3, w2, prev)


def _combine_kernel(o0_ref, o1_ref, wt_ref, x_ref, mod_ref, gpost_ref, o_ref):
    y = wt_ref[:, 0:1] * o0_ref[...].astype(F32) + wt_ref[:, 1:2] * o1_ref[...].astype(F32)
    o_ref[...] = x_ref[...] + mod_ref[0, 5:6, :] * _rms(y, gpost_ref[...])


def _combine(o0, o1, wts, x2, mod, g_post, tiles_per_mod, tm):
    m, d = x2.shape
    row = lambda i: (i, 0)
    return pl.pallas_call(
        _combine_kernel,
        grid=(m // tm,),
        in_specs=[pl.BlockSpec((tm, d), row), pl.BlockSpec((tm, d), row), pl.BlockSpec((tm, 128), row),
                  pl.BlockSpec((tm, d), row),
                  pl.BlockSpec((1, 8, d), lambda i: (i // tiles_per_mod, 0, 0)),
                  _const_spec((1, d))],
        out_specs=pl.BlockSpec((tm, d), row),
        out_shape=jax.ShapeDtypeStruct((m, d), F32),
        compiler_params=_cparams("arbitrary"),
        name="moe_combine",
    )(o0, o1, wts, x2, mod, g_post.reshape(1, d))


def _moe(x2, mod, g_pre, g_post, router, w1, w3, w2, tiles_per_mod, tm):
    m, d = x2.shape
    h, idx, wts = _route(x2, mod, g_pre, router, tiles_per_mod, tm)
    gt = MOE_GROUP_TILE
    e_flat = idx[:, :2].reshape(-1)
    onehot = (e_flat[:, None] == jnp.arange(N_EXPERTS)[None, :]).astype(jnp.int32)
    rank = jnp.take_along_axis(jnp.cumsum(onehot, axis=0), e_flat[:, None], axis=1)[:, 0] - 1
    counts = onehot.sum(axis=0)
    padded = (counts + gt - 1) // gt * gt
    ends = jnp.cumsum(padded)
    pos = (ends - padded)[e_flat] + rank
    n_tiles = (2 * m) // gt + N_EXPERTS
    src = jnp.zeros((n_tiles * gt,), jnp.int32).at[pos].set(jnp.arange(2 * m, dtype=jnp.int32) // 2,
                                                            unique_indices=True, mode='promise_in_bounds')
    tile_start = jnp.arange(n_tiles, dtype=jnp.int32) * gt
    tile_expert = jnp.minimum((ends[None, :] <= tile_start[:, None]).sum(axis=1), N_EXPERTS - 1).astype(jnp.int32)
    n_valid = (ends[-1] // gt).astype(jnp.int32).reshape(1)
    rows = lambda a, i: a.at[i].get(mode='promise_in_bounds')
    n_chunks = math.gcd(n_tiles, MOE_DISPATCH_CHUNKS)
    chunk_tiles = n_tiles // n_chunks
    out = jnp.zeros((n_tiles * gt, d), BF16)
    for ci in range(n_chunks):
        src_c = src[ci * chunk_tiles * gt:(ci + 1) * chunk_tiles * gt]
        out = _ffn_grouped(rows(h, src_c), tile_expert, n_valid, w1, w3, w2, out, ci * chunk_tiles, gt, MOE_F_CHUNK)
    pos2 = pos.reshape(m, 2)
    return _combine(rows(out, pos2[:, 0]), rows(out, pos2[:, 1]), wts, x2, mod, g_post, tiles_per_mod, tm)


def _z_weights(w_in):
    d = w_in.shape[0]
    hw = N_HEADS * HEAD_DIM
    kvw = SWA_KV_HEADS * HEAD_DIM
    sizes = (hw, hw, hw, hw, kvw, kvw, BW, BW, BW, Z_GATES_W)
    qa, ka, va, qd, kd, vd, gu, gv, su, gates = jnp.split(w_in, np.cumsum(sizes)[:-1].tolist(), axis=1)
    rep = N_HEADS // SWA_KV_HEADS
    dup = lambda w: jnp.repeat(w.reshape(d, SWA_KV_HEADS, 1, HEAD_DIM), rep, axis=2).reshape(d, hw)
    return jnp.concatenate([gates, qa, ka, va, qd, dup(kd), dup(vd), gu, gv, su], axis=1).astype(BF16)


def kernel(x, c, ctx, c_ctx, w_mod, b_mod, g_pre_mix, g_post_mix, g_pre_ffn, g_post_ffn, w_in, na_rpb, swa_sink, gmlp_ln_g, gmlp_ln_b, gmlp_ws, gmlp_bs, s5_a_re, s5_a_im, s5_log_dt, s5_b_re, s5_b_im, s5_c_re, s5_c_im, s5_d, s5_glu_w, s5_glu_b, w_branch, w_out, ffn_w1, ffn_w3, ffn_w2, moe_router, moe_w1, moe_w3, moe_w2):
    bsz, s, d = x.shape
    l = ctx.shape[1]
    depth = w_mod.shape[0]
    tm = 512
    rows = s // GRID_W

    c_all = jnp.zeros((8, d), F32).at[:bsz].set(c).at[bsz].set(c_ctx)
    mods = _modulation(c_all, w_mod, b_mod).reshape(depth, 8, 6, d)
    mods = jnp.pad(mods, ((0, 0), (0, 0), (0, 2), (0, 0)))
    rope_tabs = _rope_tables(s)
    swa_bias = _swa_bias(s)

    x2 = x.reshape(bsz * s, d)
    c2 = ctx.reshape(bsz * l, d)
    for i in range(depth):
        ctx_out = i < depth - 1
        mod_x = mods[i, :bsz]
        mod_c = mods[i, bsz:bsz + 1]
        w_z = _z_weights(w_in[i])
        zx, su_x = _project(x2, mod_x, g_pre_mix[i], w_z, s // tm, rope_tabs, tm)
        zc, su_c = _project(c2, mod_c, g_pre_mix[i], w_z, (bsz * l) // tm, None, tm)
        zx3 = zx.reshape(bsz, s, ZW)
        zc3 = zc.reshape(bsz, l, ZW)
        sink_rows = jnp.repeat(swa_sink[i].astype(F32) * LOG2E, ATT_TQ).reshape(1, N_HEADS * ATT_TQ)

        nq = s // ATT_TQ
        o_a = _local_attention(zx3, zc3, Z_QA, Z_KA, Z_VA, _na_bias(na_rpb[i], rows), None, ATT_TQ,
                               lambda qi, j: jnp.clip(qi - 1, 0, nq - 3) + j)
        o_d = _local_attention(zx3, zc3, Z_QD, Z_KD, Z_VD, swa_bias, sink_rows, SWA_WINDOW,
                               functools.partial(_swa_key_index, n_blocks=s // SWA_WINDOW))
        o_b = _gmlp(zx, gmlp_ln_g[i], gmlp_ln_b[i], gmlp_ws[i], gmlp_bs[i], tm)
        s5_mats = _s5_matrices(s5_a_re[i], s5_a_im[i], s5_log_dt[i], s5_b_re[i], s5_b_im[i],
                               s5_c_re[i], s5_c_im[i], s5_d[i])
        y_c, y_x = _s5_mix(su_c.reshape(bsz, l, BW), su_x.reshape(bsz, s, BW), s5_mats, ctx_out)

        wb = w_branch[i].astype(BF16)
        wo = w_out[i].astype(BF16)
        glu_w = s5_glu_w[i].astype(BF16)
        x2 = _merge(zx, o_a.reshape(bsz * s, BW), o_b, y_x.reshape(bsz * s, BW), o_d.reshape(bsz * s, BW),
                    x2, mod_x, g_post_mix[i], wb, wo, glu_w, s5_glu_b[i], s // tm, tm)
        if ctx_out:
            o_a_c = _ctx_attention(zc3, Z_QA, Z_KA, Z_VA, None)
            o_d_c = _ctx_attention(zc3, Z_QD, Z_KD, Z_VD, sink_rows)
            o_b_c = _gmlp(zc, gmlp_ln_g[i], gmlp_ln_b[i], gmlp_ws[i], gmlp_bs[i], tm)
            c2 = _merge(zc, o_a_c.reshape(bsz * l, BW), o_b_c, y_c.reshape(bsz * l, BW), o_d_c.reshape(bsz * l, BW),
                        c2, mod_c, g_post_mix[i], wb, wo, glu_w, s5_glu_b[i], (bsz * l) // tm, tm)

        j = i // 2
        if i % 2 == 0:
            w1, w3, w2 = ffn_w1[j].astype(BF16), ffn_w3[j].astype(BF16), ffn_w2[j].astype(BF16)
            x2 = _ffn_dense(x2, mod_x, g_pre_ffn[i], g_post_ffn[i], w1, w3, w2, s // tm, tm)
            if ctx_out:
                c2 = _ffn_dense(c2, mod_c, g_pre_ffn[i], g_post_ffn[i], w1, w3, w2, (bsz * l) // tm, tm)
        else:
            w1, w3, w2 = moe_w1[j], moe_w3[j], moe_w2[j]
            x2 = _moe(x2, mod_x, g_pre_ffn[i], g_post_ffn[i], moe_router[j], w1, w3, w2, s // tm, tm)
            if ctx_out:
                c2 = _moe(c2, mod_c, g_pre_ffn[i], g_post_ffn[i], moe_router[j], w1, w3, w2, (bsz * l) // tm, tm)
    return x2.reshape(bsz, s, d)
```

```python
import functools
import math

import numpy as np
import jax
import jax.numpy as jnp
from jax import lax
from jax.experimental import pallas as pl
from jax.experimental.pallas import tpu as pltpu

F32 = jnp.float32
BF16 = jnp.bfloat16
HIGHEST = lax.Precision.HIGHEST

GRID_W = 64
HEAD_DIM = 64
N_HEADS = 4
NA_ROWS = 8
NA_COLS = 16
SWA_KV_HEADS = 2
SWA_WINDOW = 128
GMLP_GROUPS = 4
GMLP_CHUNK = 128
S5_GROUP = 16
S5_GROUPS = 16
S5_STATE = 64
N_EXPERTS = 8
ROPE_BASE = 10000.0
EPS = 1e-6
NEG_INF = -1e30
LOG2E = math.log2(math.e)
Q_SCALE = HEAD_DIM ** -0.5 * LOG2E

BW = 256
S5_T = 16
S5_ROWS = 8
Z_GATES_W = 4096
(Z_QA, Z_KA, Z_VA, Z_QD, Z_KD, Z_VD, Z_GU, Z_GV, Z_SU) = range(Z_GATES_W // BW, Z_GATES_W // BW + 9)
ZW = Z_GATES_W + 9 * BW

V7X_VMEM_LIMIT = 56 * 1024 * 1024
ATT_TQ = 256
ONES_ROWS = 16
ATT_CHAIN_HEADS = 2
SWA_LOCAL_BLOCKS = ATT_TQ // SWA_WINDOW + 2
MOE_GROUP_TILE = 1024
MOE_F_CHUNK = 512
MOE_DISPATCH_CHUNKS = 4


def _cparams(*sem):
    return pltpu.CompilerParams(dimension_semantics=sem, vmem_limit_bytes=V7X_VMEM_LIMIT)


def _const_spec(shape):
    nd = len(shape)
    return pl.BlockSpec(shape, lambda *_: (0,) * nd, pipeline_mode=pl.Buffered(1))


def _dot(a, b):
    return jnp.dot(a, b, preferred_element_type=F32)


def _dot_nt(a, b):
    return lax.dot_general(a, b, (((1,), (1,)), ((), ())), preferred_element_type=F32)


def _sigmoid(x):
    return 0.5 * jnp.tanh(0.5 * x) + 0.5


def _rms(x, g):
    return x * lax.rsqrt(jnp.mean(x * x, axis=-1, keepdims=True) + EPS) * g


def _mod_kernel(c_ref, w_ref, b_ref, o_ref):
    c = c_ref[...]
    a = c * _sigmoid(c)
    o_ref[0] = jnp.dot(a, w_ref[0], preferred_element_type=F32, precision=HIGHEST) + b_ref[0]


def _modulation(c_all, w_mod, b_mod):
    depth, d, n = w_mod.shape
    tn = 1536
    return pl.pallas_call(
        _mod_kernel,
        grid=(depth, n // tn),
        in_specs=[pl.BlockSpec((8, d), lambda l, j: (0, 0)),
                  pl.BlockSpec((1, d, tn), lambda l, j: (l, 0, j)),
                  pl.BlockSpec((1, 1, tn), lambda l, j: (l, 0, j))],
        out_specs=pl.BlockSpec((1, 8, tn), lambda l, j: (l, 0, j)),
        out_shape=jax.ShapeDtypeStruct((depth, 8, n), F32),
        compiler_params=_cparams("arbitrary", "arbitrary"),
        name="modulation",
    )(c_all, w_mod, b_mod.reshape(depth, 1, n))


def _proj_kernel(*refs, rope, n_chunk):
    if rope:
        x_ref, mod_ref, g_ref, w_ref, cos_ref, sin_ref, o_ref, su_ref = refs
    else:
        x_ref, mod_ref, g_ref, w_ref, o_ref, su_ref = refs
    h = _rms(x_ref[...], g_ref[...]) * (1.0 + mod_ref[0, 1:2, :]) + mod_ref[0, 0:1, :]
    h = h.astype(BF16)
    rope_blocks = (Z_QD, Z_KD) if rope else ()
    for j in range(ZW // n_chunk):
        lo = j * n_chunk
        r = _dot(h, w_ref[:, lo:lo + n_chunk])
        blocks = range(lo // BW, (lo + n_chunk) // BW)
        if Z_SU in blocks:
            su_ref[...] = r[:, Z_SU * BW - lo:(Z_SU + 1) * BW - lo]
        if not any(b in rope_blocks or b in (Z_QA, Z_QD) for b in blocks):
            o_ref[:, lo:lo + n_chunk] = r.astype(BF16)
            continue
        for b in blocks:
            t = r[:, b * BW - lo:(b + 1) * BW - lo]
            if b in rope_blocks:
                lane = lax.broadcasted_iota(jnp.int32, (1, BW), 1)
                low_half = (lane % (HEAD_DIM // 2)) < (HEAD_DIM // 4)
                partner = jnp.where(low_half, pltpu.roll(t, BW - HEAD_DIM // 4, 1),
                                    pltpu.roll(t, HEAD_DIM // 4, 1))
                t = t * cos_ref[...] + partner * sin_ref[...]
            if b in (Z_QA, Z_QD):
                t = t * Q_SCALE
            o_ref[:, b * BW:(b + 1) * BW] = t.astype(BF16)


def _project(x2, mod, g, w_z, tiles_per_mod, rope_tabs, tm):
    m, d = x2.shape
    rope = rope_tabs is not None
    in_specs = [pl.BlockSpec((tm, d), lambda i: (i, 0)),
                pl.BlockSpec((1, 8, d), lambda i: (i // tiles_per_mod, 0, 0)),
                _const_spec((1, d)),
                _const_spec((d, ZW))]
    args = [x2, mod, g.reshape(1, d), w_z]
    if rope:
        n_rt = rope_tabs[0].shape[0] // tm
        in_specs += [pl.BlockSpec((tm, BW), lambda i: (i % n_rt, 0))] * 2
        args += list(rope_tabs)
    return pl.pallas_call(
        functools.partial(_proj_kernel, rope=rope, n_chunk=1280),
        grid=(m // tm,),
        in_specs=in_specs,
        out_specs=[pl.BlockSpec((tm, ZW), lambda i: (i, 0)), pl.BlockSpec((tm, BW), lambda i: (i, 0))],
        out_shape=[jax.ShapeDtypeStruct((m, ZW), BF16), jax.ShapeDtypeStruct((m, BW), F32)],
        compiler_params=_cparams("arbitrary"),
        name="project_in",
    )(*args)


def _attn_kernel(*refs, n_local, use_sink):
    q_ref = refs[0]
    kv = refs[1:1 + 2 * n_local]
    kc_ref, vc_ref = refs[1 + 2 * n_local], refs[2 + 2 * n_local]
    pos = 3 + 2 * n_local
    bias_ref = sink_ref = None
    if n_local:
        bias_ref = refs[pos]
        pos += 1
    if use_sink:
        sink_ref = refs[pos]
        pos += 1
    o_ref = refs[pos]

    q = q_ref[0]
    tq = q.shape[0]
    lane_head = lax.broadcasted_iota(jnp.int32, (1, BW), 1) // HEAD_DIM
    k_refs = [kv[2 * j] for j in range(n_local)] + [kc_ref]
    v_t = [r[0].T for r in [kv[2 * j + 1] for j in range(n_local)] + [vc_ref]]
    chains = [range(h0, h0 + ATT_CHAIN_HEADS) for h0 in range(0, N_HEADS, ATT_CHAIN_HEADS)]

    def score_stage(heads):
        cols = slice(heads[0] * tq, (heads[-1] + 1) * tq)
        q_heads = jnp.concatenate([jnp.where(lane_head == h, q, jnp.zeros_like(q)) for h in heads], axis=0)
        scores = []
        for j, k_ref in enumerate(k_refs):
            s = _dot_nt(k_ref[0], q_heads)
            if j < n_local:
                kb = k_ref.shape[1]
                s = s + bias_ref[0, j * kb:(j + 1) * kb, cols]
            scores.append(s)
        return scores

    def softmax_value_stage(heads, scores):
        cols = slice(heads[0] * tq, (heads[-1] + 1) * tq)
        mx = scores[0].max(axis=0, keepdims=True)
        for s in scores[1:]:
            mx = jnp.maximum(mx, s.max(axis=0, keepdims=True))
        if use_sink:
            mx = jnp.maximum(mx, sink_ref[:, cols])
        o = [jnp.zeros((HEAD_DIM + ONES_ROWS, tq), F32) for _ in heads]
        for s, vt in zip(scores, v_t):
            p = jnp.exp2(s - mx).astype(BF16)
            ones = jnp.ones((ONES_ROWS, vt.shape[1]), BF16)
            for i, h in enumerate(heads):
                lhs = jnp.concatenate([vt[h * HEAD_DIM:(h + 1) * HEAD_DIM, :], ones], axis=0)
                o[i] = o[i] + _dot(lhs, p[:, i * tq:(i + 1) * tq])
        res = []
        for i, h in enumerate(heads):
            den = o[i][HEAD_DIM:HEAD_DIM + 1, :]
            if use_sink:
                den = den + jnp.exp2(sink_ref[:, h * tq:(h + 1) * tq] - mx[:, i * tq:(i + 1) * tq])
            res.append(o[i][:HEAD_DIM, :] / den)
        return res

    outs = []
    pending = score_stage(chains[0])
    for c, heads in enumerate(chains):
        scores = pending
        if c + 1 < len(chains):
            pending = score_stage(chains[c + 1])
        outs += softmax_value_stage(heads, scores)
    o_ref[0] = jnp.concatenate(outs, axis=0).T.astype(BF16)


def _local_attention(z, z_c, q_col, k_col, v_col, bias, sink_rows, key_block, key_index):
    b, s, _ = z.shape
    l = z_c.shape[1]
    tq = ATT_TQ
    nq = s // tq
    n_local = bias.shape[1] // key_block
    assert nq >= 3 and l == BW

    def kv_spec(col, j):
        return pl.BlockSpec((1, key_block, BW), lambda bi, i: (bi, key_index(i, j), col))

    def pat(i):
        return jnp.where(i == 0, 0, jnp.where(i == nq - 1, 2, 1))

    in_specs = [pl.BlockSpec((1, tq, BW), lambda bi, i: (bi, i, q_col))]
    args = [z]
    for j in range(n_local):
        in_specs += [kv_spec(k_col, j), kv_spec(v_col, j)]
        args += [z, z]
    in_specs += [pl.BlockSpec((1, l, BW), lambda bi, i: (bi, 0, k_col)),
                 pl.BlockSpec((1, l, BW), lambda bi, i: (bi, 0, v_col)),
                 pl.BlockSpec((1,) + bias.shape[1:], lambda bi, i: (pat(i), 0, 0))]
    args += [z_c, z_c, bias]
    if sink_rows is not None:
        in_specs.append(_const_spec(sink_rows.shape))
        args.append(sink_rows)
    return pl.pallas_call(
        functools.partial(_attn_kernel, n_local=n_local, use_sink=sink_rows is not None),
        grid=(b, nq),
        in_specs=in_specs,
        out_specs=pl.BlockSpec((1, tq, BW), lambda bi, i: (bi, i, 0)),
        out_shape=jax.ShapeDtypeStruct((b, s, BW), BF16),
        compiler_params=_cparams("arbitrary", "arbitrary"),
        name="local_attention",
    )(*args)


def _ctx_attention(z_c, q_col, k_col, v_col, sink_rows):
    b, l, _ = z_c.shape
    in_specs = [pl.BlockSpec((1, l, BW), lambda bi, c=col: (bi, 0, c)) for col in (q_col, k_col, v_col)]
    args = [z_c, z_c, z_c]
    if sink_rows is not None:
        in_specs.append(_const_spec(sink_rows.shape))
        args.append(sink_rows)
    return pl.pallas_call(
        functools.partial(_attn_kernel, n_local=0, use_sink=sink_rows is not None),
        grid=(b,),
        in_specs=in_specs,
        out_specs=pl.BlockSpec((1, l, BW), lambda bi: (bi, 0, 0)),
        out_shape=jax.ShapeDtypeStruct((b, l, BW), BF16),
        compiler_params=_cparams("arbitrary"),
        name="ctx_attention",
    )(*args)


def _na_bias(rpb, rows):
    tile_rows = ATT_TQ // GRID_W
    nq = rows // tile_rows
    col = np.arange(GRID_W)
    cs = np.clip(col - NA_COLS // 2, 0, GRID_W - NA_COLS)[:, None]
    kc = col[None, :]
    sel_c = ((kc - col[:, None] + NA_COLS - 1)[None] == np.arange(2 * NA_COLS - 1)[:, None, None]) \
        & ((kc >= cs) & (kc < cs + NA_COLS))[None]
    sel_r = []
    for i in (0, 1, nq - 1):
        base = min(max(i - 1, 0), nq - 3)
        qr = (tile_rows * i + np.arange(tile_rows))[:, None]
        kr = (tile_rows * base + np.arange(3 * tile_rows))[None, :]
        rs = np.clip(qr - NA_ROWS // 2, 0, rows - NA_ROWS)
        sel_r.append(((kr - qr + NA_ROWS - 1)[None] == np.arange(2 * NA_ROWS - 1)[:, None, None])
                     & ((kr >= rs) & (kr < rs + NA_ROWS))[None])
    sel_r = np.stack(sel_r).astype(np.float32)
    sel_c = sel_c.astype(np.float32)
    t1 = jnp.einsum('paqk,hab->phqkb', sel_r, rpb.astype(F32), precision=HIGHEST)
    bias = jnp.einsum('phqkb,bcd->pkdhqc', t1, sel_c, precision=HIGHEST)
    valid = np.einsum('paqk,bcd->pkdqc', sel_r, sel_c) > 0.5
    mask = np.where(valid, 0.0, NEG_INF).astype(np.float32)[:, :, :, None]
    return (bias * LOG2E + mask).reshape(3, 3 * ATT_TQ, rpb.shape[0] * ATT_TQ)


def _swa_key_index(i, j, n_blocks):
    return jnp.clip(i * (ATT_TQ // SWA_WINDOW) - 1 + j, 0, n_blocks - 1)


def _swa_bias(s):
    nq = s // ATT_TQ
    tq = np.arange(ATT_TQ)[:, None]
    tk = np.arange(SWA_LOCAL_BLOCKS * SWA_WINDOW)[None, :]
    out = []
    for i in (0, 1, nq - 1):
        kpos = ATT_TQ * i - SWA_WINDOW + tk
        valid = (np.abs(kpos - (ATT_TQ * i + tq)) <= SWA_WINDOW) & (kpos >= 0) & (kpos < s)
        out.append(np.tile(np.where(valid, 0.0, NEG_INF).astype(np.float32).T, (1, N_HEADS)))
    return jnp.asarray(np.stack(out))


def _rope_tables(s):
    nq = HEAD_DIM // 4
    t = jnp.arange(s)
    inv = ROPE_BASE ** (-jnp.arange(nq, dtype=F32) / nq)
    ang_r = (t // GRID_W).astype(F32)[:, None] * inv[None, :]
    ang_c = (t % GRID_W).astype(F32)[:, None] * inv[None, :]
    cos = jnp.concatenate([jnp.cos(ang_r)] * 2 + [jnp.cos(ang_c)] * 2, axis=-1)
    sin = jnp.concatenate([-jnp.sin(ang_r), jnp.sin(ang_r), -jnp.sin(ang_c), jnp.sin(ang_c)], axis=-1)
    return jnp.tile(cos, (1, N_HEADS)), jnp.tile(sin, (1, N_HEADS))


def _gmlp_kernel(u_ref, v_ref, g_ref, b_ref, w_ref, bs_ref, o_ref):
    lane_grp = lax.broadcasted_iota(jnp.int32, (1, BW), 1) // (BW // GMLP_GROUPS)
    for c in range(u_ref.shape[0] // GMLP_CHUNK):
        rows = slice(c * GMLP_CHUNK, (c + 1) * GMLP_CHUNK)
        u = jax.nn.gelu(u_ref[rows, :].astype(F32))
        v = jax.nn.gelu(v_ref[rows, :].astype(F32))
        mu = jnp.mean(v, axis=-1, keepdims=True)
        vc = v - mu
        v = vc * lax.rsqrt(jnp.mean(vc * vc, axis=-1, keepdims=True) + EPS) * g_ref[...] + b_ref[...]
        v = v.astype(BF16)
        stack = jnp.concatenate([jnp.where(lane_grp == g, v, jnp.zeros_like(v)) for g in range(GMLP_GROUPS)], axis=0)
        sg = _dot(w_ref[...], stack) + bs_ref[...]
        o_ref[rows, :] = (u * sg).astype(BF16)


def _gmlp(z2, ln_g, ln_b, ws, bs, tm):
    m = z2.shape[0]
    w_cat = jnp.concatenate([ws[g] for g in range(GMLP_GROUPS)], axis=1).astype(BF16)
    bs_l = jnp.repeat(bs.T.astype(F32), BW // GMLP_GROUPS, axis=1)
    return pl.pallas_call(
        _gmlp_kernel,
        grid=(m // tm,),
        in_specs=[pl.BlockSpec((tm, BW), lambda i: (i, Z_GU)),
                  pl.BlockSpec((tm, BW), lambda i: (i, Z_GV)),
                  _const_spec((1, BW)), _const_spec((1, BW)),
                  _const_spec(w_cat.shape), _const_spec(bs_l.shape)],
        out_specs=pl.BlockSpec((tm, BW), lambda i: (i, 0)),
        out_shape=jax.ShapeDtypeStruct((m, BW), BF16),
        compiler_params=_cparams("arbitrary"),
        name="gmlp",
    )(z2, z2, ln_g.reshape(1, BW).astype(F32), ln_b.reshape(1, BW).astype(F32), w_cat, bs_l)


def _s5_kernel(su_ref, suc_ref, wend_ref, toep_ref, wc_ref, a16_ref, *rest, ctx_out):
    if ctx_out:
        y_ref, yc_ref, ut_ref, utc_ref, up_ref, st_ref = rest
    else:
        y_ref, ut_ref, utc_ref, up_ref, st_ref = rest
    nj, nc = su_ref.shape[1] // S5_T, suc_ref.shape[1] // S5_T
    lanes = su_ref.shape[2]
    ng = lanes // S5_GROUP
    hw = ng * S5_STATE
    cpad = utc_ref.shape[1]

    def grp(g, t):
        return slice(g * BW + t * S5_GROUP, g * BW + (t + 1) * S5_GROUP)

    for t in range(S5_T):
        at = su_ref[0, pl.ds(t, nj, stride=S5_T), :].T
        ct = jnp.concatenate([suc_ref[0, pl.ds(t, nc, stride=S5_T), :],
                              jnp.zeros((cpad - nc, lanes), F32)], axis=0).T
        for g in range(ng):
            ut_ref[grp(g, t), :] = at[g * S5_GROUP:(g + 1) * S5_GROUP, :]
            utc_ref[grp(g, t), :] = ct[g * S5_GROUP:(g + 1) * S5_GROUP, :]

    for p in range(ng // 2):
        blocks = []
        for g in (2 * p, 2 * p + 1):
            xt = ut_ref[g * BW:(g + 1) * BW, :].T
            ct = utc_ref[g * BW:(g + 1) * BW, :].T[:nc]
            blocks.append(jnp.concatenate([ct, xt], axis=0))
        up = jnp.concatenate(blocks, axis=1).astype(BF16)
        up_ref[:, p * 2 * BW:(p + 1) * 2 * BW] = up
        e = _dot(up, wend_ref[p])
        for k in range(4):
            st_ref[:, k * hw + p * 128:k * hw + (p + 1) * 128] = e[:, k * 128:(k + 1) * 128]

    n_tiles, n_ctiles = (nc + nj) // 8, nc // 8
    a = [a16_ref[k:k + 1, :] for k in range(4)]

    def tile_step(k, carry):
        fr, fi, rr, ri = carry
        kf = pl.multiple_of(k * 8, 8)
        kr = jnp.where(k < n_ctiles, n_ctiles - 1 - k, n_tiles - 1 - (k - n_ctiles))
        kr = pl.multiple_of(kr * 8, 8)
        ef_r, ef_i = st_ref[pl.ds(kf, 8), 0:hw], st_ref[pl.ds(kf, 8), hw:2 * hw]
        er_r, er_i = st_ref[pl.ds(kr, 8), 2 * hw:3 * hw], st_ref[pl.ds(kr, 8), 3 * hw:4 * hw]
        hf_r, hf_i, hr_r, hr_i = [], [], [None] * 8, [None] * 8
        for r in range(8):
            hf_r.append(fr)
            hf_i.append(fi)
            fr, fi = a[0] * fr - a[1] * fi + ef_r[r:r + 1], a[0] * fi + a[1] * fr + ef_i[r:r + 1]
            q = 7 - r
            hr_r[q], hr_i[q] = rr, ri
            rr, ri = a[2] * rr - a[3] * ri + er_r[q:q + 1], a[2] * ri + a[3] * rr + er_i[q:q + 1]
        st_ref[pl.ds(kf, 8), 0:hw] = jnp.concatenate(hf_r, axis=0)
        st_ref[pl.ds(kf, 8), hw:2 * hw] = jnp.concatenate(hf_i, axis=0)
        st_ref[pl.ds(kr, 8), 2 * hw:3 * hw] = jnp.concatenate(hr_r, axis=0)
        st_ref[pl.ds(kr, 8), 3 * hw:4 * hw] = jnp.concatenate(hr_i, axis=0)
        return fr, fi, rr, ri

    zero = jnp.zeros((1, hw), F32)
    lax.fori_loop(0, n_tiles, tile_step, (zero, zero, zero, zero))

    for p in range(ng // 2):
        h = jnp.concatenate([st_ref[:, k * hw + p * 128:k * hw + (p + 1) * 128] for k in range(4)], axis=1)
        y = _dot(up_ref[:, p * 2 * BW:(p + 1) * 2 * BW], toep_ref[p]) + _dot(h.astype(BF16), wc_ref[p])
        for gg in range(2):
            g = 2 * p + gg
            yg = y[:, gg * BW:(gg + 1) * BW]
            ut_ref[g * BW:(g + 1) * BW, :] = yg[nc:, :].T
            if ctx_out:
                utc_ref[g * BW:(g + 1) * BW, :] = jnp.concatenate(
                    [yg[:nc, :], jnp.zeros((cpad - nc, BW), F32)], axis=0).T

    for t in range(S5_T):
        z = jnp.concatenate([ut_ref[grp(g, t), :] for g in range(ng)], axis=0)
        y_ref[0, pl.ds(t, nj, stride=S5_T), :] = z.T
        if ctx_out:
            zc = jnp.concatenate([utc_ref[grp(g, t), :] for g in range(ng)], axis=0)
            yc_ref[0, pl.ds(t, nc, stride=S5_T), :] = zc.T[:nc]


def _s5_matrices(a_re, a_im, log_dt, b_re, b_im, c_re, c_im, d):
    g, p, c, t = S5_GROUPS, S5_STATE, S5_GROUP, S5_T
    tc = t * c
    lam_re = jnp.minimum(a_re.astype(F32), -1e-4)
    lam_im = a_im.astype(F32)
    dt = jnp.exp(log_dt.astype(F32))[..., None]
    lr, li = lam_re * dt, lam_im * dt

    def a_pow(n, x_re, x_im):
        mag = jnp.exp(n * x_re)
        return mag * jnp.cos(n * x_im), mag * jnp.sin(n * x_im)

    ab_re, ab_im = a_pow(1.0, lr, li)
    den = lam_re * lam_re + lam_im * lam_im
    k_re = ((ab_re - 1.0) * lam_re + ab_im * lam_im) / den
    k_im = (ab_im * lam_re - (ab_re - 1.0) * lam_im) / den
    br, bi = b_re.astype(F32), b_im.astype(F32)
    bb_re = k_re[..., None] * br - k_im[..., None] * bi
    bb_im = k_re[..., None] * bi + k_im[..., None] * br
    bbt_re, bbt_im = jnp.swapaxes(bb_re, 2, 3), jnp.swapaxes(bb_im, 2, 3)
    ct_re = jnp.swapaxes(c_re.astype(F32), 2, 3)
    ct_im = jnp.swapaxes(c_im.astype(F32), 2, 3)

    n_up = np.repeat(np.arange(t + 1, dtype=np.float32), c)
    ca = []
    for dirn, n_lane in ((0, n_up), (1, t - n_up)):
        pr, pi = a_pow(n_lane[None, None, :], lr[dirn][:, :, None], li[dirn][:, :, None])
        cr_l = jnp.tile(ct_re[dirn], (1, 1, t + 1))
        ci_l = jnp.tile(ct_im[dirn], (1, 1, t + 1))
        ca.append(jnp.concatenate([cr_l * pr - ci_l * pi, -(cr_l * pi + ci_l * pr)], axis=1))
    wc_f, wc_r = ca[0][:, :, c:], ca[1][:, :, :tc]
    bb_cat = jnp.concatenate([bbt_re, bbt_im], axis=-1)
    lag_f = jnp.einsum('gik,gkl->gil', bb_cat[0], ca[0][:, :, :tc], precision=HIGHEST)
    lag_r = jnp.einsum('gik,gkl->gil', bb_cat[1], ca[1][:, :, c:], precision=HIGHEST)
    zeros = jnp.zeros_like(lag_f)
    ext_f = jnp.concatenate([zeros, lag_f], axis=-1)
    ext_r = jnp.concatenate([lag_r, zeros], axis=-1)
    toep = jnp.stack([ext_f[:, :, tc - c * s:2 * tc - c * s] + ext_r[:, :, c * (t - 1 - s):c * (t - 1 - s) + tc]
                      for s in range(t)], axis=1)
    skip = jnp.eye(tc, dtype=F32)[None] * jnp.tile(d.astype(F32).reshape(g, 1, c), (1, 1, t))
    toep = toep.reshape(g, tc, tc) + skip

    n_row = np.repeat(np.arange(t, dtype=np.float32), c)[None, :, None]
    ends = []
    for dirn, n in ((0, t - 1 - n_row), (1, n_row)):
        pr, pi = a_pow(n, lr[dirn][:, None, :], li[dirn][:, None, :])
        b_r, b_i = jnp.tile(bbt_re[dirn], (1, t, 1)), jnp.tile(bbt_im[dirn], (1, t, 1))
        ends += [pr * b_r - pi * b_i, pr * b_i + pi * b_r]
    wend = jnp.concatenate(ends, axis=-1)

    half = g // 2
    z2 = jnp.zeros((half, tc, tc), F32)
    toep_p = jnp.concatenate([jnp.concatenate([toep[0::2], z2], axis=-1),
                              jnp.concatenate([z2, toep[1::2]], axis=-1)], axis=1)
    w4 = wend.reshape(g, tc, 4, p)
    z4 = jnp.zeros((half, tc, 4, p), F32)
    wend_p = jnp.concatenate([jnp.stack([w4[0::2], z4], axis=3).reshape(half, tc, 8 * p),
                              jnp.stack([z4, w4[1::2]], axis=3).reshape(half, tc, 8 * p)], axis=1)
    wc4 = jnp.concatenate([wc_f, wc_r], axis=1).reshape(g, 4, p, tc)
    zc = jnp.zeros((half, 4, p, tc), F32)
    wc_p = jnp.stack([jnp.concatenate([wc4[0::2], zc], axis=-1),
                      jnp.concatenate([zc, wc4[1::2]], axis=-1)], axis=2).reshape(half, 8 * p, 2 * tc)
    a16 = jnp.stack([v for dirn in (0, 1) for v in a_pow(float(t), lr[dirn], li[dirn])], axis=0)
    return toep_p.astype(BF16), wend_p.astype(BF16), wc_p.astype(BF16), a16.reshape(4, g * p)


def _s5_mix(su_c, su_x, mats, ctx_out):
    toep_p, wend_p, wc_p, a16 = mats
    b, l, _ = su_c.shape
    s = su_x.shape[1]
    n_rows = (l + s) // S5_T
    halves = 2
    lanes = BW // halves
    ng = lanes // S5_GROUP
    pw = 2 * BW
    assert (s // S5_T) % 128 == 0 and (l // S5_T) % 8 == 0 and l // S5_T <= 128
    wspec = pl.BlockSpec((ng // 2, pw, pw), lambda bi, h: (h, 0, 0))
    out_specs = [pl.BlockSpec((1, s, lanes), lambda bi, h: (bi, 0, h))]
    out_shape = [jax.ShapeDtypeStruct((b, s, BW), F32)]
    if ctx_out:
        out_specs.append(pl.BlockSpec((1, l, lanes), lambda bi, h: (bi, 0, h)))
        out_shape.append(jax.ShapeDtypeStruct((b, l, BW), F32))
    res = pl.pallas_call(
        functools.partial(_s5_kernel, ctx_out=ctx_out),
        grid=(b, halves),
        in_specs=[pl.BlockSpec((1, s, lanes), lambda bi, h: (bi, 0, h)),
                  pl.BlockSpec((1, l, lanes), lambda bi, h: (bi, 0, h)),
                  wspec, wspec, wspec,
                  pl.BlockSpec((4, ng * S5_STATE), lambda bi, h: (0, h))],
        out_specs=out_specs,
        out_shape=out_shape,
        scratch_shapes=[pltpu.VMEM((ng * BW, s // S5_T), F32), pltpu.VMEM((ng * BW, 128), F32),
                        pltpu.VMEM((n_rows, ng * BW), BF16), pltpu.VMEM((n_rows, 4 * ng * S5_STATE), F32)],
        compiler_params=_cparams("arbitrary", "arbitrary"),
        name="s5_mixer",
    )(su_x, su_c, wend_p, toep_p, wc_p, a16)
    return (res[1] if ctx_out else None), res[0]


def _merge_kernel(zg_ref, oa_ref, ob_ref, ys_ref, od_ref, x_ref, mod_ref, g_ref, wb_ref, wo_ref, gw_ref, gb_ref,
                  o_ref):
    d = x_ref.shape[1]
    y = jax.nn.gelu(ys_ref[...].astype(F32))
    oc = (y * _sigmoid(_dot(y.astype(BF16), gw_ref[...]) + gb_ref[...])).astype(BF16)
    outs = (oa_ref[...], ob_ref[...], oc, od_ref[...])
    m = None
    for i, o in enumerate(outs):
        term = _sigmoid(zg_ref[:, i * d:(i + 1) * d].astype(F32)) * _dot(o, wb_ref[i])
        m = term if m is None else m + term
    mo = _dot(m.astype(BF16), wo_ref[...])
    o_ref[...] = x_ref[...] + mod_ref[0, 2:3, :] * _rms(mo, g_ref[...])


def _merge(z2, o_a, o_b, y_s, o_d, x2, mod, g_post, wb, wo, glu_w, glu_b, tiles_per_mod, tm):
    m, d = x2.shape
    row = lambda i: (i, 0)
    return pl.pallas_call(
        _merge_kernel,
        grid=(m // tm,),
        in_specs=[pl.BlockSpec((tm, Z_GATES_W), row)] + [pl.BlockSpec((tm, BW), row)] * 4
        + [pl.BlockSpec((tm, d), row),
           pl.BlockSpec((1, 8, d), lambda i: (i // tiles_per_mod, 0, 0)),
           _const_spec((1, d)), _const_spec(wb.shape), _const_spec(wo.shape),
           _const_spec(glu_w.shape), _const_spec((1, BW))],
        out_specs=pl.BlockSpec((tm, d), row),
        out_shape=jax.ShapeDtypeStruct((m, d), F32),
        compiler_params=_cparams("arbitrary"),
        name="merge",
    )(z2, o_a, o_b, y_s, o_d, x2, mod, g_post.reshape(1, d), wb, wo, glu_w, glu_b.reshape(1, BW).astype(F32))


def _ffn_dense_kernel(x_ref, mod_ref, gpre_ref, gpost_ref, w1_ref, w3_ref, w2_ref, o_ref):
    x = x_ref[...]
    h = (_rms(x, gpre_ref[...]) * (1.0 + mod_ref[0, 4:5, :]) + mod_ref[0, 3:4, :]).astype(BF16)
    a = _dot(h, w1_ref[...])
    a = (a * _sigmoid(a)) * _dot(h, w3_ref[...])
    f = _dot(a.astype(BF16), w2_ref[...])
    o_ref[...] = x + mod_ref[0, 5:6, :] * _rms(f, gpost_ref[...])


def _ffn_dense(x2, mod, g_pre, g_post, w1, w3, w2, tiles_per_mod, tm):
    m, d = x2.shape
    return pl.pallas_call(
        _ffn_dense_kernel,
        grid=(m // tm,),
        in_specs=[pl.BlockSpec((tm, d), lambda i: (i, 0)),
                  pl.BlockSpec((1, 8, d), lambda i: (i // tiles_per_mod, 0, 0)),
                  _const_spec((1, d)), _const_spec((1, d)),
                  _const_spec(w1.shape), _const_spec(w3.shape), _const_spec(w2.shape)],
        out_specs=pl.BlockSpec((tm, d), lambda i: (i, 0)),
        out_shape=jax.ShapeDtypeStruct((m, d), F32),
        compiler_params=_cparams("arbitrary"),
        name="ffn_dense",
    )(x2, mod, g_pre.reshape(1, d), g_post.reshape(1, d), w1, w3, w2)


def _route_kernel(x_ref, mod_ref, gpre_ref, r_ref, h_ref, idx_ref, wt_ref, cnt_ref, tri_ref):
    tm = x_ref.shape[0]

    @pl.when(pl.program_id(0) == 0)
    def _():
        cnt_ref[...] = jnp.zeros_like(cnt_ref)
        earlier = lax.broadcasted_iota(jnp.int32, (tm, tm), 1) < lax.broadcasted_iota(jnp.int32, (tm, tm), 0)
        tri_ref[...] = jnp.where(earlier, 1.0, 0.0).astype(BF16)

    h = _rms(x_ref[...], gpre_ref[...]) * (1.0 + mod_ref[0, 4:5, :]) + mod_ref[0, 3:4, :]
    h_hi = h.astype(BF16)
    h_ref[...] = h_hi
    h_lo = (h - h_hi.astype(F32)).astype(BF16)
    a = _dot(h_hi, r_ref[...])
    logits = a + pltpu.roll(a, 128 - N_EXPERTS, 1) + _dot(h_lo, r_ref[...])
    lane = lax.broadcasted_iota(jnp.int32, logits.shape, 1)
    logits = jnp.where(lane < N_EXPERTS, logits, 2.0 * NEG_INF)
    m0 = logits.max(axis=-1, keepdims=True)
    i0 = jnp.where(logits == m0, lane, 128).min(axis=-1, keepdims=True)
    rest = jnp.where(lane == i0, NEG_INF, logits)
    m1 = rest.max(axis=-1, keepdims=True)
    i1 = jnp.where(rest == m1, lane, 128).min(axis=-1, keepdims=True)
    e = jnp.exp(m1 - m0)
    w0 = 1.0 / (1.0 + e)
    wt_ref[...] = jnp.where(lane == 0, w0, jnp.where(lane == 1, e * w0, 0.0))
    oh0 = jnp.where(lane == i0, 1.0, 0.0)
    oh1 = jnp.where(lane == i1, 1.0, 0.0)
    both = oh0 + oh1
    seen = _dot(tri_ref[...], both.astype(BF16)) + cnt_ref[0:1, :]
    rank0 = (oh0 * seen).sum(axis=-1, keepdims=True).astype(jnp.int32)
    rank1 = (oh1 * seen).sum(axis=-1, keepdims=True).astype(jnp.int32)
    cnt_ref[0:1, :] = cnt_ref[0:1, :] + both.sum(axis=0, keepdims=True)
    idx_ref[...] = jnp.where(lane == 0, i0, jnp.where(lane == 1, i1, jnp.where(lane == 2, rank0,
                                                                               jnp.where(lane == 3, rank1, 0))))


def _route(x2, mod, g_pre, router, tiles_per_mod, tm):
    m, d = x2.shape
    r_hi = router.astype(BF16)
    r_lo = (router.astype(F32) - r_hi.astype(F32)).astype(BF16)
    r_pad = jnp.pad(jnp.concatenate([r_hi, r_lo], axis=1), ((0, 0), (0, 128 - 2 * N_EXPERTS)))
    row = lambda i: (i, 0)
    return pl.pallas_call(
        _route_kernel,
        grid=(m // tm,),
        in_specs=[pl.BlockSpec((tm, d), row),
                  pl.BlockSpec((1, 8, d), lambda i: (i // tiles_per_mod, 0, 0)),
                  _const_spec((1, d)), _const_spec((d, 128))],
        out_specs=[pl.BlockSpec((tm, d), row), pl.BlockSpec((tm, 128), row), pl.BlockSpec((tm, 128), row),
                   pl.BlockSpec((8, 128), lambda i: (0, 0))],
        out_shape=[jax.ShapeDtypeStruct((m, d), BF16), jax.ShapeDtypeStruct((m, 128), jnp.int32),
                   jax.ShapeDtypeStruct((m, 128), F32), jax.ShapeDtypeStruct((8, 128), F32)],
        scratch_shapes=[pltpu.VMEM((tm, tm), BF16)],
        compiler_params=_cparams("arbitrary"),
        name="moe_route",
    )(x2, mod, g_pre.reshape(1, d), r_pad)


def _ffn_grouped_kernel(te_ref, nv_ref, h_ref, w1_ref, w3_ref, w2_ref, *rest, tile_off):
    o_ref, acc_ref = rest[-2:]
    i, j = pl.program_id(0), pl.program_id(1)
    last = pl.num_programs(1) - 1

    @pl.when(j == 0)
    def _():
        acc_ref[...] = jnp.zeros_like(acc_ref)

    @pl.when(tile_off + i < nv_ref[0])
    def _():
        h = h_ref[...]
        a = _dot(h, w1_ref[0].astype(BF16))
        a = (a * _sigmoid(a)) * _dot(h, w3_ref[0].astype(BF16))
        acc_ref[...] += _dot(a.astype(BF16), w2_ref[0].astype(BF16))

    @pl.when(j == last)
    def _():
        o_ref[...] = acc_ref[...].astype(BF16)


def _ffn_grouped(h_chunk, tile_expert, n_valid, w1, w3, w2, prev, tile_off, total_rows, tm, fc):
    r, d = h_chunk.shape
    f = w1.shape[2]
    nj = f // fc

    def col(i, j, nv):
        return jnp.where(tile_off + i < nv[0], j, nj - 1)

    in_specs = [pl.BlockSpec((tm, d), lambda i, j, te, nv: (i, 0)),
                pl.BlockSpec((1, d, fc), lambda i, j, te, nv: (te[tile_off + i], 0, col(i, j, nv))),
                pl.BlockSpec((1, d, fc), lambda i, j, te, nv: (te[tile_off + i], 0, col(i, j, nv))),
                pl.BlockSpec((1, fc, d), lambda i, j, te, nv: (te[tile_off + i], col(i, j, nv), 0))]
    args = [tile_expert, n_valid, h_chunk, w1, w3, w2]
    aliases = {}
    if prev is not None:
        in_specs.append(pl.BlockSpec(memory_space=pl.ANY))
        aliases = {len(args): 0}
        args.append(prev)
    grid_spec = pltpu.PrefetchScalarGridSpec(
        num_scalar_prefetch=2,
        grid=(r // tm, nj),
        in_specs=in_specs,
        out_specs=pl.BlockSpec((tm, d), lambda i, j, te, nv: (tile_off + i, 0)),
        scratch_shapes=[pltpu.VMEM((tm, d), F32)],
    )
    return pl.pallas_call(
        functools.partial(_ffn_grouped_kernel, tile_off=tile_off),
        grid_spec=grid_spec,
        out_shape=jax.ShapeDtypeStruct((total_rows, d), BF16),
        input_output_aliases=aliases,
        compiler_params=_cparams("arbitrary", "arbitrary"),
        name="ffn_grouped",
    )(*args)


def _combine_kernel(o0_ref, o1_ref, wt_ref, x_ref, mod_ref, gpost_ref, o_ref):
    y = wt_ref[:, 0:1] * o0_ref[...].astype(F32) + wt_ref[:, 1:2] * o1_ref[...].astype(F32)
    o_ref[...] = x_ref[...] + mod_ref[0, 5:6, :] * _rms(y, gpost_ref[...])


def _combine(o0, o1, wts, x2, mod, g_post, tiles_per_mod, tm):
    m, d = x2.shape
    row = lambda i: (i, 0)
    return pl.pallas_call(
        _combine_kernel,
        grid=(m // tm,),
        in_specs=[pl.BlockSpec((tm, d), row), pl.BlockSpec((tm, d), row), pl.BlockSpec((tm, 128), row),
                  pl.BlockSpec((tm, d), row),
                  pl.BlockSpec((1, 8, d), lambda i: (i // tiles_per_mod, 0, 0)),
                  _const_spec((1, d))],
        out_specs=pl.BlockSpec((tm, d), row),
        out_shape=jax.ShapeDtypeStruct((m, d), F32),
        compiler_params=_cparams("arbitrary"),
        name="moe_combine",
    )(o0, o1, wts, x2, mod, g_post.reshape(1, d))


def _moe(x2, mod, g_pre, g_post, router, w1, w3, w2, tiles_per_mod, tm):
    m, d = x2.shape
    h, idx, wts, cnt = _route(x2, mod, g_pre, router, tiles_per_mod, tm)
    gt = MOE_GROUP_TILE
    idx_t = idx[:, :4].T
    expert, rank = idx_t[:2], idx_t[2:]
    counts = cnt[0, :N_EXPERTS].astype(jnp.int32)
    padded = (counts + gt - 1) // gt * gt
    ends = jnp.cumsum(padded)
    starts = ends - padded
    pos = rank + sum(jnp.where(expert == e, starts[e], 0) for e in range(N_EXPERTS))
    n_tiles = (2 * m) // gt + N_EXPERTS
    token = jnp.tile(jnp.arange(m, dtype=jnp.int32), 2)
    src = jnp.zeros((n_tiles * gt,), jnp.int32).at[pos.reshape(-1)].set(token, unique_indices=True,
                                                                        mode='promise_in_bounds')
    tile_start = jnp.arange(n_tiles, dtype=jnp.int32) * gt
    tile_expert = jnp.minimum((ends[None, :] <= tile_start[:, None]).sum(axis=1), N_EXPERTS - 1).astype(jnp.int32)
    n_valid = (ends[-1] // gt).astype(jnp.int32).reshape(1)
    rows = lambda a, i: a.at[i].get(mode='promise_in_bounds')
    n_chunks = math.gcd(n_tiles, MOE_DISPATCH_CHUNKS)
    chunk_tiles = n_tiles // n_chunks
    out = None
    for ci in range(n_chunks):
        src_c = src[ci * chunk_tiles * gt:(ci + 1) * chunk_tiles * gt]
        out = _ffn_grouped(rows(h, src_c), tile_expert, n_valid, w1, w3, w2, out, ci * chunk_tiles, n_tiles * gt,
                           gt, MOE_F_CHUNK)
    return _combine(rows(out, pos[0]), rows(out, pos[1]), wts, x2, mod, g_post, tiles_per_mod, tm)


def _z_weights(w_in):
    d = w_in.shape[0]
    hw = N_HEADS * HEAD_DIM
    kvw = SWA_KV_HEADS * HEAD_DIM
    sizes = (hw, hw, hw, hw, kvw, kvw, BW, BW, BW, Z_GATES_W)
    qa, ka, va, qd, kd, vd, gu, gv, su, gates = jnp.split(w_in, np.cumsum(sizes)[:-1].tolist(), axis=1)
    rep = N_HEADS // SWA_KV_HEADS
    dup = lambda w: jnp.repeat(w.reshape(d, SWA_KV_HEADS, 1, HEAD_DIM), rep, axis=2).reshape(d, hw)
    return jnp.concatenate([gates, qa, ka, va, qd, dup(kd), dup(vd), gu, gv, su], axis=1).astype(BF16)


def kernel(x, c, ctx, c_ctx, w_mod, b_mod, g_pre_mix, g_post_mix, g_pre_ffn, g_post_ffn, w_in, na_rpb, swa_sink, gmlp_ln_g, gmlp_ln_b, gmlp_ws, gmlp_bs, s5_a_re, s5_a_im, s5_log_dt, s5_b_re, s5_b_im, s5_c_re, s5_c_im, s5_d, s5_glu_w, s5_glu_b, w_branch, w_out, ffn_w1, ffn_w3, ffn_w2, moe_router, moe_w1, moe_w3, moe_w2):
    bsz, s, d = x.shape
    l = ctx.shape[1]
    depth = w_mod.shape[0]
    tm = 512
    rows = s // GRID_W

    c_all = jnp.zeros((8, d), F32).at[:bsz].set(c).at[bsz].set(c_ctx)
    mods = _modulation(c_all, w_mod, b_mod).reshape(depth, 8, 6, d)
    mods = jnp.pad(mods, ((0, 0), (0, 0), (0, 2), (0, 0)))
    rope_tabs = _rope_tables(s)
    swa_bias = _swa_bias(s)
    w_z_all = jax.vmap(_z_weights)(w_in)
    na_bias_all = jax.vmap(functools.partial(_na_bias, rows=rows))(na_rpb)
    s5_all = jax.vmap(_s5_matrices)(s5_a_re, s5_a_im, s5_log_dt, s5_b_re, s5_b_im, s5_c_re, s5_c_im, s5_d)
    wb_all, wo_all, glu_all = w_branch.astype(BF16), w_out.astype(BF16), s5_glu_w.astype(BF16)
    sink_all = jnp.repeat(swa_sink.astype(F32) * LOG2E, ATT_TQ, axis=1).reshape(depth, 1, N_HEADS * ATT_TQ)

    x2 = x.reshape(bsz * s, d)
    c2 = ctx.reshape(bsz * l, d)
    for i in range(depth):
        ctx_out = i < depth - 1
        mod_x = mods[i, :bsz]
        mod_c = mods[i, bsz:bsz + 1]
        w_z = w_z_all[i]
        zx, su_x = _project(x2, mod_x, g_pre_mix[i], w_z, s // tm, rope_tabs, tm)
        zc, su_c = _project(c2, mod_c, g_pre_mix[i], w_z, (bsz * l) // tm, None, tm)
        zx3 = zx.reshape(bsz, s, ZW)
        zc3 = zc.reshape(bsz, l, ZW)
        sink_rows = sink_all[i]

        nq = s // ATT_TQ
        o_a = _local_attention(zx3, zc3, Z_QA, Z_KA, Z_VA, na_bias_all[i], None, ATT_TQ,
                               lambda qi, j: jnp.clip(qi - 1, 0, nq - 3) + j)
        o_d = _local_attention(zx3, zc3, Z_QD, Z_KD, Z_VD, swa_bias, sink_rows, SWA_WINDOW,
                               functools.partial(_swa_key_index, n_blocks=s // SWA_WINDOW))
        o_b = _gmlp(zx, gmlp_ln_g[i], gmlp_ln_b[i], gmlp_ws[i], gmlp_bs[i], tm)
        s5_mats = tuple(v[i] for v in s5_all)
        y_c, y_x = _s5_mix(su_c.reshape(bsz, l, BW), su_x.reshape(bsz, s, BW), s5_mats, ctx_out)

        wb, wo, glu_w = wb_all[i], wo_all[i], glu_all[i]
        x2 = _merge(zx, o_a.reshape(bsz * s, BW), o_b, y_x.reshape(bsz * s, BW), o_d.reshape(bsz * s, BW),
                    x2, mod_x, g_post_mix[i], wb, wo, glu_w, s5_glu_b[i], s // tm, tm)
        if ctx_out:
            o_a_c = _ctx_attention(zc3, Z_QA, Z_KA, Z_VA, None)
            o_d_c = _ctx_attention(zc3, Z_QD, Z_KD, Z_VD, sink_rows)
            o_b_c = _gmlp(zc, gmlp_ln_g[i], gmlp_ln_b[i], gmlp_ws[i], gmlp_bs[i], tm)
            c2 = _merge(zc, o_a_c.reshape(bsz * l, BW), o_b_c, y_c.reshape(bsz * l, BW), o_d_c.reshape(bsz * l, BW),
                        c2, mod_c, g_post_mix[i], wb, wo, glu_w, s5_glu_b[i], (bsz * l) // tm, tm)

        j = i // 2
        if i % 2 == 0:
            w1, w3, w2 = ffn_w1[j].astype(BF16), ffn_w3[j].astype(BF16), ffn_w2[j].astype(BF16)
            x2 = _ffn_dense(x2, mod_x, g_pre_ffn[i], g_post_ffn[i], w1, w3, w2, s // tm, tm)
            if ctx_out:
                c2 = _ffn_dense(c2, mod_c, g_pre_ffn[i], g_post_ffn[i], w1, w3, w2, (bsz * l) // tm, tm)
        else:
            w1, w3, w2 = moe_w1[j], moe_w3[j], moe_w2[j]
            x2 = _moe(x2, mod_x, g_pre_ffn[i], g_post_ffn[i], moe_router[j], w1, w3, w2, s // tm, tm)
            if ctx_out:
                c2 = _moe(c2, mod_c, g_pre_ffn[i], g_post_ffn[i], moe_router[j], w1, w3, w2, (bsz * l) // tm, tm)
    return x2.reshape(bsz, s, d)
```

```python
import functools
import math

import numpy as np
import jax
import jax.numpy as jnp
from jax import lax
from jax.experimental import pallas as pl
from jax.experimental.pallas import tpu as pltpu

F32 = jnp.float32
BF16 = jnp.bfloat16
HIGHEST = lax.Precision.HIGHEST

GRID_W = 64
HEAD_DIM = 64
N_HEADS = 4
NA_ROWS = 8
NA_COLS = 16
SWA_KV_HEADS = 2
SWA_WINDOW = 128
GMLP_GROUPS = 4
GMLP_CHUNK = 128
S5_GROUP = 16
S5_GROUPS = 16
S5_STATE = 64
N_EXPERTS = 8
ROPE_BASE = 10000.0
EPS = 1e-6
NEG_INF = -1e30
LOG2E = math.log2(math.e)
Q_SCALE = HEAD_DIM ** -0.5 * LOG2E

BW = 256
S5_T = 16
S5_ROWS = 8
Z_GATES_W = 4096
(Z_QA, Z_KA, Z_VA, Z_QD, Z_KD, Z_VD, Z_GU, Z_GV, Z_SU) = range(Z_GATES_W // BW, Z_GATES_W // BW + 9)
ZW = Z_GATES_W + 9 * BW

V7X_VMEM_LIMIT = 56 * 1024 * 1024
ATT_TQ = 256
ONES_ROWS = 16
ATT_CHAIN_HEADS = 2
ATT_TILES_PER_STEP = 4
SWA_LOCAL_BLOCKS = ATT_TQ // SWA_WINDOW + 2
MOE_GROUP_TILE = 1024
MOE_F_CHUNK = 512
MOE_DISPATCH_CHUNKS = 4


def _cparams(*sem):
    return pltpu.CompilerParams(dimension_semantics=sem, vmem_limit_bytes=V7X_VMEM_LIMIT)


def _const_spec(shape):
    nd = len(shape)
    return pl.BlockSpec(shape, lambda *_: (0,) * nd, pipeline_mode=pl.Buffered(1))


def _dot(a, b):
    return jnp.dot(a, b, preferred_element_type=F32)


def _dot_nt(a, b):
    return lax.dot_general(a, b, (((1,), (1,)), ((), ())), preferred_element_type=F32)


def _sigmoid(x):
    return 0.5 * jnp.tanh(0.5 * x) + 0.5


def _rms(x, g):
    return x * lax.rsqrt(jnp.mean(x * x, axis=-1, keepdims=True) + EPS) * g


def _mod_kernel(c_ref, w_ref, b_ref, o_ref):
    c = c_ref[...]
    a = c * _sigmoid(c)
    o_ref[0] = jnp.dot(a, w_ref[0], preferred_element_type=F32, precision=HIGHEST) + b_ref[0]


def _modulation(c_all, w_mod, b_mod):
    depth, d, n = w_mod.shape
    tn = 1536
    return pl.pallas_call(
        _mod_kernel,
        grid=(depth, n // tn),
        in_specs=[pl.BlockSpec((8, d), lambda l, j: (0, 0)),
                  pl.BlockSpec((1, d, tn), lambda l, j: (l, 0, j)),
                  pl.BlockSpec((1, 1, tn), lambda l, j: (l, 0, j))],
        out_specs=pl.BlockSpec((1, 8, tn), lambda l, j: (l, 0, j)),
        out_shape=jax.ShapeDtypeStruct((depth, 8, n), F32),
        compiler_params=_cparams("arbitrary", "arbitrary"),
        name="modulation",
    )(c_all, w_mod, b_mod.reshape(depth, 1, n))


def _proj_kernel(*refs, rope, n_chunk):
    if rope:
        x_ref, mod_ref, g_ref, w_ref, cos_ref, sin_ref, o_ref, su_ref = refs
    else:
        x_ref, mod_ref, g_ref, w_ref, o_ref, su_ref = refs
    h = _rms(x_ref[...], g_ref[...]) * (1.0 + mod_ref[0, 1:2, :]) + mod_ref[0, 0:1, :]
    h = h.astype(BF16)
    rope_blocks = (Z_QD, Z_KD) if rope else ()
    for j in range(ZW // n_chunk):
        lo = j * n_chunk
        r = _dot(h, w_ref[:, lo:lo + n_chunk])
        blocks = range(lo // BW, (lo + n_chunk) // BW)
        if Z_SU in blocks:
            su_ref[...] = r[:, Z_SU * BW - lo:(Z_SU + 1) * BW - lo]
        if not any(b in rope_blocks or b in (Z_QA, Z_QD) for b in blocks):
            o_ref[:, lo:lo + n_chunk] = r.astype(BF16)
            continue
        for b in blocks:
            t = r[:, b * BW - lo:(b + 1) * BW - lo]
            if b in rope_blocks:
                lane = lax.broadcasted_iota(jnp.int32, (1, BW), 1)
                low_half = (lane % (HEAD_DIM // 2)) < (HEAD_DIM // 4)
                partner = jnp.where(low_half, pltpu.roll(t, BW - HEAD_DIM // 4, 1),
                                    pltpu.roll(t, HEAD_DIM // 4, 1))
                t = t * cos_ref[...] + partner * sin_ref[...]
            if b in (Z_QA, Z_QD):
                t = t * Q_SCALE
            o_ref[:, b * BW:(b + 1) * BW] = t.astype(BF16)


def _project(x2, mod, g, w_z, tiles_per_mod, rope_tabs, tm):
    m, d = x2.shape
    rope = rope_tabs is not None
    in_specs = [pl.BlockSpec((tm, d), lambda i: (i, 0)),
                pl.BlockSpec((1, 8, d), lambda i: (i // tiles_per_mod, 0, 0)),
                _const_spec((1, d)),
                _const_spec((d, ZW))]
    args = [x2, mod, g.reshape(1, d), w_z]
    if rope:
        n_rt = rope_tabs[0].shape[0] // tm
        in_specs += [pl.BlockSpec((tm, BW), lambda i: (i % n_rt, 0))] * 2
        args += list(rope_tabs)
    return pl.pallas_call(
        functools.partial(_proj_kernel, rope=rope, n_chunk=1280),
        grid=(m // tm,),
        in_specs=in_specs,
        out_specs=[pl.BlockSpec((tm, ZW), lambda i: (i, 0)), pl.BlockSpec((tm, BW), lambda i: (i, 0))],
        out_shape=[jax.ShapeDtypeStruct((m, ZW), BF16), jax.ShapeDtypeStruct((m, BW), F32)],
        compiler_params=_cparams("arbitrary"),
        name="project_in",
    )(*args)


def _attn_kernel(*refs, n_local, use_sink, n_tiles):
    per_tile = 1 + 2 * n_local + (1 if n_local else 0)
    tiles = [refs[t * per_tile:(t + 1) * per_tile] for t in range(n_tiles)]
    pos = n_tiles * per_tile
    kc_ref, vc_ref = refs[pos], refs[pos + 1]
    pos += 2
    sink_ref = None
    if use_sink:
        sink_ref = refs[pos]
        pos += 1
    o_ref = refs[pos]

    tq = tiles[0][0].shape[1]
    lane_head = lax.broadcasted_iota(jnp.int32, (1, BW), 1) // HEAD_DIM
    vc_t = vc_ref[0].T
    chains = [range(h0, h0 + ATT_CHAIN_HEADS) for h0 in range(0, N_HEADS, ATT_CHAIN_HEADS)]

    def score_stage(tile, heads):
        q = tile[0][0]
        cols = slice(heads[0] * tq, (heads[-1] + 1) * tq)
        q_heads = jnp.concatenate([jnp.where(lane_head == h, q, jnp.zeros_like(q)) for h in heads], axis=0)
        scores = []
        for j in range(n_local):
            k_ref = tile[1 + 2 * j]
            kb = k_ref.shape[1]
            scores.append(_dot_nt(k_ref[0], q_heads) + tile[-1][0, j * kb:(j + 1) * kb, cols])
        scores.append(_dot_nt(kc_ref[0], q_heads))
        return scores

    def softmax_value_stage(tile, heads, scores):
        cols = slice(heads[0] * tq, (heads[-1] + 1) * tq)
        v_t = [tile[2 + 2 * j][0].T for j in range(n_local)] + [vc_t]
        mx = scores[0].max(axis=0, keepdims=True)
        for s in scores[1:]:
            mx = jnp.maximum(mx, s.max(axis=0, keepdims=True))
        if use_sink:
            mx = jnp.maximum(mx, sink_ref[:, cols])
        o = [jnp.zeros((HEAD_DIM + ONES_ROWS, tq), F32) for _ in heads]
        for s, vt in zip(scores, v_t):
            p = jnp.exp2(s - mx).astype(BF16)
            ones = jnp.ones((ONES_ROWS, vt.shape[1]), BF16)
            for i, h in enumerate(heads):
                lhs = jnp.concatenate([vt[h * HEAD_DIM:(h + 1) * HEAD_DIM, :], ones], axis=0)
                o[i] = o[i] + _dot(lhs, p[:, i * tq:(i + 1) * tq])
        res = []
        for i, h in enumerate(heads):
            den = o[i][HEAD_DIM:HEAD_DIM + 1, :]
            if use_sink:
                den = den + jnp.exp2(sink_ref[:, h * tq:(h + 1) * tq] - mx[:, i * tq:(i + 1) * tq])
            res.append(o[i][:HEAD_DIM, :] / den)
        return res

    work = [(t, heads) for heads in chains for t in range(n_tiles)]
    outs = {}
    pending = score_stage(tiles[work[0][0]], work[0][1])
    for c, (t, heads) in enumerate(work):
        scores = pending
        if c + 1 < len(work):
            pending = score_stage(tiles[work[c + 1][0]], work[c + 1][1])
        outs[(t, heads[0])] = softmax_value_stage(tiles[t], heads, scores)
    for t in range(n_tiles):
        o_t = jnp.concatenate([r for heads in chains for r in outs[(t, heads[0])]], axis=0)
        o_ref[0, t * tq:(t + 1) * tq, :] = o_t.T.astype(BF16)


def _local_attention(z, z_c, q_col, k_col, v_col, bias, sink_rows, key_block, key_index):
    b, s, _ = z.shape
    l = z_c.shape[1]
    tq = ATT_TQ
    nq = s // tq
    nt = ATT_TILES_PER_STEP
    n_local = bias.shape[1] // key_block
    assert nq >= 3 and nq % nt == 0 and l == BW

    def pat(i):
        return jnp.where(i == 0, 0, jnp.where(i == nq - 1, 2, 1))

    in_specs, args = [], []
    for t in range(nt):
        tile = lambda i, t=t: i * nt + t
        in_specs.append(pl.BlockSpec((1, tq, BW), lambda bi, i, tile=tile: (bi, tile(i), q_col)))
        args.append(z)
        for j in range(n_local):
            for col in (k_col, v_col):
                in_specs.append(pl.BlockSpec((1, key_block, BW),
                                             lambda bi, i, tile=tile, j=j, col=col: (bi, key_index(tile(i), j), col)))
                args.append(z)
        in_specs.append(pl.BlockSpec((1,) + bias.shape[1:], lambda bi, i, tile=tile: (pat(tile(i)), 0, 0)))
        args.append(bias)
    in_specs += [pl.BlockSpec((1, l, BW), lambda bi, i: (bi, 0, k_col)),
                 pl.BlockSpec((1, l, BW), lambda bi, i: (bi, 0, v_col))]
    args += [z_c, z_c]
    if sink_rows is not None:
        in_specs.append(_const_spec(sink_rows.shape))
        args.append(sink_rows)
    return pl.pallas_call(
        functools.partial(_attn_kernel, n_local=n_local, use_sink=sink_rows is not None, n_tiles=nt),
        grid=(b, nq // nt),
        in_specs=in_specs,
        out_specs=pl.BlockSpec((1, nt * tq, BW), lambda bi, i: (bi, i, 0)),
        out_shape=jax.ShapeDtypeStruct((b, s, BW), BF16),
        compiler_params=_cparams("arbitrary", "arbitrary"),
        name="local_attention",
    )(*args)


def _ctx_attention(z_c, q_col, k_col, v_col, sink_rows):
    b, l, _ = z_c.shape
    in_specs = [pl.BlockSpec((1, l, BW), lambda bi, c=col: (bi, 0, c)) for col in (q_col, k_col, v_col)]
    args = [z_c, z_c, z_c]
    if sink_rows is not None:
        in_specs.append(_const_spec(sink_rows.shape))
        args.append(sink_rows)
    return pl.pallas_call(
        functools.partial(_attn_kernel, n_local=0, use_sink=sink_rows is not None, n_tiles=1),
        grid=(b,),
        in_specs=in_specs,
        out_specs=pl.BlockSpec((1, l, BW), lambda bi: (bi, 0, 0)),
        out_shape=jax.ShapeDtypeStruct((b, l, BW), BF16),
        compiler_params=_cparams("arbitrary"),
        name="ctx_attention",
    )(*args)


def _na_bias(rpb, rows):
    tile_rows = ATT_TQ // GRID_W
    nq = rows // tile_rows
    col = np.arange(GRID_W)
    cs = np.clip(col - NA_COLS // 2, 0, GRID_W - NA_COLS)[:, None]
    kc = col[None, :]
    sel_c = ((kc - col[:, None] + NA_COLS - 1)[None] == np.arange(2 * NA_COLS - 1)[:, None, None]) \
        & ((kc >= cs) & (kc < cs + NA_COLS))[None]
    sel_r = []
    for i in (0, 1, nq - 1):
        base = min(max(i - 1, 0), nq - 3)
        qr = (tile_rows * i + np.arange(tile_rows))[:, None]
        kr = (tile_rows * base + np.arange(3 * tile_rows))[None, :]
        rs = np.clip(qr - NA_ROWS // 2, 0, rows - NA_ROWS)
        sel_r.append(((kr - qr + NA_ROWS - 1)[None] == np.arange(2 * NA_ROWS - 1)[:, None, None])
                     & ((kr >= rs) & (kr < rs + NA_ROWS))[None])
    sel_r = np.stack(sel_r).astype(np.float32)
    sel_c = sel_c.astype(np.float32)
    t1 = jnp.einsum('paqk,hab->phqkb', sel_r, rpb.astype(F32), precision=HIGHEST)
    bias = jnp.einsum('phqkb,bcd->pkdhqc', t1, sel_c, precision=HIGHEST)
    valid = np.einsum('paqk,bcd->pkdqc', sel_r, sel_c) > 0.5
    mask = np.where(valid, 0.0, NEG_INF).astype(np.float32)[:, :, :, None]
    return (bias * LOG2E + mask).reshape(3, 3 * ATT_TQ, rpb.shape[0] * ATT_TQ)


def _swa_key_index(i, j, n_blocks):
    return jnp.clip(i * (ATT_TQ // SWA_WINDOW) - 1 + j, 0, n_blocks - 1)


def _swa_bias(s):
    nq = s // ATT_TQ
    tq = np.arange(ATT_TQ)[:, None]
    tk = np.arange(SWA_LOCAL_BLOCKS * SWA_WINDOW)[None, :]
    out = []
    for i in (0, 1, nq - 1):
        kpos = ATT_TQ * i - SWA_WINDOW + tk
        valid = (np.abs(kpos - (ATT_TQ * i + tq)) <= SWA_WINDOW) & (kpos >= 0) & (kpos < s)
        out.append(np.tile(np.where(valid, 0.0, NEG_INF).astype(np.float32).T, (1, N_HEADS)))
    return jnp.asarray(np.stack(out))


def _rope_tables(s):
    nq = HEAD_DIM // 4
    t = jnp.arange(s)
    inv = ROPE_BASE ** (-jnp.arange(nq, dtype=F32) / nq)
    ang_r = (t // GRID_W).astype(F32)[:, None] * inv[None, :]
    ang_c = (t % GRID_W).astype(F32)[:, None] * inv[None, :]
    cos = jnp.concatenate([jnp.cos(ang_r)] * 2 + [jnp.cos(ang_c)] * 2, axis=-1)
    sin = jnp.concatenate([-jnp.sin(ang_r), jnp.sin(ang_r), -jnp.sin(ang_c), jnp.sin(ang_c)], axis=-1)
    return jnp.tile(cos, (1, N_HEADS)), jnp.tile(sin, (1, N_HEADS))


def _gmlp_kernel(u_ref, v_ref, g_ref, b_ref, w_ref, bs_ref, o_ref):
    lane_grp = lax.broadcasted_iota(jnp.int32, (1, BW), 1) // (BW // GMLP_GROUPS)
    for c in range(u_ref.shape[0] // GMLP_CHUNK):
        rows = slice(c * GMLP_CHUNK, (c + 1) * GMLP_CHUNK)
        u = jax.nn.gelu(u_ref[rows, :].astype(F32))
        v = jax.nn.gelu(v_ref[rows, :].astype(F32))
        mu = jnp.mean(v, axis=-1, keepdims=True)
        vc = v - mu
        v = vc * lax.rsqrt(jnp.mean(vc * vc, axis=-1, keepdims=True) + EPS) * g_ref[...] + b_ref[...]
        v = v.astype(BF16)
        stack = jnp.concatenate([jnp.where(lane_grp == g, v, jnp.zeros_like(v)) for g in range(GMLP_GROUPS)], axis=0)
        sg = _dot(w_ref[...], stack) + bs_ref[...]
        o_ref[rows, :] = (u * sg).astype(BF16)


def _gmlp(z2, ln_g, ln_b, ws, bs, tm):
    m = z2.shape[0]
    w_cat = jnp.concatenate([ws[g] for g in range(GMLP_GROUPS)], axis=1).astype(BF16)
    bs_l = jnp.repeat(bs.T.astype(F32), BW // GMLP_GROUPS, axis=1)
    return pl.pallas_call(
        _gmlp_kernel,
        grid=(m // tm,),
        in_specs=[pl.BlockSpec((tm, BW), lambda i: (i, Z_GU)),
                  pl.BlockSpec((tm, BW), lambda i: (i, Z_GV)),
                  _const_spec((1, BW)), _const_spec((1, BW)),
                  _const_spec(w_cat.shape), _const_spec(bs_l.shape)],
        out_specs=pl.BlockSpec((tm, BW), lambda i: (i, 0)),
        out_shape=jax.ShapeDtypeStruct((m, BW), BF16),
        compiler_params=_cparams("arbitrary"),
        name="gmlp",
    )(z2, z2, ln_g.reshape(1, BW).astype(F32), ln_b.reshape(1, BW).astype(F32), w_cat, bs_l)


def _s5_kernel(su_ref, suc_ref, wend_ref, toep_ref, wc_ref, a16_ref, *rest, ctx_out):
    if ctx_out:
        y_ref, yc_ref, ut_ref, utc_ref, up_ref, st_ref = rest
    else:
        y_ref, ut_ref, utc_ref, up_ref, st_ref = rest
    nj, nc = su_ref.shape[1] // S5_T, suc_ref.shape[1] // S5_T
    lanes = su_ref.shape[2]
    ng = lanes // S5_GROUP
    hw = ng * S5_STATE
    cpad = utc_ref.shape[1]

    def grp(g, t):
        return slice(g * BW + t * S5_GROUP, g * BW + (t + 1) * S5_GROUP)

    for t in range(S5_T):
        at = su_ref[0, pl.ds(t, nj, stride=S5_T), :].T
        ct = jnp.concatenate([suc_ref[0, pl.ds(t, nc, stride=S5_T), :],
                              jnp.zeros((cpad - nc, lanes), F32)], axis=0).T
        for g in range(ng):
            ut_ref[grp(g, t), :] = at[g * S5_GROUP:(g + 1) * S5_GROUP, :]
            utc_ref[grp(g, t), :] = ct[g * S5_GROUP:(g + 1) * S5_GROUP, :]

    for p in range(ng // 2):
        blocks = []
        for g in (2 * p, 2 * p + 1):
            xt = ut_ref[g * BW:(g + 1) * BW, :].T
            ct = utc_ref[g * BW:(g + 1) * BW, :].T[:nc]
            blocks.append(jnp.concatenate([ct, xt], axis=0))
        up = jnp.concatenate(blocks, axis=1).astype(BF16)
        up_ref[:, p * 2 * BW:(p + 1) * 2 * BW] = up
        e = _dot(up, wend_ref[p])
        for k in range(4):
            st_ref[:, k * hw + p * 128:k * hw + (p + 1) * 128] = e[:, k * 128:(k + 1) * 128]

    n_tiles, n_ctiles = (nc + nj) // 8, nc // 8
    a = [a16_ref[k:k + 1, :] for k in range(4)]

    def tile_step(k, carry):
        fr, fi, rr, ri = carry
        kf = pl.multiple_of(k * 8, 8)
        kr = jnp.where(k < n_ctiles, n_ctiles - 1 - k, n_tiles - 1 - (k - n_ctiles))
        kr = pl.multiple_of(kr * 8, 8)
        ef_r, ef_i = st_ref[pl.ds(kf, 8), 0:hw], st_ref[pl.ds(kf, 8), hw:2 * hw]
        er_r, er_i = st_ref[pl.ds(kr, 8), 2 * hw:3 * hw], st_ref[pl.ds(kr, 8), 3 * hw:4 * hw]
        hf_r, hf_i, hr_r, hr_i = [], [], [None] * 8, [None] * 8
        for r in range(8):
            hf_r.append(fr)
            hf_i.append(fi)
            fr, fi = a[0] * fr - a[1] * fi + ef_r[r:r + 1], a[0] * fi + a[1] * fr + ef_i[r:r + 1]
            q = 7 - r
            hr_r[q], hr_i[q] = rr, ri
            rr, ri = a[2] * rr - a[3] * ri + er_r[q:q + 1], a[2] * ri + a[3] * rr + er_i[q:q + 1]
        st_ref[pl.ds(kf, 8), 0:hw] = jnp.concatenate(hf_r, axis=0)
        st_ref[pl.ds(kf, 8), hw:2 * hw] = jnp.concatenate(hf_i, axis=0)
        st_ref[pl.ds(kr, 8), 2 * hw:3 * hw] = jnp.concatenate(hr_r, axis=0)
        st_ref[pl.ds(kr, 8), 3 * hw:4 * hw] = jnp.concatenate(hr_i, axis=0)
        return fr, fi, rr, ri

    zero = jnp.zeros((1, hw), F32)
    lax.fori_loop(0, n_tiles, tile_step, (zero, zero, zero, zero))

    for p in range(ng // 2):
        h = jnp.concatenate([st_ref[:, k * hw + p * 128:k * hw + (p + 1) * 128] for k in range(4)], axis=1)
        y = _dot(up_ref[:, p * 2 * BW:(p + 1) * 2 * BW], toep_ref[p]) + _dot(h.astype(BF16), wc_ref[p])
        for gg in range(2):
            g = 2 * p + gg
            yg = y[:, gg * BW:(gg + 1) * BW]
            ut_ref[g * BW:(g + 1) * BW, :] = yg[nc:, :].T
            if ctx_out:
                utc_ref[g * BW:(g + 1) * BW, :] = jnp.concatenate(
                    [yg[:nc, :], jnp.zeros((cpad - nc, BW), F32)], axis=0).T

    for t in range(S5_T):
        z = jnp.concatenate([ut_ref[grp(g, t), :] for g in range(ng)], axis=0)
        y_ref[0, pl.ds(t, nj, stride=S5_T), :] = z.T
        if ctx_out:
            zc = jnp.concatenate([utc_ref[grp(g, t), :] for g in range(ng)], axis=0)
            yc_ref[0, pl.ds(t, nc, stride=S5_T), :] = zc.T[:nc]


def _s5_matrices(a_re, a_im, log_dt, b_re, b_im, c_re, c_im, d):
    g, p, c, t = S5_GROUPS, S5_STATE, S5_GROUP, S5_T
    tc = t * c
    lam_re = jnp.minimum(a_re.astype(F32), -1e-4)
    lam_im = a_im.astype(F32)
    dt = jnp.exp(log_dt.astype(F32))[..., None]
    lr, li = lam_re * dt, lam_im * dt

    def a_pow(n, x_re, x_im):
        mag = jnp.exp(n * x_re)
        return mag * jnp.cos(n * x_im), mag * jnp.sin(n * x_im)

    ab_re, ab_im = a_pow(1.0, lr, li)
    den = lam_re * lam_re + lam_im * lam_im
    k_re = ((ab_re - 1.0) * lam_re + ab_im * lam_im) / den
    k_im = (ab_im * lam_re - (ab_re - 1.0) * lam_im) / den
    br, bi = b_re.astype(F32), b_im.astype(F32)
    bb_re = k_re[..., None] * br - k_im[..., None] * bi
    bb_im = k_re[..., None] * bi + k_im[..., None] * br
    bbt_re, bbt_im = jnp.swapaxes(bb_re, 2, 3), jnp.swapaxes(bb_im, 2, 3)
    ct_re = jnp.swapaxes(c_re.astype(F32), 2, 3)
    ct_im = jnp.swapaxes(c_im.astype(F32), 2, 3)

    n_up = np.repeat(np.arange(t + 1, dtype=np.float32), c)
    ca = []
    for dirn, n_lane in ((0, n_up), (1, t - n_up)):
        pr, pi = a_pow(n_lane[None, None, :], lr[dirn][:, :, None], li[dirn][:, :, None])
        cr_l = jnp.tile(ct_re[dirn], (1, 1, t + 1))
        ci_l = jnp.tile(ct_im[dirn], (1, 1, t + 1))
        ca.append(jnp.concatenate([cr_l * pr - ci_l * pi, -(cr_l * pi + ci_l * pr)], axis=1))
    wc_f, wc_r = ca[0][:, :, c:], ca[1][:, :, :tc]
    bb_cat = jnp.concatenate([bbt_re, bbt_im], axis=-1)
    lag_f = jnp.einsum('gik,gkl->gil', bb_cat[0], ca[0][:, :, :tc], precision=HIGHEST)
    lag_r = jnp.einsum('gik,gkl->gil', bb_cat[1], ca[1][:, :, c:], precision=HIGHEST)
    zeros = jnp.zeros_like(lag_f)
    ext_f = jnp.concatenate([zeros, lag_f], axis=-1)
    ext_r = jnp.concatenate([lag_r, zeros], axis=-1)
    toep = jnp.stack([ext_f[:, :, tc - c * s:2 * tc - c * s] + ext_r[:, :, c * (t - 1 - s):c * (t - 1 - s) + tc]
                      for s in range(t)], axis=1)
    skip = jnp.eye(tc, dtype=F32)[None] * jnp.tile(d.astype(F32).reshape(g, 1, c), (1, 1, t))
    toep = toep.reshape(g, tc, tc) + skip

    n_row = np.repeat(np.arange(t, dtype=np.float32), c)[None, :, None]
    ends = []
    for dirn, n in ((0, t - 1 - n_row), (1, n_row)):
        pr, pi = a_pow(n, lr[dirn][:, None, :], li[dirn][:, None, :])
        b_r, b_i = jnp.tile(bbt_re[dirn], (1, t, 1)), jnp.tile(bbt_im[dirn], (1, t, 1))
        ends += [pr * b_r - pi * b_i, pr * b_i + pi * b_r]
    wend = jnp.concatenate(ends, axis=-1)

    half = g // 2
    z2 = jnp.zeros((half, tc, tc), F32)
    toep_p = jnp.concatenate([jnp.concatenate([toep[0::2], z2], axis=-1),
                              jnp.concatenate([z2, toep[1::2]], axis=-1)], axis=1)
    w4 = wend.reshape(g, tc, 4, p)
    z4 = jnp.zeros((half, tc, 4, p), F32)
    wend_p = jnp.concatenate([jnp.stack([w4[0::2], z4], axis=3).reshape(half, tc, 8 * p),
                              jnp.stack([z4, w4[1::2]], axis=3).reshape(half, tc, 8 * p)], axis=1)
    wc4 = jnp.concatenate([wc_f, wc_r], axis=1).reshape(g, 4, p, tc)
    zc = jnp.zeros((half, 4, p, tc), F32)
    wc_p = jnp.stack([jnp.concatenate([wc4[0::2], zc], axis=-1),
                      jnp.concatenate([zc, wc4[1::2]], axis=-1)], axis=2).reshape(half, 8 * p, 2 * tc)
    a16 = jnp.stack([v for dirn in (0, 1) for v in a_pow(float(t), lr[dirn], li[dirn])], axis=0)
    return toep_p.astype(BF16), wend_p.astype(BF16), wc_p.astype(BF16), a16.reshape(4, g * p)


def _s5_mix(su_c, su_x, mats, ctx_out):
    toep_p, wend_p, wc_p, a16 = mats
    b, l, _ = su_c.shape
    s = su_x.shape[1]
    n_rows = (l + s) // S5_T
    halves = 2
    lanes = BW // halves
    ng = lanes // S5_GROUP
    pw = 2 * BW
    assert (s // S5_T) % 128 == 0 and (l // S5_T) % 8 == 0 and l // S5_T <= 128
    wspec = pl.BlockSpec((ng // 2, pw, pw), lambda bi, h: (h, 0, 0))
    out_specs = [pl.BlockSpec((1, s, lanes), lambda bi, h: (bi, 0, h))]
    out_shape = [jax.ShapeDtypeStruct((b, s, BW), F32)]
    if ctx_out:
        out_specs.append(pl.BlockSpec((1, l, lanes), lambda bi, h: (bi, 0, h)))
        out_shape.append(jax.ShapeDtypeStruct((b, l, BW), F32))
    res = pl.pallas_call(
        functools.partial(_s5_kernel, ctx_out=ctx_out),
        grid=(b, halves),
        in_specs=[pl.BlockSpec((1, s, lanes), lambda bi, h: (bi, 0, h)),
                  pl.BlockSpec((1, l, lanes), lambda bi, h: (bi, 0, h)),
                  wspec, wspec, wspec,
                  pl.BlockSpec((4, ng * S5_STATE), lambda bi, h: (0, h))],
        out_specs=out_specs,
        out_shape=out_shape,
        scratch_shapes=[pltpu.VMEM((ng * BW, s // S5_T), F32), pltpu.VMEM((ng * BW, 128), F32),
                        pltpu.VMEM((n_rows, ng * BW), BF16), pltpu.VMEM((n_rows, 4 * ng * S5_STATE), F32)],
        compiler_params=_cparams("arbitrary", "arbitrary"),
        name="s5_mixer",
    )(su_x, su_c, wend_p, toep_p, wc_p, a16)
    return (res[1] if ctx_out else None), res[0]


def _merge_kernel(zg_ref, oa_ref, ob_ref, ys_ref, od_ref, x_ref, mod_ref, g_ref, wb_ref, wo_ref, gw_ref, gb_ref,
                  o_ref):
    d = x_ref.shape[1]
    y = jax.nn.gelu(ys_ref[...].astype(F32))
    oc = (y * _sigmoid(_dot(y.astype(BF16), gw_ref[...]) + gb_ref[...])).astype(BF16)
    outs = (oa_ref[...], ob_ref[...], oc, od_ref[...])
    m = None
    for i, o in enumerate(outs):
        term = (jnp.tanh(zg_ref[:, i * d:(i + 1) * d].astype(F32)) + 1.0) * _dot(o, wb_ref[i])
        m = term if m is None else m + term
    mo = _dot(m.astype(BF16), wo_ref[...])
    o_ref[...] = x_ref[...] + mod_ref[0, 2:3, :] * _rms(mo, g_ref[...])


def _merge(z2, o_a, o_b, y_s, o_d, x2, mod, g_post, wb, wo, glu_w, glu_b, tiles_per_mod, tm):
    m, d = x2.shape
    row = lambda i: (i, 0)
    return pl.pallas_call(
        _merge_kernel,
        grid=(m // tm,),
        in_specs=[pl.BlockSpec((tm, Z_GATES_W), row)] + [pl.BlockSpec((tm, BW), row)] * 4
        + [pl.BlockSpec((tm, d), row),
           pl.BlockSpec((1, 8, d), lambda i: (i // tiles_per_mod, 0, 0)),
           _const_spec((1, d)), _const_spec(wb.shape), _const_spec(wo.shape),
           _const_spec(glu_w.shape), _const_spec((1, BW))],
        out_specs=pl.BlockSpec((tm, d), row),
        out_shape=jax.ShapeDtypeStruct((m, d), F32),
        compiler_params=_cparams("arbitrary"),
        name="merge",
    )(z2, o_a, o_b, y_s, o_d, x2, mod, g_post.reshape(1, d), wb, wo, glu_w, glu_b.reshape(1, BW).astype(F32))


def _ffn_dense_kernel(x_ref, mod_ref, gpre_ref, gpost_ref, w1_ref, w3_ref, w2_ref, o_ref):
    x = x_ref[...]
    h = (_rms(x, gpre_ref[...]) * (1.0 + mod_ref[0, 4:5, :]) + mod_ref[0, 3:4, :]).astype(BF16)
    a = _dot(h, w1_ref[...])
    a = (a * _sigmoid(a)) * _dot(h, w3_ref[...])
    f = _dot(a.astype(BF16), w2_ref[...])
    o_ref[...] = x + mod_ref[0, 5:6, :] * _rms(f, gpost_ref[...])


def _ffn_dense(x2, mod, g_pre, g_post, w1, w3, w2, tiles_per_mod, tm):
    m, d = x2.shape
    return pl.pallas_call(
        _ffn_dense_kernel,
        grid=(m // tm,),
        in_specs=[pl.BlockSpec((tm, d), lambda i: (i, 0)),
                  pl.BlockSpec((1, 8, d), lambda i: (i // tiles_per_mod, 0, 0)),
                  _const_spec((1, d)), _const_spec((1, d)),
                  _const_spec(w1.shape), _const_spec(w3.shape), _const_spec(w2.shape)],
        out_specs=pl.BlockSpec((tm, d), lambda i: (i, 0)),
        out_shape=jax.ShapeDtypeStruct((m, d), F32),
        compiler_params=_cparams("arbitrary"),
        name="ffn_dense",
    )(x2, mod, g_pre.reshape(1, d), g_post.reshape(1, d), w1, w3, w2)


def _route_kernel(x_ref, mod_ref, gpre_ref, r_ref, h_ref, idx_ref, wt_ref, cnt_ref, tri_ref):
    tm = x_ref.shape[0]

    @pl.when(pl.program_id(0) == 0)
    def _():
        cnt_ref[...] = jnp.zeros_like(cnt_ref)
        earlier = lax.broadcasted_iota(jnp.int32, (tm, tm), 1) < lax.broadcasted_iota(jnp.int32, (tm, tm), 0)
        tri_ref[...] = jnp.where(earlier, 1.0, 0.0).astype(BF16)

    h = _rms(x_ref[...], gpre_ref[...]) * (1.0 + mod_ref[0, 4:5, :]) + mod_ref[0, 3:4, :]
    h_hi = h.astype(BF16)
    h_ref[...] = h_hi
    h_lo = (h - h_hi.astype(F32)).astype(BF16)
    a = _dot(h_hi, r_ref[...])
    logits = a + pltpu.roll(a, 128 - N_EXPERTS, 1) + _dot(h_lo, r_ref[...])
    lane = lax.broadcasted_iota(jnp.int32, logits.shape, 1)
    logits = jnp.where(lane < N_EXPERTS, logits, 2.0 * NEG_INF)
    m0 = logits.max(axis=-1, keepdims=True)
    i0 = jnp.where(logits == m0, lane, 128).min(axis=-1, keepdims=True)
    rest = jnp.where(lane == i0, NEG_INF, logits)
    m1 = rest.max(axis=-1, keepdims=True)
    i1 = jnp.where(rest == m1, lane, 128).min(axis=-1, keepdims=True)
    e = jnp.exp(m1 - m0)
    w0 = 1.0 / (1.0 + e)
    wt_ref[...] = jnp.where(lane == 0, w0, jnp.where(lane == 1, e * w0, 0.0))
    oh0 = jnp.where(lane == i0, 1.0, 0.0)
    oh1 = jnp.where(lane == i1, 1.0, 0.0)
    both = oh0 + oh1
    seen = _dot(tri_ref[...], both.astype(BF16)) + cnt_ref[0:1, :]
    rank0 = (oh0 * seen).sum(axis=-1, keepdims=True).astype(jnp.int32)
    rank1 = (oh1 * seen).sum(axis=-1, keepdims=True).astype(jnp.int32)
    cnt_ref[0:1, :] = cnt_ref[0:1, :] + both.sum(axis=0, keepdims=True)
    idx_ref[...] = jnp.where(lane == 0, i0, jnp.where(lane == 1, i1, jnp.where(lane == 2, rank0,
                                                                               jnp.where(lane == 3, rank1, 0))))


def _route(x2, mod, g_pre, router, tiles_per_mod, tm):
    m, d = x2.shape
    r_hi = router.astype(BF16)
    r_lo = (router.astype(F32) - r_hi.astype(F32)).astype(BF16)
    r_pad = jnp.pad(jnp.concatenate([r_hi, r_lo], axis=1), ((0, 0), (0, 128 - 2 * N_EXPERTS)))
    row = lambda i: (i, 0)
    return pl.pallas_call(
        _route_kernel,
        grid=(m // tm,),
        in_specs=[pl.BlockSpec((tm, d), row),
                  pl.BlockSpec((1, 8, d), lambda i: (i // tiles_per_mod, 0, 0)),
                  _const_spec((1, d)), _const_spec((d, 128))],
        out_specs=[pl.BlockSpec((tm, d), row), pl.BlockSpec((tm, 128), row), pl.BlockSpec((tm, 128), row),
                   pl.BlockSpec((8, 128), lambda i: (0, 0))],
        out_shape=[jax.ShapeDtypeStruct((m, d), BF16), jax.ShapeDtypeStruct((m, 128), jnp.int32),
                   jax.ShapeDtypeStruct((m, 128), F32), jax.ShapeDtypeStruct((8, 128), F32)],
        scratch_shapes=[pltpu.VMEM((tm, tm), BF16)],
        compiler_params=_cparams("arbitrary"),
        name="moe_route",
    )(x2, mod, g_pre.reshape(1, d), r_pad)


def _ffn_grouped_kernel(te_ref, nv_ref, h_ref, w1_ref, w3_ref, w2_ref, *rest, tile_off):
    o_ref, acc_ref = rest[-2:]
    i, j = pl.program_id(0), pl.program_id(1)
    last = pl.num_programs(1) - 1

    @pl.when(j == 0)
    def _():
        acc_ref[...] = jnp.zeros_like(acc_ref)

    @pl.when(tile_off + i < nv_ref[0])
    def _():
        h = h_ref[...]
        a = _dot(h, w1_ref[0].astype(BF16))
        a = (a * _sigmoid(a)) * _dot(h, w3_ref[0].astype(BF16))
        acc_ref[...] += _dot(a.astype(BF16), w2_ref[0].astype(BF16))

    @pl.when(j == last)
    def _():
        o_ref[...] = acc_ref[...].astype(BF16)


def _ffn_grouped(h_chunk, tile_expert, n_valid, w1, w3, w2, prev, tile_off, total_rows, tm, fc):
    r, d = h_chunk.shape
    f = w1.shape[2]
    nj = f // fc

    def col(i, j, nv):
        return jnp.where(tile_off + i < nv[0], j, nj - 1)

    in_specs = [pl.BlockSpec((tm, d), lambda i, j, te, nv: (i, 0)),
                pl.BlockSpec((1, d, fc), lambda i, j, te, nv: (te[tile_off + i], 0, col(i, j, nv))),
                pl.BlockSpec((1, d, fc), lambda i, j, te, nv: (te[tile_off + i], 0, col(i, j, nv))),
                pl.BlockSpec((1, fc, d), lambda i, j, te, nv: (te[tile_off + i], col(i, j, nv), 0))]
    args = [tile_expert, n_valid, h_chunk, w1, w3, w2]
    aliases = {}
    if prev is not None:
        in_specs.append(pl.BlockSpec(memory_space=pl.ANY))
        aliases = {len(args): 0}
        args.append(prev)
    grid_spec = pltpu.PrefetchScalarGridSpec(
        num_scalar_prefetch=2,
        grid=(r // tm, nj),
        in_specs=in_specs,
        out_specs=pl.BlockSpec((tm, d), lambda i, j, te, nv: (tile_off + i, 0)),
        scratch_shapes=[pltpu.VMEM((tm, d), F32)],
    )
    return pl.pallas_call(
        functools.partial(_ffn_grouped_kernel, tile_off=tile_off),
        grid_spec=grid_spec,
        out_shape=jax.ShapeDtypeStruct((total_rows, d), BF16),
        input_output_aliases=aliases,
        compiler_params=_cparams("arbitrary", "arbitrary"),
        name="ffn_grouped",
    )(*args)


def _combine_kernel(o0_ref, o1_ref, wt_ref, x_ref, mod_ref, gpost_ref, o_ref):
    y = wt_ref[:, 0:1] * o0_ref[...].astype(F32) + wt_ref[:, 1:2] * o1_ref[...].astype(F32)
    o_ref[...] = x_ref[...] + mod_ref[0, 5:6, :] * _rms(y, gpost_ref[...])


def _combine(o0, o1, wts, x2, mod, g_post, tiles_per_mod, tm):
    m, d = x2.shape
    row = lambda i: (i, 0)
    return pl.pallas_call(
        _combine_kernel,
        grid=(m // tm,),
        in_specs=[pl.BlockSpec((tm, d), row), pl.BlockSpec((tm, d), row), pl.BlockSpec((tm, 128), row),
                  pl.BlockSpec((tm, d), row),
                  pl.BlockSpec((1, 8, d), lambda i: (i // tiles_per_mod, 0, 0)),
                  _const_spec((1, d))],
        out_specs=pl.BlockSpec((tm, d), row),
        out_shape=jax.ShapeDtypeStruct((m, d), F32),
        compiler_params=_cparams("arbitrary"),
        name="moe_combine",
    )(o0, o1, wts, x2, mod, g_post.reshape(1, d))


def _moe(x2, mod, g_pre, g_post, router, w1, w3, w2, tiles_per_mod, tm):
    m, d = x2.shape
    h, idx, wts, cnt = _route(x2, mod, g_pre, router, tiles_per_mod, tm)
    gt = MOE_GROUP_TILE
    idx_t = idx[:, :4].T
    expert, rank = idx_t[:2], idx_t[2:]
    counts = cnt[0, :N_EXPERTS].astype(jnp.int32)
    padded = (counts + gt - 1) // gt * gt
    ends = jnp.cumsum(padded)
    starts = ends - padded
    pos = rank + sum(jnp.where(expert == e, starts[e], 0) for e in range(N_EXPERTS))
    n_tiles = (2 * m) // gt + N_EXPERTS
    token = jnp.tile(jnp.arange(m, dtype=jnp.int32), 2)
    src = jnp.zeros((n_tiles * gt,), jnp.int32).at[pos.reshape(-1)].set(token, unique_indices=True,
                                                                        mode='promise_in_bounds')
    tile_start = jnp.arange(n_tiles, dtype=jnp.int32) * gt
    tile_expert = jnp.minimum((ends[None, :] <= tile_start[:, None]).sum(axis=1), N_EXPERTS - 1).astype(jnp.int32)
    n_valid = (ends[-1] // gt).astype(jnp.int32).reshape(1)
    rows = lambda a, i: a.at[i].get(mode='promise_in_bounds')
    n_chunks = math.gcd(n_tiles, MOE_DISPATCH_CHUNKS)
    chunk_tiles = n_tiles // n_chunks
    out = None
    for ci in range(n_chunks):
        src_c = src[ci * chunk_tiles * gt:(ci + 1) * chunk_tiles * gt]
        out = _ffn_grouped(rows(h, src_c), tile_expert, n_valid, w1, w3, w2, out, ci * chunk_tiles, n_tiles * gt,
                           gt, MOE_F_CHUNK)
    return _combine(rows(out, pos[0]), rows(out, pos[1]), wts, x2, mod, g_post, tiles_per_mod, tm)


def _z_weights(w_in):
    d = w_in.shape[0]
    hw = N_HEADS * HEAD_DIM
    kvw = SWA_KV_HEADS * HEAD_DIM
    sizes = (hw, hw, hw, hw, kvw, kvw, BW, BW, BW, Z_GATES_W)
    qa, ka, va, qd, kd, vd, gu, gv, su, gates = jnp.split(w_in, np.cumsum(sizes)[:-1].tolist(), axis=1)
    rep = N_HEADS // SWA_KV_HEADS
    dup = lambda w: jnp.repeat(w.reshape(d, SWA_KV_HEADS, 1, HEAD_DIM), rep, axis=2).reshape(d, hw)
    return jnp.concatenate([0.5 * gates, qa, ka, va, qd, dup(kd), dup(vd), gu, gv, su], axis=1).astype(BF16)


def kernel(x, c, ctx, c_ctx, w_mod, b_mod, g_pre_mix, g_post_mix, g_pre_ffn, g_post_ffn, w_in, na_rpb, swa_sink, gmlp_ln_g, gmlp_ln_b, gmlp_ws, gmlp_bs, s5_a_re, s5_a_im, s5_log_dt, s5_b_re, s5_b_im, s5_c_re, s5_c_im, s5_d, s5_glu_w, s5_glu_b, w_branch, w_out, ffn_w1, ffn_w3, ffn_w2, moe_router, moe_w1, moe_w3, moe_w2):
    bsz, s, d = x.shape
    l = ctx.shape[1]
    depth = w_mod.shape[0]
    tm = 512
    rows = s // GRID_W

    c_all = jnp.zeros((8, d), F32).at[:bsz].set(c).at[bsz].set(c_ctx)
    mods = _modulation(c_all, w_mod, b_mod).reshape(depth, 8, 6, d)
    mods = jnp.pad(mods, ((0, 0), (0, 0), (0, 2), (0, 0)))
    rope_tabs = _rope_tables(s)
    swa_bias = _swa_bias(s)
    w_z_all = jax.vmap(_z_weights)(w_in)
    na_bias_all = jax.vmap(functools.partial(_na_bias, rows=rows))(na_rpb)
    s5_all = jax.vmap(_s5_matrices)(s5_a_re, s5_a_im, s5_log_dt, s5_b_re, s5_b_im, s5_c_re, s5_c_im, s5_d)
    wb_all, wo_all, glu_all = (0.5 * w_branch).astype(BF16), w_out.astype(BF16), s5_glu_w.astype(BF16)
    sink_all = jnp.repeat(swa_sink.astype(F32) * LOG2E, ATT_TQ, axis=1).reshape(depth, 1, N_HEADS * ATT_TQ)

    x2 = x.reshape(bsz * s, d)
    c2 = ctx.reshape(bsz * l, d)
    for i in range(depth):
        ctx_out = i < depth - 1
        mod_x = mods[i, :bsz]
        mod_c = mods[i, bsz:bsz + 1]
        w_z = w_z_all[i]
        zx, su_x = _project(x2, mod_x, g_pre_mix[i], w_z, s // tm, rope_tabs, tm)
        zc, su_c = _project(c2, mod_c, g_pre_mix[i], w_z, (bsz * l) // tm, None, tm)
        zx3 = zx.reshape(bsz, s, ZW)
        zc3 = zc.reshape(bsz, l, ZW)
        sink_rows = sink_all[i]

        nq = s // ATT_TQ
        o_a = _local_attention(zx3, zc3, Z_QA, Z_KA, Z_VA, na_bias_all[i], None, ATT_TQ,
                               lambda qi, j: jnp.clip(qi - 1, 0, nq - 3) + j)
        o_d = _local_attention(zx3, zc3, Z_QD, Z_KD, Z_VD, swa_bias, sink_rows, SWA_WINDOW,
                               functools.partial(_swa_key_index, n_blocks=s // SWA_WINDOW))
        o_b = _gmlp(zx, gmlp_ln_g[i], gmlp_ln_b[i], gmlp_ws[i], gmlp_bs[i], tm)
        s5_mats = tuple(v[i] for v in s5_all)
        y_c, y_x = _s5_mix(su_c.reshape(bsz, l, BW), su_x.reshape(bsz, s, BW), s5_mats, ctx_out)

        wb, wo, glu_w = wb_all[i], wo_all[i], glu_all[i]
        x2 = _merge(zx, o_a.reshape(bsz * s, BW), o_b, y_x.reshape(bsz * s, BW), o_d.reshape(bsz * s, BW),
                    x2, mod_x, g_post_mix[i], wb, wo, glu_w, s5_glu_b[i], s // tm, tm)
        if ctx_out:
            o_a_c = _ctx_attention(zc3, Z_QA, Z_KA, Z_VA, None)
            o_d_c = _ctx_attention(zc3, Z_QD, Z_KD, Z_VD, sink_rows)
            o_b_c = _gmlp(zc, gmlp_ln_g[i], gmlp_ln_b[i], gmlp_ws[i], gmlp_bs[i], tm)
            c2 = _merge(zc, o_a_c.reshape(bsz * l, BW), o_b_c, y_c.reshape(bsz * l, BW), o_d_c.reshape(bsz * l, BW),
                        c2, mod_c, g_post_mix[i], wb, wo, glu_w, s5_glu_b[i], (bsz * l) // tm, tm)

        j = i // 2
        if i % 2 == 0:
            w1, w3, w2 = ffn_w1[j].astype(BF16), ffn_w3[j].astype(BF16), ffn_w2[j].astype(BF16)
            x2 = _ffn_dense(x2, mod_x, g_pre_ffn[i], g_post_ffn[i], w1, w3, w2, s // tm, tm)
            if ctx_out:
                c2 = _ffn_dense(c2, mod_c, g_pre_ffn[i], g_post_ffn[i], w1, w3, w2, (bsz * l) // tm, tm)
        else:
            w1, w3, w2 = moe_w1[j], moe_w3[j], moe_w2[j]
            x2 = _moe(x2, mod_x, g_pre_ffn[i], g_post_ffn[i], moe_router[j], w1, w3, w2, s // tm, tm)
            if ctx_out:
                c2 = _moe(c2, mod_c, g_pre_ffn[i], g_post_ffn[i], moe_router[j], w1, w3, w2, (bsz * l) // tm, tm)
    return x2.reshape(bsz, s, d)
```

```python
import functools
import math

import numpy as np
import jax
import jax.numpy as jnp
from jax import lax
from jax.experimental import pallas as pl
from jax.experimental.pallas import tpu as pltpu

F32 = jnp.float32
BF16 = jnp.bfloat16
HIGHEST = lax.Precision.HIGHEST

GRID_W = 64
HEAD_DIM = 64
N_HEADS = 4
NA_ROWS = 8
NA_COLS = 16
SWA_KV_HEADS = 2
SWA_WINDOW = 128
GMLP_GROUPS = 4
GMLP_CHUNK = 128
S5_GROUP = 16
S5_GROUPS = 16
S5_STATE = 64
N_EXPERTS = 8
ROPE_BASE = 10000.0
EPS = 1e-6
NEG_INF = -1e30
LOG2E = math.log2(math.e)
Q_SCALE = HEAD_DIM ** -0.5 * LOG2E

BW = 256
S5_T = 16
S5_ROWS = 8
Z_GATES_W = 4096
(Z_QA, Z_KA, Z_VA, Z_QD, Z_KD, Z_VD, Z_GU, Z_GV, Z_SU) = range(Z_GATES_W // BW, Z_GATES_W // BW + 9)
ZW = Z_GATES_W + 9 * BW

V7X_VMEM_LIMIT = 56 * 1024 * 1024
ATT_TQ = 256
ONES_ROWS = 16
ATT_CHAIN_HEADS = 4
ATT_TILES_PER_STEP = 4
SWA_LOCAL_BLOCKS = ATT_TQ // SWA_WINDOW + 2
MOE_GROUP_TILE = 1024
MOE_F_CHUNK = 512
MOE_DISPATCH_CHUNKS = 4


def _cparams(*sem):
    return pltpu.CompilerParams(dimension_semantics=sem, vmem_limit_bytes=V7X_VMEM_LIMIT)


def _const_spec(shape):
    nd = len(shape)
    return pl.BlockSpec(shape, lambda *_: (0,) * nd, pipeline_mode=pl.Buffered(1))


def _dot(a, b):
    return jnp.dot(a, b, preferred_element_type=F32)


def _dot_nt(a, b):
    return lax.dot_general(a, b, (((1,), (1,)), ((), ())), preferred_element_type=F32)


def _sigmoid(x):
    return 0.5 * jnp.tanh(0.5 * x) + 0.5


def _rms(x, g):
    return x * lax.rsqrt(jnp.mean(x * x, axis=-1, keepdims=True) + EPS) * g


def _mod_kernel(c_ref, w_ref, b_ref, o_ref):
    c = c_ref[...]
    a = c * _sigmoid(c)
    o_ref[0] = jnp.dot(a, w_ref[0], preferred_element_type=F32, precision=HIGHEST) + b_ref[0]


def _modulation(c_all, w_mod, b_mod):
    depth, d, n = w_mod.shape
    tn = 1536
    return pl.pallas_call(
        _mod_kernel,
        grid=(depth, n // tn),
        in_specs=[pl.BlockSpec((8, d), lambda l, j: (0, 0)),
                  pl.BlockSpec((1, d, tn), lambda l, j: (l, 0, j)),
                  pl.BlockSpec((1, 1, tn), lambda l, j: (l, 0, j))],
        out_specs=pl.BlockSpec((1, 8, tn), lambda l, j: (l, 0, j)),
        out_shape=jax.ShapeDtypeStruct((depth, 8, n), F32),
        compiler_params=_cparams("arbitrary", "arbitrary"),
        name="modulation",
    )(c_all, w_mod, b_mod.reshape(depth, 1, n))


def _proj_kernel(*refs, rope, n_chunk):
    if rope:
        x_ref, mod_ref, g_ref, w_ref, cos_ref, sin_ref, o_ref, su_ref = refs
    else:
        x_ref, mod_ref, g_ref, w_ref, o_ref, su_ref = refs
    h = _rms(x_ref[...], g_ref[...]) * (1.0 + mod_ref[0, 1:2, :]) + mod_ref[0, 0:1, :]
    h = h.astype(BF16)
    rope_blocks = (Z_QD, Z_KD) if rope else ()
    for j in range(ZW // n_chunk):
        lo = j * n_chunk
        r = _dot(h, w_ref[:, lo:lo + n_chunk])
        blocks = range(lo // BW, (lo + n_chunk) // BW)
        if Z_SU in blocks:
            su_ref[...] = r[:, Z_SU * BW - lo:(Z_SU + 1) * BW - lo]
        if not any(b in rope_blocks or b in (Z_QA, Z_QD) for b in blocks):
            o_ref[:, lo:lo + n_chunk] = r.astype(BF16)
            continue
        for b in blocks:
            t = r[:, b * BW - lo:(b + 1) * BW - lo]
            if b in rope_blocks:
                lane = lax.broadcasted_iota(jnp.int32, (1, BW), 1)
                low_half = (lane % (HEAD_DIM // 2)) < (HEAD_DIM // 4)
                partner = jnp.where(low_half, pltpu.roll(t, BW - HEAD_DIM // 4, 1),
                                    pltpu.roll(t, HEAD_DIM // 4, 1))
                t = t * cos_ref[...] + partner * sin_ref[...]
            if b in (Z_QA, Z_QD):
                t = t * Q_SCALE
            o_ref[:, b * BW:(b + 1) * BW] = t.astype(BF16)


def _project(x2, mod, g, w_z, tiles_per_mod, rope_tabs, tm):
    m, d = x2.shape
    rope = rope_tabs is not None
    in_specs = [pl.BlockSpec((tm, d), lambda i: (i, 0)),
                pl.BlockSpec((1, 8, d), lambda i: (i // tiles_per_mod, 0, 0)),
                _const_spec((1, d)),
                _const_spec((d, ZW))]
    args = [x2, mod, g.reshape(1, d), w_z]
    if rope:
        n_rt = rope_tabs[0].shape[0] // tm
        in_specs += [pl.BlockSpec((tm, BW), lambda i: (i % n_rt, 0))] * 2
        args += list(rope_tabs)
    return pl.pallas_call(
        functools.partial(_proj_kernel, rope=rope, n_chunk=1280),
        grid=(m // tm,),
        in_specs=in_specs,
        out_specs=[pl.BlockSpec((tm, ZW), lambda i: (i, 0)), pl.BlockSpec((tm, BW), lambda i: (i, 0))],
        out_shape=[jax.ShapeDtypeStruct((m, ZW), BF16), jax.ShapeDtypeStruct((m, BW), F32)],
        compiler_params=_cparams("arbitrary"),
        name="project_in",
    )(*args)


def _attn_kernel(*refs, n_local, use_sink, n_tiles):
    per_tile = 1 + 2 * n_local + (1 if n_local else 0)
    tiles = [refs[t * per_tile:(t + 1) * per_tile] for t in range(n_tiles)]
    pos = n_tiles * per_tile
    kc_ref, vc_ref = refs[pos], refs[pos + 1]
    pos += 2
    sink_ref = None
    if use_sink:
        sink_ref = refs[pos]
        pos += 1
    o_ref = refs[pos]

    tq = tiles[0][0].shape[1]
    lane_head = lax.broadcasted_iota(jnp.int32, (1, BW), 1) // HEAD_DIM
    vc_t = vc_ref[0].T
    chains = [range(h0, h0 + ATT_CHAIN_HEADS) for h0 in range(0, N_HEADS, ATT_CHAIN_HEADS)]

    def score_stage(tile, heads):
        q = tile[0][0]
        cols = slice(heads[0] * tq, (heads[-1] + 1) * tq)
        q_heads = jnp.concatenate([jnp.where(lane_head == h, q, jnp.zeros_like(q)) for h in heads], axis=0)
        scores = []
        for j in range(n_local):
            k_ref = tile[1 + 2 * j]
            kb = k_ref.shape[1]
            scores.append(_dot_nt(k_ref[0], q_heads) + tile[-1][0, j * kb:(j + 1) * kb, cols])
        scores.append(_dot_nt(kc_ref[0], q_heads))
        return scores

    def softmax_value_stage(tile, heads, scores):
        cols = slice(heads[0] * tq, (heads[-1] + 1) * tq)
        v_t = [tile[2 + 2 * j][0].T for j in range(n_local)] + [vc_t]
        mx = scores[0].max(axis=0, keepdims=True)
        for s in scores[1:]:
            mx = jnp.maximum(mx, s.max(axis=0, keepdims=True))
        if use_sink:
            mx = jnp.maximum(mx, sink_ref[:, cols])
        o = [jnp.zeros((HEAD_DIM + ONES_ROWS, tq), F32) for _ in heads]
        for s, vt in zip(scores, v_t):
            p = jnp.exp2(s - mx).astype(BF16)
            ones = jnp.ones((ONES_ROWS, vt.shape[1]), BF16)
            for i, h in enumerate(heads):
                lhs = jnp.concatenate([vt[h * HEAD_DIM:(h + 1) * HEAD_DIM, :], ones], axis=0)
                o[i] = o[i] + _dot(lhs, p[:, i * tq:(i + 1) * tq])
        res = []
        for i, h in enumerate(heads):
            den = o[i][HEAD_DIM:HEAD_DIM + 1, :]
            if use_sink:
                den = den + jnp.exp2(sink_ref[:, h * tq:(h + 1) * tq] - mx[:, i * tq:(i + 1) * tq])
            res.append(o[i][:HEAD_DIM, :] / den)
        return res

    work = [(t, heads) for heads in chains for t in range(n_tiles)]
    outs = {}
    pending = score_stage(tiles[work[0][0]], work[0][1])
    for c, (t, heads) in enumerate(work):
        scores = pending
        if c + 1 < len(work):
            pending = score_stage(tiles[work[c + 1][0]], work[c + 1][1])
        outs[(t, heads[0])] = softmax_value_stage(tiles[t], heads, scores)
    for t in range(n_tiles):
        o_t = jnp.concatenate([r for heads in chains for r in outs[(t, heads[0])]], axis=0)
        o_ref[0, t * tq:(t + 1) * tq, :] = o_t.T.astype(BF16)


def _local_attention(z, z_c, q_col, k_col, v_col, bias, sink_rows, key_block, key_index):
    b, s, _ = z.shape
    l = z_c.shape[1]
    tq = ATT_TQ
    nq = s // tq
    nt = ATT_TILES_PER_STEP
    n_local = bias.shape[1] // key_block
    assert nq >= 3 and nq % nt == 0 and l == BW

    def pat(i):
        return jnp.where(i == 0, 0, jnp.where(i == nq - 1, 2, 1))

    in_specs, args = [], []
    for t in range(nt):
        tile = lambda i, t=t: i * nt + t
        in_specs.append(pl.BlockSpec((1, tq, BW), lambda bi, i, tile=tile: (bi, tile(i), q_col)))
        args.append(z)
        for j in range(n_local):
            for col in (k_col, v_col):
                in_specs.append(pl.BlockSpec((1, key_block, BW),
                                             lambda bi, i, tile=tile, j=j, col=col: (bi, key_index(tile(i), j), col)))
                args.append(z)
        in_specs.append(pl.BlockSpec((1,) + bias.shape[1:], lambda bi, i, tile=tile: (pat(tile(i)), 0, 0)))
        args.append(bias)
    in_specs += [pl.BlockSpec((1, l, BW), lambda bi, i: (bi, 0, k_col)),
                 pl.BlockSpec((1, l, BW), lambda bi, i: (bi, 0, v_col))]
    args += [z_c, z_c]
    if sink_rows is not None:
        in_specs.append(_const_spec(sink_rows.shape))
        args.append(sink_rows)
    return pl.pallas_call(
        functools.partial(_attn_kernel, n_local=n_local, use_sink=sink_rows is not None, n_tiles=nt),
        grid=(b, nq // nt),
        in_specs=in_specs,
        out_specs=pl.BlockSpec((1, nt * tq, BW), lambda bi, i: (bi, i, 0)),
        out_shape=jax.ShapeDtypeStruct((b, s, BW), BF16),
        compiler_params=_cparams("arbitrary", "arbitrary"),
        name="local_attention",
    )(*args)


def _ctx_attention(z_c, q_col, k_col, v_col, sink_rows):
    b, l, _ = z_c.shape
    in_specs = [pl.BlockSpec((1, l, BW), lambda bi, c=col: (bi, 0, c)) for col in (q_col, k_col, v_col)]
    args = [z_c, z_c, z_c]
    if sink_rows is not None:
        in_specs.append(_const_spec(sink_rows.shape))
        args.append(sink_rows)
    return pl.pallas_call(
        functools.partial(_attn_kernel, n_local=0, use_sink=sink_rows is not None, n_tiles=1),
        grid=(b,),
        in_specs=in_specs,
        out_specs=pl.BlockSpec((1, l, BW), lambda bi: (bi, 0, 0)),
        out_shape=jax.ShapeDtypeStruct((b, l, BW), BF16),
        compiler_params=_cparams("arbitrary"),
        name="ctx_attention",
    )(*args)


def _na_bias(rpb, rows):
    tile_rows = ATT_TQ // GRID_W
    nq = rows // tile_rows
    col = np.arange(GRID_W)
    cs = np.clip(col - NA_COLS // 2, 0, GRID_W - NA_COLS)[:, None]
    kc = col[None, :]
    sel_c = ((kc - col[:, None] + NA_COLS - 1)[None] == np.arange(2 * NA_COLS - 1)[:, None, None]) \
        & ((kc >= cs) & (kc < cs + NA_COLS))[None]
    sel_r = []
    for i in (0, 1, nq - 1):
        base = min(max(i - 1, 0), nq - 3)
        qr = (tile_rows * i + np.arange(tile_rows))[:, None]
        kr = (tile_rows * base + np.arange(3 * tile_rows))[None, :]
        rs = np.clip(qr - NA_ROWS // 2, 0, rows - NA_ROWS)
        sel_r.append(((kr - qr + NA_ROWS - 1)[None] == np.arange(2 * NA_ROWS - 1)[:, None, None])
                     & ((kr >= rs) & (kr < rs + NA_ROWS))[None])
    sel_r = np.stack(sel_r).astype(np.float32)
    sel_c = sel_c.astype(np.float32)
    t1 = jnp.einsum('paqk,hab->phqkb', sel_r, rpb.astype(F32), precision=HIGHEST)
    bias = jnp.einsum('phqkb,bcd->pkdhqc', t1, sel_c, precision=HIGHEST)
    valid = np.einsum('paqk,bcd->pkdqc', sel_r, sel_c) > 0.5
    mask = np.where(valid, 0.0, NEG_INF).astype(np.float32)[:, :, :, None]
    return (bias * LOG2E + mask).reshape(3, 3 * ATT_TQ, rpb.shape[0] * ATT_TQ)


def _swa_key_index(i, j, n_blocks):
    return jnp.clip(i * (ATT_TQ // SWA_WINDOW) - 1 + j, 0, n_blocks - 1)


def _swa_bias(s):
    nq = s // ATT_TQ
    tq = np.arange(ATT_TQ)[:, None]
    tk = np.arange(SWA_LOCAL_BLOCKS * SWA_WINDOW)[None, :]
    out = []
    for i in (0, 1, nq - 1):
        kpos = ATT_TQ * i - SWA_WINDOW + tk
        valid = (np.abs(kpos - (ATT_TQ * i + tq)) <= SWA_WINDOW) & (kpos >= 0) & (kpos < s)
        out.append(np.tile(np.where(valid, 0.0, NEG_INF).astype(np.float32).T, (1, N_HEADS)))
    return jnp.asarray(np.stack(out))


def _rope_tables(s):
    nq = HEAD_DIM // 4
    t = jnp.arange(s)
    inv = ROPE_BASE ** (-jnp.arange(nq, dtype=F32) / nq)
    ang_r = (t // GRID_W).astype(F32)[:, None] * inv[None, :]
    ang_c = (t % GRID_W).astype(F32)[:, None] * inv[None, :]
    cos = jnp.concatenate([jnp.cos(ang_r)] * 2 + [jnp.cos(ang_c)] * 2, axis=-1)
    sin = jnp.concatenate([-jnp.sin(ang_r), jnp.sin(ang_r), -jnp.sin(ang_c), jnp.sin(ang_c)], axis=-1)
    return jnp.tile(cos, (1, N_HEADS)), jnp.tile(sin, (1, N_HEADS))


def _gmlp_kernel(u_ref, v_ref, g_ref, b_ref, w_ref, bs_ref, o_ref):
    lane_grp = lax.broadcasted_iota(jnp.int32, (1, BW), 1) // (BW // GMLP_GROUPS)
    for c in range(u_ref.shape[0] // GMLP_CHUNK):
        rows = slice(c * GMLP_CHUNK, (c + 1) * GMLP_CHUNK)
        u = jax.nn.gelu(u_ref[rows, :].astype(F32))
        v = jax.nn.gelu(v_ref[rows, :].astype(F32))
        mu = jnp.mean(v, axis=-1, keepdims=True)
        vc = v - mu
        v = vc * lax.rsqrt(jnp.mean(vc * vc, axis=-1, keepdims=True) + EPS) * g_ref[...] + b_ref[...]
        v = v.astype(BF16)
        stack = jnp.concatenate([jnp.where(lane_grp == g, v, jnp.zeros_like(v)) for g in range(GMLP_GROUPS)], axis=0)
        sg = _dot(w_ref[...], stack) + bs_ref[...]
        o_ref[rows, :] = (u * sg).astype(BF16)


def _gmlp(z2, ln_g, ln_b, ws, bs, tm):
    m = z2.shape[0]
    w_cat = jnp.concatenate([ws[g] for g in range(GMLP_GROUPS)], axis=1).astype(BF16)
    bs_l = jnp.repeat(bs.T.astype(F32), BW // GMLP_GROUPS, axis=1)
    return pl.pallas_call(
        _gmlp_kernel,
        grid=(m // tm,),
        in_specs=[pl.BlockSpec((tm, BW), lambda i: (i, Z_GU)),
                  pl.BlockSpec((tm, BW), lambda i: (i, Z_GV)),
                  _const_spec((1, BW)), _const_spec((1, BW)),
                  _const_spec(w_cat.shape), _const_spec(bs_l.shape)],
        out_specs=pl.BlockSpec((tm, BW), lambda i: (i, 0)),
        out_shape=jax.ShapeDtypeStruct((m, BW), BF16),
        compiler_params=_cparams("arbitrary"),
        name="gmlp",
    )(z2, z2, ln_g.reshape(1, BW).astype(F32), ln_b.reshape(1, BW).astype(F32), w_cat, bs_l)


def _s5_kernel(su_ref, suc_ref, wend_ref, toep_ref, wc_ref, a16_ref, *rest, ctx_out):
    if ctx_out:
        y_ref, yc_ref, ut_ref, utc_ref, up_ref, st_ref = rest
    else:
        y_ref, ut_ref, utc_ref, up_ref, st_ref = rest
    nj, nc = su_ref.shape[1] // S5_T, suc_ref.shape[1] // S5_T
    lanes = su_ref.shape[2]
    ng = lanes // S5_GROUP
    hw = ng * S5_STATE
    cpad = utc_ref.shape[1]

    def grp(g, t):
        return slice(g * BW + t * S5_GROUP, g * BW + (t + 1) * S5_GROUP)

    for t in range(S5_T):
        at = su_ref[0, pl.ds(t, nj, stride=S5_T), :].T
        ct = jnp.concatenate([suc_ref[0, pl.ds(t, nc, stride=S5_T), :],
                              jnp.zeros((cpad - nc, lanes), F32)], axis=0).T
        for g in range(ng):
            ut_ref[grp(g, t), :] = at[g * S5_GROUP:(g + 1) * S5_GROUP, :]
            utc_ref[grp(g, t), :] = ct[g * S5_GROUP:(g + 1) * S5_GROUP, :]

    for p in range(ng // 2):
        blocks = []
        for g in (2 * p, 2 * p + 1):
            xt = ut_ref[g * BW:(g + 1) * BW, :].T
            ct = utc_ref[g * BW:(g + 1) * BW, :].T[:nc]
            blocks.append(jnp.concatenate([ct, xt], axis=0))
        up = jnp.concatenate(blocks, axis=1).astype(BF16)
        up_ref[:, p * 2 * BW:(p + 1) * 2 * BW] = up
        e = _dot(up, wend_ref[p])
        for k in range(4):
            st_ref[:, k * hw + p * 128:k * hw + (p + 1) * 128] = e[:, k * 128:(k + 1) * 128]

    n_tiles, n_ctiles = (nc + nj) // 8, nc // 8
    a = [a16_ref[k:k + 1, :] for k in range(4)]

    def tile_step(k, carry):
        fr, fi, rr, ri = carry
        kf = pl.multiple_of(k * 8, 8)
        kr = jnp.where(k < n_ctiles, n_ctiles - 1 - k, n_tiles - 1 - (k - n_ctiles))
        kr = pl.multiple_of(kr * 8, 8)
        ef_r, ef_i = st_ref[pl.ds(kf, 8), 0:hw], st_ref[pl.ds(kf, 8), hw:2 * hw]
        er_r, er_i = st_ref[pl.ds(kr, 8), 2 * hw:3 * hw], st_ref[pl.ds(kr, 8), 3 * hw:4 * hw]
        hf_r, hf_i, hr_r, hr_i = [], [], [None] * 8, [None] * 8
        for r in range(8):
            hf_r.append(fr)
            hf_i.append(fi)
            fr, fi = a[0] * fr - a[1] * fi + ef_r[r:r + 1], a[0] * fi + a[1] * fr + ef_i[r:r + 1]
            q = 7 - r
            hr_r[q], hr_i[q] = rr, ri
            rr, ri = a[2] * rr - a[3] * ri + er_r[q:q + 1], a[2] * ri + a[3] * rr + er_i[q:q + 1]
        st_ref[pl.ds(kf, 8), 0:hw] = jnp.concatenate(hf_r, axis=0)
        st_ref[pl.ds(kf, 8), hw:2 * hw] = jnp.concatenate(hf_i, axis=0)
        st_ref[pl.ds(kr, 8), 2 * hw:3 * hw] = jnp.concatenate(hr_r, axis=0)
        st_ref[pl.ds(kr, 8), 3 * hw:4 * hw] = jnp.concatenate(hr_i, axis=0)
        return fr, fi, rr, ri

    zero = jnp.zeros((1, hw), F32)
    lax.fori_loop(0, n_tiles, tile_step, (zero, zero, zero, zero))

    for p in range(ng // 2):
        h = jnp.concatenate([st_ref[:, k * hw + p * 128:k * hw + (p + 1) * 128] for k in range(4)], axis=1)
        y = _dot(up_ref[:, p * 2 * BW:(p + 1) * 2 * BW], toep_ref[p]) + _dot(h.astype(BF16), wc_ref[p])
        for gg in range(2):
            g = 2 * p + gg
            yg = y[:, gg * BW:(gg + 1) * BW]
            ut_ref[g * BW:(g + 1) * BW, :] = yg[nc:, :].T
            if ctx_out:
                utc_ref[g * BW:(g + 1) * BW, :] = jnp.concatenate(
                    [yg[:nc, :], jnp.zeros((cpad - nc, BW), F32)], axis=0).T

    for t in range(S5_T):
        z = jnp.concatenate([ut_ref[grp(g, t), :] for g in range(ng)], axis=0)
        y_ref[0, pl.ds(t, nj, stride=S5_T), :] = z.T
        if ctx_out:
            zc = jnp.concatenate([utc_ref[grp(g, t), :] for g in range(ng)], axis=0)
            yc_ref[0, pl.ds(t, nc, stride=S5_T), :] = zc.T[:nc]


def _s5_matrices(a_re, a_im, log_dt, b_re, b_im, c_re, c_im, d):
    g, p, c, t = S5_GROUPS, S5_STATE, S5_GROUP, S5_T
    tc = t * c
    lam_re = jnp.minimum(a_re.astype(F32), -1e-4)
    lam_im = a_im.astype(F32)
    dt = jnp.exp(log_dt.astype(F32))[..., None]
    lr, li = lam_re * dt, lam_im * dt

    def a_pow(n, x_re, x_im):
        mag = jnp.exp(n * x_re)
        return mag * jnp.cos(n * x_im), mag * jnp.sin(n * x_im)

    ab_re, ab_im = a_pow(1.0, lr, li)
    den = lam_re * lam_re + lam_im * lam_im
    k_re = ((ab_re - 1.0) * lam_re + ab_im * lam_im) / den
    k_im = (ab_im * lam_re - (ab_re - 1.0) * lam_im) / den
    br, bi = b_re.astype(F32), b_im.astype(F32)
    bb_re = k_re[..., None] * br - k_im[..., None] * bi
    bb_im = k_re[..., None] * bi + k_im[..., None] * br
    bbt_re, bbt_im = jnp.swapaxes(bb_re, 2, 3), jnp.swapaxes(bb_im, 2, 3)
    ct_re = jnp.swapaxes(c_re.astype(F32), 2, 3)
    ct_im = jnp.swapaxes(c_im.astype(F32), 2, 3)

    n_up = np.repeat(np.arange(t + 1, dtype=np.float32), c)
    ca = []
    for dirn, n_lane in ((0, n_up), (1, t - n_up)):
        pr, pi = a_pow(n_lane[None, None, :], lr[dirn][:, :, None], li[dirn][:, :, None])
        cr_l = jnp.tile(ct_re[dirn], (1, 1, t + 1))
        ci_l = jnp.tile(ct_im[dirn], (1, 1, t + 1))
        ca.append(jnp.concatenate([cr_l * pr - ci_l * pi, -(cr_l * pi + ci_l * pr)], axis=1))
    wc_f, wc_r = ca[0][:, :, c:], ca[1][:, :, :tc]
    bb_cat = jnp.concatenate([bbt_re, bbt_im], axis=-1)
    lag_f = jnp.einsum('gik,gkl->gil', bb_cat[0], ca[0][:, :, :tc], precision=HIGHEST)
    lag_r = jnp.einsum('gik,gkl->gil', bb_cat[1], ca[1][:, :, c:], precision=HIGHEST)
    zeros = jnp.zeros_like(lag_f)
    ext_f = jnp.concatenate([zeros, lag_f], axis=-1)
    ext_r = jnp.concatenate([lag_r, zeros], axis=-1)
    toep = jnp.stack([ext_f[:, :, tc - c * s:2 * tc - c * s] + ext_r[:, :, c * (t - 1 - s):c * (t - 1 - s) + tc]
                      for s in range(t)], axis=1)
    skip = jnp.eye(tc, dtype=F32)[None] * jnp.tile(d.astype(F32).reshape(g, 1, c), (1, 1, t))
    toep = toep.reshape(g, tc, tc) + skip

    n_row = np.repeat(np.arange(t, dtype=np.float32), c)[None, :, None]
    ends = []
    for dirn, n in ((0, t - 1 - n_row), (1, n_row)):
        pr, pi = a_pow(n, lr[dirn][:, None, :], li[dirn][:, None, :])
        b_r, b_i = jnp.tile(bbt_re[dirn], (1, t, 1)), jnp.tile(bbt_im[dirn], (1, t, 1))
        ends += [pr * b_r - pi * b_i, pr * b_i + pi * b_r]
    wend = jnp.concatenate(ends, axis=-1)

    half = g // 2
    z2 = jnp.zeros((half, tc, tc), F32)
    toep_p = jnp.concatenate([jnp.concatenate([toep[0::2], z2], axis=-1),
                              jnp.concatenate([z2, toep[1::2]], axis=-1)], axis=1)
    w4 = wend.reshape(g, tc, 4, p)
    z4 = jnp.zeros((half, tc, 4, p), F32)
    wend_p = jnp.concatenate([jnp.stack([w4[0::2], z4], axis=3).reshape(half, tc, 8 * p),
                              jnp.stack([z4, w4[1::2]], axis=3).reshape(half, tc, 8 * p)], axis=1)
    wc4 = jnp.concatenate([wc_f, wc_r], axis=1).reshape(g, 4, p, tc)
    zc = jnp.zeros((half, 4, p, tc), F32)
    wc_p = jnp.stack([jnp.concatenate([wc4[0::2], zc], axis=-1),
                      jnp.concatenate([zc, wc4[1::2]], axis=-1)], axis=2).reshape(half, 8 * p, 2 * tc)
    a16 = jnp.stack([v for dirn in (0, 1) for v in a_pow(float(t), lr[dirn], li[dirn])], axis=0)
    return toep_p.astype(BF16), wend_p.astype(BF16), wc_p.astype(BF16), a16.reshape(4, g * p)


def _s5_mix(su_c, su_x, mats, ctx_out):
    toep_p, wend_p, wc_p, a16 = mats
    b, l, _ = su_c.shape
    s = su_x.shape[1]
    n_rows = (l + s) // S5_T
    halves = 2
    lanes = BW // halves
    ng = lanes // S5_GROUP
    pw = 2 * BW
    assert (s // S5_T) % 128 == 0 and (l // S5_T) % 8 == 0 and l // S5_T <= 128
    wspec = pl.BlockSpec((ng // 2, pw, pw), lambda bi, h: (h, 0, 0))
    out_specs = [pl.BlockSpec((1, s, lanes), lambda bi, h: (bi, 0, h))]
    out_shape = [jax.ShapeDtypeStruct((b, s, BW), F32)]
    if ctx_out:
        out_specs.append(pl.BlockSpec((1, l, lanes), lambda bi, h: (bi, 0, h)))
        out_shape.append(jax.ShapeDtypeStruct((b, l, BW), F32))
    res = pl.pallas_call(
        functools.partial(_s5_kernel, ctx_out=ctx_out),
        grid=(b, halves),
        in_specs=[pl.BlockSpec((1, s, lanes), lambda bi, h: (bi, 0, h)),
                  pl.BlockSpec((1, l, lanes), lambda bi, h: (bi, 0, h)),
                  wspec, wspec, wspec,
                  pl.BlockSpec((4, ng * S5_STATE), lambda bi, h: (0, h))],
        out_specs=out_specs,
        out_shape=out_shape,
        scratch_shapes=[pltpu.VMEM((ng * BW, s // S5_T), F32), pltpu.VMEM((ng * BW, 128), F32),
                        pltpu.VMEM((n_rows, ng * BW), BF16), pltpu.VMEM((n_rows, 4 * ng * S5_STATE), F32)],
        compiler_params=_cparams("arbitrary", "arbitrary"),
        name="s5_mixer",
    )(su_x, su_c, wend_p, toep_p, wc_p, a16)
    return (res[1] if ctx_out else None), res[0]


def _merge_kernel(zg_ref, oa_ref, ob_ref, ys_ref, od_ref, x_ref, mod_ref, g_ref, wb_ref, wo_ref, gw_ref, gb_ref,
                  o_ref):
    d = x_ref.shape[1]
    y = jax.nn.gelu(ys_ref[...].astype(F32))
    oc = (y * _sigmoid(_dot(y.astype(BF16), gw_ref[...]) + gb_ref[...])).astype(BF16)
    outs = (oa_ref[...], ob_ref[...], oc, od_ref[...])
    m = None
    for i, o in enumerate(outs):
        term = (jnp.tanh(zg_ref[:, i * d:(i + 1) * d].astype(F32)) + 1.0) * _dot(o, wb_ref[i])
        m = term if m is None else m + term
    mo = _dot(m.astype(BF16), wo_ref[...])
    o_ref[...] = x_ref[...] + mod_ref[0, 2:3, :] * _rms(mo, g_ref[...])


def _merge(z2, o_a, o_b, y_s, o_d, x2, mod, g_post, wb, wo, glu_w, glu_b, tiles_per_mod, tm):
    m, d = x2.shape
    row = lambda i: (i, 0)
    return pl.pallas_call(
        _merge_kernel,
        grid=(m // tm,),
        in_specs=[pl.BlockSpec((tm, Z_GATES_W), row)] + [pl.BlockSpec((tm, BW), row)] * 4
        + [pl.BlockSpec((tm, d), row),
           pl.BlockSpec((1, 8, d), lambda i: (i // tiles_per_mod, 0, 0)),
           _const_spec((1, d)), _const_spec(wb.shape), _const_spec(wo.shape),
           _const_spec(glu_w.shape), _const_spec((1, BW))],
        out_specs=pl.BlockSpec((tm, d), row),
        out_shape=jax.ShapeDtypeStruct((m, d), F32),
        compiler_params=_cparams("arbitrary"),
        name="merge",
    )(z2, o_a, o_b, y_s, o_d, x2, mod, g_post.reshape(1, d), wb, wo, glu_w, glu_b.reshape(1, BW).astype(F32))


def _ffn_dense_kernel(x_ref, mod_ref, gpre_ref, gpost_ref, w1_ref, w3_ref, w2_ref, o_ref):
    x = x_ref[...]
    h = (_rms(x, gpre_ref[...]) * (1.0 + mod_ref[0, 4:5, :]) + mod_ref[0, 3:4, :]).astype(BF16)
    a = _dot(h, w1_ref[...])
    a = (a * _sigmoid(a)) * _dot(h, w3_ref[...])
    f = _dot(a.astype(BF16), w2_ref[...])
    o_ref[...] = x + mod_ref[0, 5:6, :] * _rms(f, gpost_ref[...])


def _ffn_dense(x2, mod, g_pre, g_post, w1, w3, w2, tiles_per_mod, tm):
    m, d = x2.shape
    return pl.pallas_call(
        _ffn_dense_kernel,
        grid=(m // tm,),
        in_specs=[pl.BlockSpec((tm, d), lambda i: (i, 0)),
                  pl.BlockSpec((1, 8, d), lambda i: (i // tiles_per_mod, 0, 0)),
                  _const_spec((1, d)), _const_spec((1, d)),
                  _const_spec(w1.shape), _const_spec(w3.shape), _const_spec(w2.shape)],
        out_specs=pl.BlockSpec((tm, d), lambda i: (i, 0)),
        out_shape=jax.ShapeDtypeStruct((m, d), F32),
        compiler_params=_cparams("arbitrary"),
        name="ffn_dense",
    )(x2, mod, g_pre.reshape(1, d), g_post.reshape(1, d), w1, w3, w2)


def _route_kernel(x_ref, mod_ref, gpre_ref, r_ref, h_ref, idx_ref, wt_ref, cnt_ref, tri_ref):
    tm = x_ref.shape[0]

    @pl.when(pl.program_id(0) == 0)
    def _():
        cnt_ref[...] = jnp.zeros_like(cnt_ref)
        earlier = lax.broadcasted_iota(jnp.int32, (tm, tm), 1) < lax.broadcasted_iota(jnp.int32, (tm, tm), 0)
        tri_ref[...] = jnp.where(earlier, 1.0, 0.0).astype(BF16)

    h = _rms(x_ref[...], gpre_ref[...]) * (1.0 + mod_ref[0, 4:5, :]) + mod_ref[0, 3:4, :]
    h_hi = h.astype(BF16)
    h_ref[...] = h_hi
    h_lo = (h - h_hi.astype(F32)).astype(BF16)
    a = _dot(h_hi, r_ref[...])
    logits = a + pltpu.roll(a, 128 - N_EXPERTS, 1) + _dot(h_lo, r_ref[...])
    lane = lax.broadcasted_iota(jnp.int32, logits.shape, 1)
    logits = jnp.where(lane < N_EXPERTS, logits, 2.0 * NEG_INF)
    m0 = logits.max(axis=-1, keepdims=True)
    i0 = jnp.where(logits == m0, lane, 128).min(axis=-1, keepdims=True)
    rest = jnp.where(lane == i0, NEG_INF, logits)
    m1 = rest.max(axis=-1, keepdims=True)
    i1 = jnp.where(rest == m1, lane, 128).min(axis=-1, keepdims=True)
    e = jnp.exp(m1 - m0)
    w0 = 1.0 / (1.0 + e)
    wt_ref[...] = jnp.where(lane == 0, w0, jnp.where(lane == 1, e * w0, 0.0))
    oh0 = jnp.where(lane == i0, 1.0, 0.0)
    oh1 = jnp.where(lane == i1, 1.0, 0.0)
    both = oh0 + oh1
    seen = _dot(tri_ref[...], both.astype(BF16)) + cnt_ref[0:1, :]
    rank0 = (oh0 * seen).sum(axis=-1, keepdims=True).astype(jnp.int32)
    rank1 = (oh1 * seen).sum(axis=-1, keepdims=True).astype(jnp.int32)
    cnt_ref[0:1, :] = cnt_ref[0:1, :] + both.sum(axis=0, keepdims=True)
    idx_ref[...] = jnp.where(lane == 0, i0, jnp.where(lane == 1, i1, jnp.where(lane == 2, rank0,
                                                                               jnp.where(lane == 3, rank1, 0))))


def _route(x2, mod, g_pre, router, tiles_per_mod, tm):
    m, d = x2.shape
    r_hi = router.astype(BF16)
    r_lo = (router.astype(F32) - r_hi.astype(F32)).astype(BF16)
    r_pad = jnp.pad(jnp.concatenate([r_hi, r_lo], axis=1), ((0, 0), (0, 128 - 2 * N_EXPERTS)))
    row = lambda i: (i, 0)
    return pl.pallas_call(
        _route_kernel,
        grid=(m // tm,),
        in_specs=[pl.BlockSpec((tm, d), row),
                  pl.BlockSpec((1, 8, d), lambda i: (i // tiles_per_mod, 0, 0)),
                  _const_spec((1, d)), _const_spec((d, 128))],
        out_specs=[pl.BlockSpec((tm, d), row), pl.BlockSpec((tm, 128), row), pl.BlockSpec((tm, 128), row),
                   pl.BlockSpec((8, 128), lambda i: (0, 0))],
        out_shape=[jax.ShapeDtypeStruct((m, d), BF16), jax.ShapeDtypeStruct((m, 128), jnp.int32),
                   jax.ShapeDtypeStruct((m, 128), F32), jax.ShapeDtypeStruct((8, 128), F32)],
        scratch_shapes=[pltpu.VMEM((tm, tm), BF16)],
        compiler_params=_cparams("arbitrary"),
        name="moe_route",
    )(x2, mod, g_pre.reshape(1, d), r_pad)


def _ffn_grouped_kernel(te_ref, nv_ref, h_ref, w1_ref, w3_ref, w2_ref, *rest, tile_off):
    o_ref, acc_ref = rest[-2:]
    i, j = pl.program_id(0), pl.program_id(1)
    last = pl.num_programs(1) - 1

    @pl.when(j == 0)
    def _():
        acc_ref[...] = jnp.zeros_like(acc_ref)

    @pl.when(tile_off + i < nv_ref[0])
    def _():
        h = h_ref[...]
        a = _dot(h, w1_ref[0].astype(BF16))
        a = (a * _sigmoid(a)) * _dot(h, w3_ref[0].astype(BF16))
        acc_ref[...] += _dot(a.astype(BF16), w2_ref[0].astype(BF16))

    @pl.when(j == last)
    def _():
        o_ref[...] = acc_ref[...].astype(BF16)


def _ffn_grouped(h_chunk, tile_expert, n_valid, w1, w3, w2, prev, tile_off, total_rows, tm, fc):
    r, d = h_chunk.shape
    f = w1.shape[2]
    nj = f // fc

    def col(i, j, nv):
        return jnp.where(tile_off + i < nv[0], j, nj - 1)

    in_specs = [pl.BlockSpec((tm, d), lambda i, j, te, nv: (i, 0)),
                pl.BlockSpec((1, d, fc), lambda i, j, te, nv: (te[tile_off + i], 0, col(i, j, nv))),
                pl.BlockSpec((1, d, fc), lambda i, j, te, nv: (te[tile_off + i], 0, col(i, j, nv))),
                pl.BlockSpec((1, fc, d), lambda i, j, te, nv: (te[tile_off + i], col(i, j, nv), 0))]
    args = [tile_expert, n_valid, h_chunk, w1, w3, w2]
    aliases = {}
    if prev is not None:
        in_specs.append(pl.BlockSpec(memory_space=pl.ANY))
        aliases = {len(args): 0}
        args.append(prev)
    grid_spec = pltpu.PrefetchScalarGridSpec(
        num_scalar_prefetch=2,
        grid=(r // tm, nj),
        in_specs=in_specs,
        out_specs=pl.BlockSpec((tm, d), lambda i, j, te, nv: (tile_off + i, 0)),
        scratch_shapes=[pltpu.VMEM((tm, d), F32)],
    )
    return pl.pallas_call(
        functools.partial(_ffn_grouped_kernel, tile_off=tile_off),
        grid_spec=grid_spec,
        out_shape=jax.ShapeDtypeStruct((total_rows, d), BF16),
        input_output_aliases=aliases,
        compiler_params=_cparams("arbitrary", "arbitrary"),
        name="ffn_grouped",
    )(*args)


def _combine_kernel(o0_ref, o1_ref, wt_ref, x_ref, mod_ref, gpost_ref, o_ref):
    y = wt_ref[:, 0:1] * o0_ref[...].astype(F32) + wt_ref[:, 1:2] * o1_ref[...].astype(F32)
    o_ref[...] = x_ref[...] + mod_ref[0, 5:6, :] * _rms(y, gpost_ref[...])


def _combine(o0, o1, wts, x2, mod, g_post, tiles_per_mod, tm):
    m, d = x2.shape
    row = lambda i: (i, 0)
    return pl.pallas_call(
        _combine_kernel,
        grid=(m // tm,),
        in_specs=[pl.BlockSpec((tm, d), row), pl.BlockSpec((tm, d), row), pl.BlockSpec((tm, 128), row),
                  pl.BlockSpec((tm, d), row),
                  pl.BlockSpec((1, 8, d), lambda i: (i // tiles_per_mod, 0, 0)),
                  _const_spec((1, d))],
        out_specs=pl.BlockSpec((tm, d), row),
        out_shape=jax.ShapeDtypeStruct((m, d), F32),
        compiler_params=_cparams("arbitrary"),
        name="moe_combine",
    )(o0, o1, wts, x2, mod, g_post.reshape(1, d))


def _moe(x2, mod, g_pre, g_post, router, w1, w3, w2, tiles_per_mod, tm):
    m, d = x2.shape
    h, idx, wts, cnt = _route(x2, mod, g_pre, router, tiles_per_mod, tm)
    gt = MOE_GROUP_TILE
    idx_t = idx[:, :4].T
    expert, rank = idx_t[:2], idx_t[2:]
    counts = cnt[0, :N_EXPERTS].astype(jnp.int32)
    padded = (counts + gt - 1) // gt * gt
    ends = jnp.cumsum(padded)
    starts = ends - padded
    pos = rank + sum(jnp.where(expert == e, starts[e], 0) for e in range(N_EXPERTS))
    n_tiles = (2 * m) // gt + N_EXPERTS
    token = jnp.tile(jnp.arange(m, dtype=jnp.int32), 2)
    src = jnp.zeros((n_tiles * gt,), jnp.int32).at[pos.reshape(-1)].set(token, unique_indices=True,
                                                                        mode='promise_in_bounds')
    tile_start = jnp.arange(n_tiles, dtype=jnp.int32) * gt
    tile_expert = jnp.minimum((ends[None, :] <= tile_start[:, None]).sum(axis=1), N_EXPERTS - 1).astype(jnp.int32)
    n_valid = (ends[-1] // gt).astype(jnp.int32).reshape(1)
    rows = lambda a, i: a.at[i].get(mode='promise_in_bounds')
    n_chunks = math.gcd(n_tiles, MOE_DISPATCH_CHUNKS)
    chunk_tiles = n_tiles // n_chunks
    out = jnp.zeros((n_tiles * gt, d), BF16)
    for ci in range(n_chunks):
        src_c = src[ci * chunk_tiles * gt:(ci + 1) * chunk_tiles * gt]
        out = _ffn_grouped(rows(h, src_c), tile_expert, n_valid, w1, w3, w2, out, ci * chunk_tiles, n_tiles * gt,
                           gt, MOE_F_CHUNK)
    return _combine(rows(out, pos[0]), rows(out, pos[1]), wts, x2, mod, g_post, tiles_per_mod, tm)


def _z_weights(w_in):
    d = w_in.shape[0]
    hw = N_HEADS * HEAD_DIM
    kvw = SWA_KV_HEADS * HEAD_DIM
    sizes = (hw, hw, hw, hw, kvw, kvw, BW, BW, BW, Z_GATES_W)
    qa, ka, va, qd, kd, vd, gu, gv, su, gates = jnp.split(w_in, np.cumsum(sizes)[:-1].tolist(), axis=1)
    rep = N_HEADS // SWA_KV_HEADS
    dup = lambda w: jnp.repeat(w.reshape(d, SWA_KV_HEADS, 1, HEAD_DIM), rep, axis=2).reshape(d, hw)
    return jnp.concatenate([0.5 * gates, qa, ka, va, qd, dup(kd), dup(vd), gu, gv, su], axis=1).astype(BF16)


def kernel(x, c, ctx, c_ctx, w_mod, b_mod, g_pre_mix, g_post_mix, g_pre_ffn, g_post_ffn, w_in, na_rpb, swa_sink, gmlp_ln_g, gmlp_ln_b, gmlp_ws, gmlp_bs, s5_a_re, s5_a_im, s5_log_dt, s5_b_re, s5_b_im, s5_c_re, s5_c_im, s5_d, s5_glu_w, s5_glu_b, w_branch, w_out, ffn_w1, ffn_w3, ffn_w2, moe_router, moe_w1, moe_w3, moe_w2):
    bsz, s, d = x.shape
    l = ctx.shape[1]
    depth = w_mod.shape[0]
    tm = 512
    rows = s // GRID_W

    c_all = jnp.zeros((8, d), F32).at[:bsz].set(c).at[bsz].set(c_ctx)
    mods = _modulation(c_all, w_mod, b_mod).reshape(depth, 8, 6, d)
    mods = jnp.pad(mods, ((0, 0), (0, 0), (0, 2), (0, 0)))
    rope_tabs = _rope_tables(s)
    swa_bias = _swa_bias(s)
    w_z_all = jax.vmap(_z_weights)(w_in)
    na_bias_all = jax.vmap(functools.partial(_na_bias, rows=rows))(na_rpb)
    s5_all = jax.vmap(_s5_matrices)(s5_a_re, s5_a_im, s5_log_dt, s5_b_re, s5_b_im, s5_c_re, s5_c_im, s5_d)
    wb_all, wo_all, glu_all = (0.5 * w_branch).astype(BF16), w_out.astype(BF16), s5_glu_w.astype(BF16)
    sink_all = jnp.repeat(swa_sink.astype(F32) * LOG2E, ATT_TQ, axis=1).reshape(depth, 1, N_HEADS * ATT_TQ)

    x2 = x.reshape(bsz * s, d)
    c2 = ctx.reshape(bsz * l, d)
    for i in range(depth):
        ctx_out = i < depth - 1
        mod_x = mods[i, :bsz]
        mod_c = mods[i, bsz:bsz + 1]
        w_z = w_z_all[i]
        zx, su_x = _project(x2, mod_x, g_pre_mix[i], w_z, s // tm, rope_tabs, tm)
        zc, su_c = _project(c2, mod_c, g_pre_mix[i], w_z, (bsz * l) // tm, None, tm)
        zx3 = zx.reshape(bsz, s, ZW)
        zc3 = zc.reshape(bsz, l, ZW)
        sink_rows = sink_all[i]

        nq = s // ATT_TQ
        o_a = _local_attention(zx3, zc3, Z_QA, Z_KA, Z_VA, na_bias_all[i], None, ATT_TQ,
                               lambda qi, j: jnp.clip(qi - 1, 0, nq - 3) + j)
        o_d = _local_attention(zx3, zc3, Z_QD, Z_KD, Z_VD, swa_bias, sink_rows, SWA_WINDOW,
                               functools.partial(_swa_key_index, n_blocks=s // SWA_WINDOW))
        o_b = _gmlp(zx, gmlp_ln_g[i], gmlp_ln_b[i], gmlp_ws[i], gmlp_bs[i], tm)
        s5_mats = tuple(v[i] for v in s5_all)
        y_c, y_x = _s5_mix(su_c.reshape(bsz, l, BW), su_x.reshape(bsz, s, BW), s5_mats, ctx_out)

        wb, wo, glu_w = wb_all[i], wo_all[i], glu_all[i]
        x2 = _merge(zx, o_a.reshape(bsz * s, BW), o_b, y_x.reshape(bsz * s, BW), o_d.reshape(bsz * s, BW),
                    x2, mod_x, g_post_mix[i], wb, wo, glu_w, s5_glu_b[i], s // tm, tm)
        if ctx_out:
            o_a_c = _ctx_attention(zc3, Z_QA, Z_KA, Z_VA, None)
            o_d_c = _ctx_attention(zc3, Z_QD, Z_KD, Z_VD, sink_rows)
            o_b_c = _gmlp(zc, gmlp_ln_g[i], gmlp_ln_b[i], gmlp_ws[i], gmlp_bs[i], tm)
            c2 = _merge(zc, o_a_c.reshape(bsz * l, BW), o_b_c, y_c.reshape(bsz * l, BW), o_d_c.reshape(bsz * l, BW),
                        c2, mod_c, g_post_mix[i], wb, wo, glu_w, s5_glu_b[i], (bsz * l) // tm, tm)

        j = i // 2
        if i % 2 == 0:
            w1, w3, w2 = ffn_w1[j].astype(BF16), ffn_w3[j].astype(BF16), ffn_w2[j].astype(BF16)
            x2 = _ffn_dense(x2, mod_x, g_pre_ffn[i], g_post_ffn[i], w1, w3, w2, s // tm, tm)
            if ctx_out:
                c2 = _ffn_dense(c2, mod_c, g_pre_ffn[i], g_post_ffn[i], w1, w3, w2, (bsz * l) // tm, tm)
        else:
            w1, w3, w2 = moe_w1[j], moe_w3[j], moe_w2[j]
            x2 = _moe(x2, mod_x, g_pre_ffn[i], g_post_ffn[i], moe_router[j], w1, w3, w2, s // tm, tm)
            if ctx_out:
                c2 = _moe(c2, mod_c, g_pre_ffn[i], g_post_ffn[i], moe_router[j], w1, w3, w2, (bsz * l) // tm, tm)
    return x2.reshape(bsz, s, d)
```

```python
import functools
import math

import numpy as np
import jax
import jax.numpy as jnp
from jax import lax
from jax.experimental import pallas as pl
from jax.experimental.pallas import tpu as pltpu

F32 = jnp.float32
BF16 = jnp.bfloat16
HIGHEST = lax.Precision.HIGHEST

GRID_W = 64
HEAD_DIM = 64
N_HEADS = 4
NA_ROWS = 8
NA_COLS = 16
SWA_KV_HEADS = 2
SWA_WINDOW = 128
GMLP_GROUPS = 4
GMLP_CHUNK = 128
S5_GROUP = 16
S5_GROUPS = 16
S5_STATE = 64
N_EXPERTS = 8
ROPE_BASE = 10000.0
EPS = 1e-6
NEG_INF = -1e30
LOG2E = math.log2(math.e)
Q_SCALE = HEAD_DIM ** -0.5 * LOG2E

BW = 256
S5_T = 16
S5_ROWS = 8
Z_GATES_W = 4096
(Z_QA, Z_KA, Z_VA, Z_QD, Z_KD, Z_VD, Z_GU, Z_GV, Z_SU) = range(Z_GATES_W // BW, Z_GATES_W // BW + 9)
ZW = Z_GATES_W + 9 * BW

V7X_VMEM_LIMIT = 56 * 1024 * 1024
ATT_TQ = 256
ONES_ROWS = 16
ATT_CHAIN_HEADS = 4
ATT_TILES_PER_STEP = 4
SWA_LOCAL_BLOCKS = ATT_TQ // SWA_WINDOW + 2
MOE_GROUP_TILE = 1024
MOE_F_CHUNK = 512
MOE_DISPATCH_CHUNKS = 4


def _cparams(*sem):
    return pltpu.CompilerParams(dimension_semantics=sem, vmem_limit_bytes=V7X_VMEM_LIMIT)


def _const_spec(shape):
    nd = len(shape)
    return pl.BlockSpec(shape, lambda *_: (0,) * nd, pipeline_mode=pl.Buffered(1))


def _dot(a, b):
    return jnp.dot(a, b, preferred_element_type=F32)


def _dot_nt(a, b):
    return lax.dot_general(a, b, (((1,), (1,)), ((), ())), preferred_element_type=F32)


def _sigmoid(x):
    return 0.5 * jnp.tanh(0.5 * x) + 0.5


def _rms(x, g):
    return x * lax.rsqrt(jnp.mean(x * x, axis=-1, keepdims=True) + EPS) * g


def _mod_kernel(c_ref, w_ref, b_ref, o_ref):
    c = c_ref[...]
    a = c * _sigmoid(c)
    o_ref[0] = jnp.dot(a, w_ref[0], preferred_element_type=F32, precision=HIGHEST) + b_ref[0]


def _modulation(c_all, w_mod, b_mod):
    depth, d, n = w_mod.shape
    tn = 1536
    return pl.pallas_call(
        _mod_kernel,
        grid=(depth, n // tn),
        in_specs=[pl.BlockSpec((8, d), lambda l, j: (0, 0)),
                  pl.BlockSpec((1, d, tn), lambda l, j: (l, 0, j)),
                  pl.BlockSpec((1, 1, tn), lambda l, j: (l, 0, j))],
        out_specs=pl.BlockSpec((1, 8, tn), lambda l, j: (l, 0, j)),
        out_shape=jax.ShapeDtypeStruct((depth, 8, n), F32),
        compiler_params=_cparams("arbitrary", "arbitrary"),
        name="modulation",
    )(c_all, w_mod, b_mod.reshape(depth, 1, n))


def _proj_kernel(*refs, rope, n_chunk):
    if rope:
        x_ref, mod_ref, g_ref, w_ref, cos_ref, sin_ref, o_ref, su_ref = refs
    else:
        x_ref, mod_ref, g_ref, w_ref, o_ref, su_ref = refs
    h = _rms(x_ref[...], g_ref[...]) * (1.0 + mod_ref[0, 1:2, :]) + mod_ref[0, 0:1, :]
    h = h.astype(BF16)
    rope_blocks = (Z_QD, Z_KD) if rope else ()
    for j in range(ZW // n_chunk):
        lo = j * n_chunk
        r = _dot(h, w_ref[:, lo:lo + n_chunk])
        blocks = range(lo // BW, (lo + n_chunk) // BW)
        if Z_SU in blocks:
            su_ref[...] = r[:, Z_SU * BW - lo:(Z_SU + 1) * BW - lo]
        if not any(b in rope_blocks or b in (Z_QA, Z_QD) for b in blocks):
            o_ref[:, lo:lo + n_chunk] = r.astype(BF16)
            continue
        for b in blocks:
            t = r[:, b * BW - lo:(b + 1) * BW - lo]
            if b in rope_blocks:
                lane = lax.broadcasted_iota(jnp.int32, (1, BW), 1)
                low_half = (lane % (HEAD_DIM // 2)) < (HEAD_DIM // 4)
                partner = jnp.where(low_half, pltpu.roll(t, BW - HEAD_DIM // 4, 1),
                                    pltpu.roll(t, HEAD_DIM // 4, 1))
                t = t * cos_ref[...] + partner * sin_ref[...]
            if b in (Z_QA, Z_QD):
                t = t * Q_SCALE
            o_ref[:, b * BW:(b + 1) * BW] = t.astype(BF16)


def _project(x2, mod, g, w_z, tiles_per_mod, rope_tabs, tm):
    m, d = x2.shape
    rope = rope_tabs is not None
    in_specs = [pl.BlockSpec((tm, d), lambda i: (i, 0)),
                pl.BlockSpec((1, 8, d), lambda i: (i // tiles_per_mod, 0, 0)),
                _const_spec((1, d)),
                _const_spec((d, ZW))]
    args = [x2, mod, g.reshape(1, d), w_z]
    if rope:
        n_rt = rope_tabs[0].shape[0] // tm
        in_specs += [pl.BlockSpec((tm, BW), lambda i: (i % n_rt, 0))] * 2
        args += list(rope_tabs)
    return pl.pallas_call(
        functools.partial(_proj_kernel, rope=rope, n_chunk=1280),
        grid=(m // tm,),
        in_specs=in_specs,
        out_specs=[pl.BlockSpec((tm, ZW), lambda i: (i, 0)), pl.BlockSpec((tm, BW), lambda i: (i, 0))],
        out_shape=[jax.ShapeDtypeStruct((m, ZW), BF16), jax.ShapeDtypeStruct((m, BW), F32)],
        compiler_params=_cparams("arbitrary"),
        name="project_in",
    )(*args)


def _attn_kernel(*refs, n_local, use_sink, n_tiles):
    per_tile = 1 + 2 * n_local + (1 if n_local else 0)
    tiles = [refs[t * per_tile:(t + 1) * per_tile] for t in range(n_tiles)]
    pos = n_tiles * per_tile
    kc_ref, vc_ref = refs[pos], refs[pos + 1]
    pos += 2
    sink_ref = None
    if use_sink:
        sink_ref = refs[pos]
        pos += 1
    o_ref = refs[pos]

    tq = tiles[0][0].shape[1]
    lane_head = lax.broadcasted_iota(jnp.int32, (1, BW), 1) // HEAD_DIM
    vc_t = vc_ref[0].T
    chains = [range(h0, h0 + ATT_CHAIN_HEADS) for h0 in range(0, N_HEADS, ATT_CHAIN_HEADS)]

    def score_stage(tile, heads):
        q = tile[0][0]
        cols = slice(heads[0] * tq, (heads[-1] + 1) * tq)
        q_heads = jnp.concatenate([jnp.where(lane_head == h, q, jnp.zeros_like(q)) for h in heads], axis=0)
        scores = []
        for j in range(n_local):
            k_ref = tile[1 + 2 * j]
            kb = k_ref.shape[1]
            scores.append(_dot_nt(k_ref[0], q_heads) + tile[-1][0, j * kb:(j + 1) * kb, cols])
        scores.append(_dot_nt(kc_ref[0], q_heads))
        return scores

    def softmax_value_stage(tile, heads, scores):
        cols = slice(heads[0] * tq, (heads[-1] + 1) * tq)
        v_t = [tile[2 + 2 * j][0].T for j in range(n_local)] + [vc_t]
        mx = scores[0].max(axis=0, keepdims=True)
        for s in scores[1:]:
            mx = jnp.maximum(mx, s.max(axis=0, keepdims=True))
        if use_sink:
            mx = jnp.maximum(mx, sink_ref[:, cols])
        o = [jnp.zeros((HEAD_DIM + ONES_ROWS, tq), F32) for _ in heads]
        for s, vt in zip(scores, v_t):
            p = jnp.exp2(s - mx).astype(BF16)
            ones = jnp.ones((ONES_ROWS, vt.shape[1]), BF16)
            for i, h in enumerate(heads):
                lhs = jnp.concatenate([vt[h * HEAD_DIM:(h + 1) * HEAD_DIM, :], ones], axis=0)
                o[i] = o[i] + _dot(lhs, p[:, i * tq:(i + 1) * tq])
        res = []
        for i, h in enumerate(heads):
            den = o[i][HEAD_DIM:HEAD_DIM + 1, :]
            if use_sink:
                den = den + jnp.exp2(sink_ref[:, h * tq:(h + 1) * tq] - mx[:, i * tq:(i + 1) * tq])
            res.append(o[i][:HEAD_DIM, :] / den)
        return res

    work = [(t, heads) for heads in chains for t in range(n_tiles)]
    outs = {}
    pending = score_stage(tiles[work[0][0]], work[0][1])
    for c, (t, heads) in enumerate(work):
        scores = pending
        if c + 1 < len(work):
            pending = score_stage(tiles[work[c + 1][0]], work[c + 1][1])
        outs[(t, heads[0])] = softmax_value_stage(tiles[t], heads, scores)
    for t in range(n_tiles):
        o_t = jnp.concatenate([r for heads in chains for r in outs[(t, heads[0])]], axis=0)
        o_ref[0, t * tq:(t + 1) * tq, :] = o_t.T.astype(BF16)


def _local_attention(z, z_c, q_col, k_col, v_col, bias, sink_rows, key_block, key_index):
    b, s, _ = z.shape
    l = z_c.shape[1]
    tq = ATT_TQ
    nq = s // tq
    nt = ATT_TILES_PER_STEP
    n_local = bias.shape[1] // key_block
    assert nq >= 3 and nq % nt == 0 and l == BW

    def pat(i):
        return jnp.where(i == 0, 0, jnp.where(i == nq - 1, 2, 1))

    in_specs, args = [], []
    for t in range(nt):
        tile = lambda i, t=t: i * nt + t
        in_specs.append(pl.BlockSpec((1, tq, BW), lambda bi, i, tile=tile: (bi, tile(i), q_col)))
        args.append(z)
        for j in range(n_local):
            for col in (k_col, v_col):
                in_specs.append(pl.BlockSpec((1, key_block, BW),
                                             lambda bi, i, tile=tile, j=j, col=col: (bi, key_index(tile(i), j), col)))
                args.append(z)
        in_specs.append(pl.BlockSpec((1,) + bias.shape[1:], lambda bi, i, tile=tile: (pat(tile(i)), 0, 0)))
        args.append(bias)
    in_specs += [pl.BlockSpec((1, l, BW), lambda bi, i: (bi, 0, k_col)),
                 pl.BlockSpec((1, l, BW), lambda bi, i: (bi, 0, v_col))]
    args += [z_c, z_c]
    if sink_rows is not None:
        in_specs.append(_const_spec(sink_rows.shape))
        args.append(sink_rows)
    return pl.pallas_call(
        functools.partial(_attn_kernel, n_local=n_local, use_sink=sink_rows is not None, n_tiles=nt),
        grid=(b, nq // nt),
        in_specs=in_specs,
        out_specs=pl.BlockSpec((1, nt * tq, BW), lambda bi, i: (bi, i, 0)),
        out_shape=jax.ShapeDtypeStruct((b, s, BW), BF16),
        compiler_params=_cparams("arbitrary", "arbitrary"),
        name="local_attention",
    )(*args)


def _ctx_attention(z_c, q_col, k_col, v_col, sink_rows):
    b, l, _ = z_c.shape
    in_specs = [pl.BlockSpec((1, l, BW), lambda bi, c=col: (bi, 0, c)) for col in (q_col, k_col, v_col)]
    args = [z_c, z_c, z_c]
    if sink_rows is not None:
        in_specs.append(_const_spec(sink_rows.shape))
        args.append(sink_rows)
    return pl.pallas_call(
        functools.partial(_attn_kernel, n_local=0, use_sink=sink_rows is not None, n_tiles=1),
        grid=(b,),
        in_specs=in_specs,
        out_specs=pl.BlockSpec((1, l, BW), lambda bi: (bi, 0, 0)),
        out_shape=jax.ShapeDtypeStruct((b, l, BW), BF16),
        compiler_params=_cparams("arbitrary"),
        name="ctx_attention",
    )(*args)


def _na_bias(rpb, rows):
    tile_rows = ATT_TQ // GRID_W
    nq = rows // tile_rows
    col = np.arange(GRID_W)
    cs = np.clip(col - NA_COLS // 2, 0, GRID_W - NA_COLS)[:, None]
    kc = col[None, :]
    sel_c = ((kc - col[:, None] + NA_COLS - 1)[None] == np.arange(2 * NA_COLS - 1)[:, None, None]) \
        & ((kc >= cs) & (kc < cs + NA_COLS))[None]
    sel_r = []
    for i in (0, 1, nq - 1):
        base = min(max(i - 1, 0), nq - 3)
        qr = (tile_rows * i + np.arange(tile_rows))[:, None]
        kr = (tile_rows * base + np.arange(3 * tile_rows))[None, :]
        rs = np.clip(qr - NA_ROWS // 2, 0, rows - NA_ROWS)
        sel_r.append(((kr - qr + NA_ROWS - 1)[None] == np.arange(2 * NA_ROWS - 1)[:, None, None])
                     & ((kr >= rs) & (kr < rs + NA_ROWS))[None])
    sel_r = np.stack(sel_r).astype(np.float32)
    sel_c = sel_c.astype(np.float32)
    t1 = jnp.einsum('paqk,hab->phqkb', sel_r, rpb.astype(F32), precision=HIGHEST)
    bias = jnp.einsum('phqkb,bcd->pkdhqc', t1, sel_c, precision=HIGHEST)
    valid = np.einsum('paqk,bcd->pkdqc', sel_r, sel_c) > 0.5
    mask = np.where(valid, 0.0, NEG_INF).astype(np.float32)[:, :, :, None]
    return (bias * LOG2E + mask).reshape(3, 3 * ATT_TQ, rpb.shape[0] * ATT_TQ)


def _swa_key_index(i, j, n_blocks):
    return jnp.clip(i * (ATT_TQ // SWA_WINDOW) - 1 + j, 0, n_blocks - 1)


def _swa_bias(s):
    nq = s // ATT_TQ
    tq = np.arange(ATT_TQ)[:, None]
    tk = np.arange(SWA_LOCAL_BLOCKS * SWA_WINDOW)[None, :]
    out = []
    for i in (0, 1, nq - 1):
        kpos = ATT_TQ * i - SWA_WINDOW + tk
        valid = (np.abs(kpos - (ATT_TQ * i + tq)) <= SWA_WINDOW) & (kpos >= 0) & (kpos < s)
        out.append(np.tile(np.where(valid, 0.0, NEG_INF).astype(np.float32).T, (1, N_HEADS)))
    return jnp.asarray(np.stack(out))


def _rope_tables(s):
    nq = HEAD_DIM // 4
    t = jnp.arange(s)
    inv = ROPE_BASE ** (-jnp.arange(nq, dtype=F32) / nq)
    ang_r = (t // GRID_W).astype(F32)[:, None] * inv[None, :]
    ang_c = (t % GRID_W).astype(F32)[:, None] * inv[None, :]
    cos = jnp.concatenate([jnp.cos(ang_r)] * 2 + [jnp.cos(ang_c)] * 2, axis=-1)
    sin = jnp.concatenate([-jnp.sin(ang_r), jnp.sin(ang_r), -jnp.sin(ang_c), jnp.sin(ang_c)], axis=-1)
    return jnp.tile(cos, (1, N_HEADS)), jnp.tile(sin, (1, N_HEADS))


def _gmlp_kernel(u_ref, v_ref, g_ref, b_ref, w_ref, bs_ref, o_ref):
    lane_grp = lax.broadcasted_iota(jnp.int32, (1, BW), 1) // (BW // GMLP_GROUPS)
    for c in range(u_ref.shape[0] // GMLP_CHUNK):
        rows = slice(c * GMLP_CHUNK, (c + 1) * GMLP_CHUNK)
        u = jax.nn.gelu(u_ref[rows, :].astype(F32))
        v = jax.nn.gelu(v_ref[rows, :].astype(F32))
        mu = jnp.mean(v, axis=-1, keepdims=True)
        vc = v - mu
        v = vc * lax.rsqrt(jnp.mean(vc * vc, axis=-1, keepdims=True) + EPS) * g_ref[...] + b_ref[...]
        v = v.astype(BF16)
        stack = jnp.concatenate([jnp.where(lane_grp == g, v, jnp.zeros_like(v)) for g in range(GMLP_GROUPS)], axis=0)
        sg = _dot(w_ref[...], stack) + bs_ref[...]
        o_ref[rows, :] = (u * sg).astype(BF16)


def _gmlp(z2, ln_g, ln_b, ws, bs, tm):
    m = z2.shape[0]
    w_cat = jnp.concatenate([ws[g] for g in range(GMLP_GROUPS)], axis=1).astype(BF16)
    bs_l = jnp.repeat(bs.T.astype(F32), BW // GMLP_GROUPS, axis=1)
    return pl.pallas_call(
        _gmlp_kernel,
        grid=(m // tm,),
        in_specs=[pl.BlockSpec((tm, BW), lambda i: (i, Z_GU)),
                  pl.BlockSpec((tm, BW), lambda i: (i, Z_GV)),
                  _const_spec((1, BW)), _const_spec((1, BW)),
                  _const_spec(w_cat.shape), _const_spec(bs_l.shape)],
        out_specs=pl.BlockSpec((tm, BW), lambda i: (i, 0)),
        out_shape=jax.ShapeDtypeStruct((m, BW), BF16),
        compiler_params=_cparams("arbitrary"),
        name="gmlp",
    )(z2, z2, ln_g.reshape(1, BW).astype(F32), ln_b.reshape(1, BW).astype(F32), w_cat, bs_l)


def _s5_kernel(su_ref, suc_ref, wend_ref, toep_ref, wc_ref, a16_ref, *rest, ctx_out):
    if ctx_out:
        y_ref, yc_ref, ut_ref, utc_ref, up_ref, st_ref = rest
    else:
        y_ref, ut_ref, utc_ref, up_ref, st_ref = rest
    nj, nc = su_ref.shape[1] // S5_T, suc_ref.shape[1] // S5_T
    lanes = su_ref.shape[2]
    ng = lanes // S5_GROUP
    hw = ng * S5_STATE
    cpad = utc_ref.shape[1]

    def grp(g, t):
        return slice(g * BW + t * S5_GROUP, g * BW + (t + 1) * S5_GROUP)

    for t in range(S5_T):
        at = su_ref[0, pl.ds(t, nj, stride=S5_T), :].T
        ct = jnp.concatenate([suc_ref[0, pl.ds(t, nc, stride=S5_T), :],
                              jnp.zeros((cpad - nc, lanes), F32)], axis=0).T
        for g in range(ng):
            ut_ref[grp(g, t), :] = at[g * S5_GROUP:(g + 1) * S5_GROUP, :]
            utc_ref[grp(g, t), :] = ct[g * S5_GROUP:(g + 1) * S5_GROUP, :]

    for p in range(ng // 2):
        blocks = []
        for g in (2 * p, 2 * p + 1):
            xt = ut_ref[g * BW:(g + 1) * BW, :].T
            ct = utc_ref[g * BW:(g + 1) * BW, :].T[:nc]
            blocks.append(jnp.concatenate([ct, xt], axis=0))
        up = jnp.concatenate(blocks, axis=1).astype(BF16)
        up_ref[:, p * 2 * BW:(p + 1) * 2 * BW] = up
        e = _dot(up, wend_ref[p])
        for k in range(4):
            st_ref[:, k * hw + p * 128:k * hw + (p + 1) * 128] = e[:, k * 128:(k + 1) * 128]

    n_tiles, n_ctiles = (nc + nj) // 8, nc // 8
    a = [a16_ref[k:k + 1, :] for k in range(4)]

    def tile_step(k, carry):
        fr, fi, rr, ri = carry
        kf = pl.multiple_of(k * 8, 8)
        kr = jnp.where(k < n_ctiles, n_ctiles - 1 - k, n_tiles - 1 - (k - n_ctiles))
        kr = pl.multiple_of(kr * 8, 8)
        ef_r, ef_i = st_ref[pl.ds(kf, 8), 0:hw], st_ref[pl.ds(kf, 8), hw:2 * hw]
        er_r, er_i = st_ref[pl.ds(kr, 8), 2 * hw:3 * hw], st_ref[pl.ds(kr, 8), 3 * hw:4 * hw]
        hf_r, hf_i, hr_r, hr_i = [], [], [None] * 8, [None] * 8
        for r in range(8):
            hf_r.append(fr)
            hf_i.append(fi)
            fr, fi = a[0] * fr - a[1] * fi + ef_r[r:r + 1], a[0] * fi + a[1] * fr + ef_i[r:r + 1]
            q = 7 - r
            hr_r[q], hr_i[q] = rr, ri
            rr, ri = a[2] * rr - a[3] * ri + er_r[q:q + 1], a[2] * ri + a[3] * rr + er_i[q:q + 1]
        st_ref[pl.ds(kf, 8), 0:hw] = jnp.concatenate(hf_r, axis=0)
        st_ref[pl.ds(kf, 8), hw:2 * hw] = jnp.concatenate(hf_i, axis=0)
        st_ref[pl.ds(kr, 8), 2 * hw:3 * hw] = jnp.concatenate(hr_r, axis=0)
        st_ref[pl.ds(kr, 8), 3 * hw:4 * hw] = jnp.concatenate(hr_i, axis=0)
        return fr, fi, rr, ri

    zero = jnp.zeros((1, hw), F32)
    lax.fori_loop(0, n_tiles, tile_step, (zero, zero, zero, zero))

    for p in range(ng // 2):
        h = jnp.concatenate([st_ref[:, k * hw + p * 128:k * hw + (p + 1) * 128] for k in range(4)], axis=1)
        y = _dot(up_ref[:, p * 2 * BW:(p + 1) * 2 * BW], toep_ref[p]) + _dot(h.astype(BF16), wc_ref[p])
        for gg in range(2):
            g = 2 * p + gg
            yg = y[:, gg * BW:(gg + 1) * BW]
            ut_ref[g * BW:(g + 1) * BW, :] = yg[nc:, :].T
            if ctx_out:
                utc_ref[g * BW:(g + 1) * BW, :] = jnp.concatenate(
                    [yg[:nc, :], jnp.zeros((cpad - nc, BW), F32)], axis=0).T

    for t in range(S5_T):
        z = jnp.concatenate([ut_ref[grp(g, t), :] for g in range(ng)], axis=0)
        y_ref[0, pl.ds(t, nj, stride=S5_T), :] = z.T
        if ctx_out:
            zc = jnp.concatenate([utc_ref[grp(g, t), :] for g in range(ng)], axis=0)
            yc_ref[0, pl.ds(t, nc, stride=S5_T), :] = zc.T[:nc]


def _s5_matrices(a_re, a_im, log_dt, b_re, b_im, c_re, c_im, d):
    g, p, c, t = S5_GROUPS, S5_STATE, S5_GROUP, S5_T
    tc = t * c
    lam_re = jnp.minimum(a_re.astype(F32), -1e-4)
    lam_im = a_im.astype(F32)
    dt = jnp.exp(log_dt.astype(F32))[..., None]
    lr, li = lam_re * dt, lam_im * dt

    def a_pow(n, x_re, x_im):
        mag = jnp.exp(n * x_re)
        return mag * jnp.cos(n * x_im), mag * jnp.sin(n * x_im)

    ab_re, ab_im = a_pow(1.0, lr, li)
    den = lam_re * lam_re + lam_im * lam_im
    k_re = ((ab_re - 1.0) * lam_re + ab_im * lam_im) / den
    k_im = (ab_im * lam_re - (ab_re - 1.0) * lam_im) / den
    br, bi = b_re.astype(F32), b_im.astype(F32)
    bb_re = k_re[..., None] * br - k_im[..., None] * bi
    bb_im = k_re[..., None] * bi + k_im[..., None] * br
    bbt_re, bbt_im = jnp.swapaxes(bb_re, 2, 3), jnp.swapaxes(bb_im, 2, 3)
    ct_re = jnp.swapaxes(c_re.astype(F32), 2, 3)
    ct_im = jnp.swapaxes(c_im.astype(F32), 2, 3)

    n_up = np.repeat(np.arange(t + 1, dtype=np.float32), c)
    ca = []
    for dirn, n_lane in ((0, n_up), (1, t - n_up)):
        pr, pi = a_pow(n_lane[None, None, :], lr[dirn][:, :, None], li[dirn][:, :, None])
        cr_l = jnp.tile(ct_re[dirn], (1, 1, t + 1))
        ci_l = jnp.tile(ct_im[dirn], (1, 1, t + 1))
        ca.append(jnp.concatenate([cr_l * pr - ci_l * pi, -(cr_l * pi + ci_l * pr)], axis=1))
    wc_f, wc_r = ca[0][:, :, c:], ca[1][:, :, :tc]
    bb_cat = jnp.concatenate([bbt_re, bbt_im], axis=-1)
    lag_f = jnp.einsum('gik,gkl->gil', bb_cat[0], ca[0][:, :, :tc], precision=HIGHEST)
    lag_r = jnp.einsum('gik,gkl->gil', bb_cat[1], ca[1][:, :, c:], precision=HIGHEST)
    zeros = jnp.zeros_like(lag_f)
    ext_f = jnp.concatenate([zeros, lag_f], axis=-1)
    ext_r = jnp.concatenate([lag_r, zeros], axis=-1)
    toep = jnp.stack([ext_f[:, :, tc - c * s:2 * tc - c * s] + ext_r[:, :, c * (t - 1 - s):c * (t - 1 - s) + tc]
                      for s in range(t)], axis=1)
    skip = jnp.eye(tc, dtype=F32)[None] * jnp.tile(d.astype(F32).reshape(g, 1, c), (1, 1, t))
    toep = toep.reshape(g, tc, tc) + skip

    n_row = np.repeat(np.arange(t, dtype=np.float32), c)[None, :, None]
    ends = []
    for dirn, n in ((0, t - 1 - n_row), (1, n_row)):
        pr, pi = a_pow(n, lr[dirn][:, None, :], li[dirn][:, None, :])
        b_r, b_i = jnp.tile(bbt_re[dirn], (1, t, 1)), jnp.tile(bbt_im[dirn], (1, t, 1))
        ends += [pr * b_r - pi * b_i, pr * b_i + pi * b_r]
    wend = jnp.concatenate(ends, axis=-1)

    half = g // 2
    z2 = jnp.zeros((half, tc, tc), F32)
    toep_p = jnp.concatenate([jnp.concatenate([toep[0::2], z2], axis=-1),
                              jnp.concatenate([z2, toep[1::2]], axis=-1)], axis=1)
    w4 = wend.reshape(g, tc, 4, p)
    z4 = jnp.zeros((half, tc, 4, p), F32)
    wend_p = jnp.concatenate([jnp.stack([w4[0::2], z4], axis=3).reshape(half, tc, 8 * p),
                              jnp.stack([z4, w4[1::2]], axis=3).reshape(half, tc, 8 * p)], axis=1)
    wc4 = jnp.concatenate([wc_f, wc_r], axis=1).reshape(g, 4, p, tc)
    zc = jnp.zeros((half, 4, p, tc), F32)
    wc_p = jnp.stack([jnp.concatenate([wc4[0::2], zc], axis=-1),
                      jnp.concatenate([zc, wc4[1::2]], axis=-1)], axis=2).reshape(half, 8 * p, 2 * tc)
    a16 = jnp.stack([v for dirn in (0, 1) for v in a_pow(float(t), lr[dirn], li[dirn])], axis=0)
    return toep_p.astype(BF16), wend_p.astype(BF16), wc_p.astype(BF16), a16.reshape(4, g * p)


def _s5_mix(su_c, su_x, mats, ctx_out):
    toep_p, wend_p, wc_p, a16 = mats
    b, l, _ = su_c.shape
    s = su_x.shape[1]
    n_rows = (l + s) // S5_T
    halves = 2
    lanes = BW // halves
    ng = lanes // S5_GROUP
    pw = 2 * BW
    assert (s // S5_T) % 128 == 0 and (l // S5_T) % 8 == 0 and l // S5_T <= 128
    wspec = pl.BlockSpec((ng // 2, pw, pw), lambda bi, h: (h, 0, 0))
    out_specs = [pl.BlockSpec((1, s, lanes), lambda bi, h: (bi, 0, h))]
    out_shape = [jax.ShapeDtypeStruct((b, s, BW), F32)]
    if ctx_out:
        out_specs.append(pl.BlockSpec((1, l, lanes), lambda bi, h: (bi, 0, h)))
        out_shape.append(jax.ShapeDtypeStruct((b, l, BW), F32))
    res = pl.pallas_call(
        functools.partial(_s5_kernel, ctx_out=ctx_out),
        grid=(b, halves),
        in_specs=[pl.BlockSpec((1, s, lanes), lambda bi, h: (bi, 0, h)),
                  pl.BlockSpec((1, l, lanes), lambda bi, h: (bi, 0, h)),
                  wspec, wspec, wspec,
                  pl.BlockSpec((4, ng * S5_STATE), lambda bi, h: (0, h))],
        out_specs=out_specs,
        out_shape=out_shape,
        scratch_shapes=[pltpu.VMEM((ng * BW, s // S5_T), F32), pltpu.VMEM((ng * BW, 128), F32),
                        pltpu.VMEM((n_rows, ng * BW), BF16), pltpu.VMEM((n_rows, 4 * ng * S5_STATE), F32)],
        compiler_params=_cparams("arbitrary", "arbitrary"),
        name="s5_mixer",
    )(su_x, su_c, wend_p, toep_p, wc_p, a16)
    return (res[1] if ctx_out else None), res[0]


def _merge_kernel(zg_ref, oa_ref, ob_ref, ys_ref, od_ref, x_ref, mod_ref, g_ref, wb_ref, wo_ref, gw_ref, gb_ref,
                  o_ref):
    d = x_ref.shape[1]
    y = jax.nn.gelu(ys_ref[...].astype(F32))
    oc = (y * _sigmoid(_dot(y.astype(BF16), gw_ref[...]) + gb_ref[...])).astype(BF16)
    outs = (oa_ref[...], ob_ref[...], oc, od_ref[...])
    m = None
    for i, o in enumerate(outs):
        term = (jnp.tanh(zg_ref[:, i * d:(i + 1) * d].astype(F32)) + 1.0) * _dot(o, wb_ref[i])
        m = term if m is None else m + term
    mo = _dot(m.astype(BF16), wo_ref[...])
    o_ref[...] = x_ref[...] + mod_ref[0, 2:3, :] * _rms(mo, g_ref[...])


def _merge(z2, o_a, o_b, y_s, o_d, x2, mod, g_post, wb, wo, glu_w, glu_b, tiles_per_mod, tm):
    m, d = x2.shape
    row = lambda i: (i, 0)
    return pl.pallas_call(
        _merge_kernel,
        grid=(m // tm,),
        in_specs=[pl.BlockSpec((tm, Z_GATES_W), row)] + [pl.BlockSpec((tm, BW), row)] * 4
        + [pl.BlockSpec((tm, d), row),
           pl.BlockSpec((1, 8, d), lambda i: (i // tiles_per_mod, 0, 0)),
           _const_spec((1, d)), _const_spec(wb.shape), _const_spec(wo.shape),
           _const_spec(glu_w.shape), _const_spec((1, BW))],
        out_specs=pl.BlockSpec((tm, d), row),
        out_shape=jax.ShapeDtypeStruct((m, d), F32),
        compiler_params=_cparams("arbitrary"),
        name="merge",
    )(z2, o_a, o_b, y_s, o_d, x2, mod, g_post.reshape(1, d), wb, wo, glu_w, glu_b.reshape(1, BW).astype(F32))


def _ffn_dense_kernel(x_ref, mod_ref, gpre_ref, gpost_ref, w1_ref, w3_ref, w2_ref, o_ref):
    x = x_ref[...]
    h = (_rms(x, gpre_ref[...]) * (1.0 + mod_ref[0, 4:5, :]) + mod_ref[0, 3:4, :]).astype(BF16)
    a = _dot(h, w1_ref[...])
    a = (a * _sigmoid(a)) * _dot(h, w3_ref[...])
    f = _dot(a.astype(BF16), w2_ref[...])
    o_ref[...] = x + mod_ref[0, 5:6, :] * _rms(f, gpost_ref[...])


def _ffn_dense(x2, mod, g_pre, g_post, w1, w3, w2, tiles_per_mod, tm):
    m, d = x2.shape
    return pl.pallas_call(
        _ffn_dense_kernel,
        grid=(m // tm,),
        in_specs=[pl.BlockSpec((tm, d), lambda i: (i, 0)),
                  pl.BlockSpec((1, 8, d), lambda i: (i // tiles_per_mod, 0, 0)),
                  _const_spec((1, d)), _const_spec((1, d)),
                  _const_spec(w1.shape), _const_spec(w3.shape), _const_spec(w2.shape)],
        out_specs=pl.BlockSpec((tm, d), lambda i: (i, 0)),
        out_shape=jax.ShapeDtypeStruct((m, d), F32),
        compiler_params=_cparams("arbitrary"),
        name="ffn_dense",
    )(x2, mod, g_pre.reshape(1, d), g_post.reshape(1, d), w1, w3, w2)


def _route_kernel(x_ref, mod_ref, gpre_ref, r_ref, h_ref, idx_ref, wt_ref, cnt_ref, buf_ref, tri_ref):
    tm = x_ref.shape[0]
    buf_ref[...] = jnp.zeros_like(buf_ref)

    @pl.when(pl.program_id(0) == 0)
    def _():
        cnt_ref[...] = jnp.zeros_like(cnt_ref)
        earlier = lax.broadcasted_iota(jnp.int32, (tm, tm), 1) < lax.broadcasted_iota(jnp.int32, (tm, tm), 0)
        tri_ref[...] = jnp.where(earlier, 1.0, 0.0).astype(BF16)

    h = _rms(x_ref[...], gpre_ref[...]) * (1.0 + mod_ref[0, 4:5, :]) + mod_ref[0, 3:4, :]
    h_hi = h.astype(BF16)
    h_ref[...] = h_hi
    h_lo = (h - h_hi.astype(F32)).astype(BF16)
    a = _dot(h_hi, r_ref[...])
    logits = a + pltpu.roll(a, 128 - N_EXPERTS, 1) + _dot(h_lo, r_ref[...])
    lane = lax.broadcasted_iota(jnp.int32, logits.shape, 1)
    logits = jnp.where(lane < N_EXPERTS, logits, 2.0 * NEG_INF)
    m0 = logits.max(axis=-1, keepdims=True)
    i0 = jnp.where(logits == m0, lane, 128).min(axis=-1, keepdims=True)
    rest = jnp.where(lane == i0, NEG_INF, logits)
    m1 = rest.max(axis=-1, keepdims=True)
    i1 = jnp.where(rest == m1, lane, 128).min(axis=-1, keepdims=True)
    e = jnp.exp(m1 - m0)
    w0 = 1.0 / (1.0 + e)
    wt_ref[...] = jnp.where(lane == 0, w0, jnp.where(lane == 1, e * w0, 0.0))
    oh0 = jnp.where(lane == i0, 1.0, 0.0)
    oh1 = jnp.where(lane == i1, 1.0, 0.0)
    both = oh0 + oh1
    seen = _dot(tri_ref[...], both.astype(BF16)) + cnt_ref[0:1, :]
    rank0 = (oh0 * seen).sum(axis=-1, keepdims=True).astype(jnp.int32)
    rank1 = (oh1 * seen).sum(axis=-1, keepdims=True).astype(jnp.int32)
    cnt_ref[0:1, :] = cnt_ref[0:1, :] + both.sum(axis=0, keepdims=True)
    idx_ref[...] = jnp.where(lane == 0, i0, jnp.where(lane == 1, i1, jnp.where(lane == 2, rank0,
                                                                               jnp.where(lane == 3, rank1, 0))))


def _route(x2, mod, g_pre, router, tiles_per_mod, tm, buf_rows):
    m, d = x2.shape
    buf_blk = buf_rows // (m // tm)
    assert buf_blk * (m // tm) == buf_rows and buf_blk % 16 == 0
    r_hi = router.astype(BF16)
    r_lo = (router.astype(F32) - r_hi.astype(F32)).astype(BF16)
    r_pad = jnp.pad(jnp.concatenate([r_hi, r_lo], axis=1), ((0, 0), (0, 128 - 2 * N_EXPERTS)))
    row = lambda i: (i, 0)
    return pl.pallas_call(
        _route_kernel,
        grid=(m // tm,),
        in_specs=[pl.BlockSpec((tm, d), row),
                  pl.BlockSpec((1, 8, d), lambda i: (i // tiles_per_mod, 0, 0)),
                  _const_spec((1, d)), _const_spec((d, 128))],
        out_specs=[pl.BlockSpec((tm, d), row), pl.BlockSpec((tm, 128), row), pl.BlockSpec((tm, 128), row),
                   pl.BlockSpec((8, 128), lambda i: (0, 0)), pl.BlockSpec((buf_blk, d), row)],
        out_shape=[jax.ShapeDtypeStruct((m, d), BF16), jax.ShapeDtypeStruct((m, 128), jnp.int32),
                   jax.ShapeDtypeStruct((m, 128), F32), jax.ShapeDtypeStruct((8, 128), F32),
                   jax.ShapeDtypeStruct((buf_rows, d), BF16)],
        scratch_shapes=[pltpu.VMEM((tm, tm), BF16)],
        compiler_params=_cparams("arbitrary"),
        name="moe_route",
    )(x2, mod, g_pre.reshape(1, d), r_pad)


def _ffn_grouped_kernel(te_ref, nv_ref, h_ref, w1_ref, w3_ref, w2_ref, *rest, tile_off):
    o_ref, acc_ref = rest[-2:]
    i, j = pl.program_id(0), pl.program_id(1)
    last = pl.num_programs(1) - 1

    @pl.when(j == 0)
    def _():
        acc_ref[...] = jnp.zeros_like(acc_ref)

    @pl.when(tile_off + i < nv_ref[0])
    def _():
        h = h_ref[...]
        a = _dot(h, w1_ref[0].astype(BF16))
        a = (a * _sigmoid(a)) * _dot(h, w3_ref[0].astype(BF16))
        acc_ref[...] += _dot(a.astype(BF16), w2_ref[0].astype(BF16))

    @pl.when(j == last)
    def _():
        o_ref[...] = acc_ref[...].astype(BF16)


def _ffn_grouped(h_chunk, tile_expert, n_valid, w1, w3, w2, prev, tile_off, total_rows, tm, fc):
    r, d = h_chunk.shape
    f = w1.shape[2]
    nj = f // fc

    def col(i, j, nv):
        return jnp.where(tile_off + i < nv[0], j, nj - 1)

    in_specs = [pl.BlockSpec((tm, d), lambda i, j, te, nv: (i, 0)),
                pl.BlockSpec((1, d, fc), lambda i, j, te, nv: (te[tile_off + i], 0, col(i, j, nv))),
                pl.BlockSpec((1, d, fc), lambda i, j, te, nv: (te[tile_off + i], 0, col(i, j, nv))),
                pl.BlockSpec((1, fc, d), lambda i, j, te, nv: (te[tile_off + i], col(i, j, nv), 0))]
    args = [tile_expert, n_valid, h_chunk, w1, w3, w2]
    aliases = {}
    if prev is not None:
        in_specs.append(pl.BlockSpec(memory_space=pl.ANY))
        aliases = {len(args): 0}
        args.append(prev)
    grid_spec = pltpu.PrefetchScalarGridSpec(
        num_scalar_prefetch=2,
        grid=(r // tm, nj),
        in_specs=in_specs,
        out_specs=pl.BlockSpec((tm, d), lambda i, j, te, nv: (tile_off + i, 0)),
        scratch_shapes=[pltpu.VMEM((tm, d), F32)],
    )
    return pl.pallas_call(
        functools.partial(_ffn_grouped_kernel, tile_off=tile_off),
        grid_spec=grid_spec,
        out_shape=jax.ShapeDtypeStruct((total_rows, d), BF16),
        input_output_aliases=aliases,
        compiler_params=_cparams("arbitrary", "arbitrary"),
        name="ffn_grouped",
    )(*args)


def _combine_kernel(o0_ref, o1_ref, wt_ref, x_ref, mod_ref, gpost_ref, o_ref):
    y = wt_ref[:, 0:1] * o0_ref[...].astype(F32) + wt_ref[:, 1:2] * o1_ref[...].astype(F32)
    o_ref[...] = x_ref[...] + mod_ref[0, 5:6, :] * _rms(y, gpost_ref[...])


def _combine(o0, o1, wts, x2, mod, g_post, tiles_per_mod, tm):
    m, d = x2.shape
    row = lambda i: (i, 0)
    return pl.pallas_call(
        _combine_kernel,
        grid=(m // tm,),
        in_specs=[pl.BlockSpec((tm, d), row), pl.BlockSpec((tm, d), row), pl.BlockSpec((tm, 128), row),
                  pl.BlockSpec((tm, d), row),
                  pl.BlockSpec((1, 8, d), lambda i: (i // tiles_per_mod, 0, 0)),
                  _const_spec((1, d))],
        out_specs=pl.BlockSpec((tm, d), row),
        out_shape=jax.ShapeDtypeStruct((m, d), F32),
        compiler_params=_cparams("arbitrary"),
        name="moe_combine",
    )(o0, o1, wts, x2, mod, g_post.reshape(1, d))


def _moe(x2, mod, g_pre, g_post, router, w1, w3, w2, tiles_per_mod, tm):
    m, d = x2.shape
    n_tiles = (2 * m) // MOE_GROUP_TILE + N_EXPERTS
    h, idx, wts, cnt, out = _route(x2, mod, g_pre, router, tiles_per_mod, tm, n_tiles * MOE_GROUP_TILE)
    gt = MOE_GROUP_TILE
    idx_t = idx[:, :4].T
    expert, rank = idx_t[:2], idx_t[2:]
    counts = cnt[0, :N_EXPERTS].astype(jnp.int32)
    padded = (counts + gt - 1) // gt * gt
    ends = jnp.cumsum(padded)
    starts = ends - padded
    pos = rank + sum(jnp.where(expert == e, starts[e], 0) for e in range(N_EXPERTS))
    token = jnp.tile(jnp.arange(m, dtype=jnp.int32), 2)
    src = jnp.zeros((n_tiles * gt,), jnp.int32).at[pos.reshape(-1)].set(token, unique_indices=True,
                                                                        mode='promise_in_bounds')
    tile_start = jnp.arange(n_tiles, dtype=jnp.int32) * gt
    tile_expert = jnp.minimum((ends[None, :] <= tile_start[:, None]).sum(axis=1), N_EXPERTS - 1).astype(jnp.int32)
    n_valid = (ends[-1] // gt).astype(jnp.int32).reshape(1)
    rows = lambda a, i: a.at[i].get(mode='promise_in_bounds')
    n_chunks = math.gcd(n_tiles, MOE_DISPATCH_CHUNKS)
    chunk_tiles = n_tiles // n_chunks
    for ci in range(n_chunks):
        src_c = src[ci * chunk_tiles * gt:(ci + 1) * chunk_tiles * gt]
        out = _ffn_grouped(rows(h, src_c), tile_expert, n_valid, w1, w3, w2, out, ci * chunk_tiles, n_tiles * gt,
                           gt, MOE_F_CHUNK)
    return _combine(rows(out, pos[0]), rows(out, pos[1]), wts, x2, mod, g_post, tiles_per_mod, tm)


def _z_weights(w_in):
    d = w_in.shape[0]
    hw = N_HEADS * HEAD_DIM
    kvw = SWA_KV_HEADS * HEAD_DIM
    sizes = (hw, hw, hw, hw, kvw, kvw, BW, BW, BW, Z_GATES_W)
    qa, ka, va, qd, kd, vd, gu, gv, su, gates = jnp.split(w_in, np.cumsum(sizes)[:-1].tolist(), axis=1)
    rep = N_HEADS // SWA_KV_HEADS
    dup = lambda w: jnp.repeat(w.reshape(d, SWA_KV_HEADS, 1, HEAD_DIM), rep, axis=2).reshape(d, hw)
    return jnp.concatenate([0.5 * gates, qa, ka, va, qd, dup(kd), dup(vd), gu, gv, su], axis=1).astype(BF16)


def kernel(x, c, ctx, c_ctx, w_mod, b_mod, g_pre_mix, g_post_mix, g_pre_ffn, g_post_ffn, w_in, na_rpb, swa_sink, gmlp_ln_g, gmlp_ln_b, gmlp_ws, gmlp_bs, s5_a_re, s5_a_im, s5_log_dt, s5_b_re, s5_b_im, s5_c_re, s5_c_im, s5_d, s5_glu_w, s5_glu_b, w_branch, w_out, ffn_w1, ffn_w3, ffn_w2, moe_router, moe_w1, moe_w3, moe_w2):
    bsz, s, d = x.shape
    l = ctx.shape[1]
    depth = w_mod.shape[0]
    tm = 512
    rows = s // GRID_W

    c_all = jnp.zeros((8, d), F32).at[:bsz].set(c).at[bsz].set(c_ctx)
    mods = _modulation(c_all, w_mod, b_mod).reshape(depth, 8, 6, d)
    mods = jnp.pad(mods, ((0, 0), (0, 0), (0, 2), (0, 0)))
    rope_tabs = _rope_tables(s)
    swa_bias = _swa_bias(s)
    w_z_all = jax.vmap(_z_weights)(w_in)
    na_bias_all = jax.vmap(functools.partial(_na_bias, rows=rows))(na_rpb)
    s5_all = jax.vmap(_s5_matrices)(s5_a_re, s5_a_im, s5_log_dt, s5_b_re, s5_b_im, s5_c_re, s5_c_im, s5_d)
    wb_all, wo_all, glu_all = (0.5 * w_branch).astype(BF16), w_out.astype(BF16), s5_glu_w.astype(BF16)
    sink_all = jnp.repeat(swa_sink.astype(F32) * LOG2E, ATT_TQ, axis=1).reshape(depth, 1, N_HEADS * ATT_TQ)

    x2 = x.reshape(bsz * s, d)
    c2 = ctx.reshape(bsz * l, d)
    for i in range(depth):
        ctx_out = i < depth - 1
        mod_x = mods[i, :bsz]
        mod_c = mods[i, bsz:bsz + 1]
        w_z = w_z_all[i]
        zx, su_x = _project(x2, mod_x, g_pre_mix[i], w_z, s // tm, rope_tabs, tm)
        zc, su_c = _project(c2, mod_c, g_pre_mix[i], w_z, (bsz * l) // tm, None, tm)
        zx3 = zx.reshape(bsz, s, ZW)
        zc3 = zc.reshape(bsz, l, ZW)
        sink_rows = sink_all[i]

        nq = s // ATT_TQ
        o_a = _local_attention(zx3, zc3, Z_QA, Z_KA, Z_VA, na_bias_all[i], None, ATT_TQ,
                               lambda qi, j: jnp.clip(qi - 1, 0, nq - 3) + j)
        o_d = _local_attention(zx3, zc3, Z_QD, Z_KD, Z_VD, swa_bias, sink_rows, SWA_WINDOW,
                               functools.partial(_swa_key_index, n_blocks=s // SWA_WINDOW))
        o_b = _gmlp(zx, gmlp_ln_g[i], gmlp_ln_b[i], gmlp_ws[i], gmlp_bs[i], tm)
        s5_mats = tuple(v[i] for v in s5_all)
        y_c, y_x = _s5_mix(su_c.reshape(bsz, l, BW), su_x.reshape(bsz, s, BW), s5_mats, ctx_out)

        wb, wo, glu_w = wb_all[i], wo_all[i], glu_all[i]
        x2 = _merge(zx, o_a.reshape(bsz * s, BW), o_b, y_x.reshape(bsz * s, BW), o_d.reshape(bsz * s, BW),
                    x2, mod_x, g_post_mix[i], wb, wo, glu_w, s5_glu_b[i], s // tm, tm)
        if ctx_out:
            o_a_c = _ctx_attention(zc3, Z_QA, Z_KA, Z_VA, None)
            o_d_c = _ctx_attention(zc3, Z_QD, Z_KD, Z_VD, sink_rows)
            o_b_c = _gmlp(zc, gmlp_ln_g[i], gmlp_ln_b[i], gmlp_ws[i], gmlp_bs[i], tm)
            c2 = _merge(zc, o_a_c.reshape(bsz * l, BW), o_b_c, y_c.reshape(bsz * l, BW), o_d_c.reshape(bsz * l, BW),
                        c2, mod_c, g_post_mix[i], wb, wo, glu_w, s5_glu_b[i], (bsz * l) // tm, tm)

        j = i // 2
        if i % 2 == 0:
            w1, w3, w2 = ffn_w1[j].astype(BF16), ffn_w3[j].astype(BF16), ffn_w2[j].astype(BF16)
            x2 = _ffn_dense(x2, mod_x, g_pre_ffn[i], g_post_ffn[i], w1, w3, w2, s // tm, tm)
            if ctx_out:
                c2 = _ffn_dense(c2, mod_c, g_pre_ffn[i], g_post_ffn[i], w1, w3, w2, (bsz * l) // tm, tm)
        else:
            w1, w3, w2 = moe_w1[j], moe_w3[j], moe_w2[j]
            x2 = _moe(x2, mod_x, g_pre_ffn[i], g_post_ffn[i], moe_router[j], w1, w3, w2, s // tm, tm)
            if ctx_out:
                c2 = _moe(c2, mod_c, g_pre_ffn[i], g_post_ffn[i], moe_router[j], w1, w3, w2, (bsz * l) // tm, tm)
    return x2.reshape(bsz, s, d)
```
